```python
import math, functools
import jax, jax.numpy as jnp
from jax import lax
import numpy as np

D_MODEL = 1024
BATCH = 4
SEQ = 4096
DEPTH = 1
DEC_BATCH = 32
DEC_SEQ = 1
PAST_LEN = 8192
PAGE_SIZE = 128

N_META = 16
NORM_EPS = 1e-6
RW_HEAD = 64
RW_DIM = D_MODEL // 2
RW_HEADS = RW_DIM // RW_HEAD
W_LORA = 64
A_LORA = 64
G_LORA = 128
RW_PROJ = 3 * RW_DIM + W_LORA + A_LORA + G_LORA
RW_LN_EPS = 64e-5
DA_DK = 64
DA_DV = 2 * DA_DK
DA_DIM = D_MODEL // 2
DA_HEADS = DA_DIM // DA_DV
DA_QK = DA_HEADS * 2 * DA_DK
DA_V = DA_HEADS * DA_DV
Q_BLOCK = 128
IN_WIDTH = RW_PROJ + 2 * DA_QK + DA_V + 2 * D_MODEL
N_KEYS = 128
N_EXPERTS = N_KEYS * N_KEYS
PEER_HEADS = 8
PEER_TOPK = 16
PEER_DQ = 256
PEER_DHALF = PEER_DQ // 2
PEER_BLOCK = 256

kernel_name = "rwkv7_diffattn_peer_hybrid_step"

F32 = jnp.float32


def rms_norm(x, g, eps=NORM_EPS):
    xf = x.astype(F32)
    y = xf * lax.rsqrt(jnp.mean(xf * xf, axis=-1, keepdims=True) + eps)
    return (y * g.astype(F32)).astype(x.dtype)


def alibi_slopes(n):
    return jnp.exp2(-8.0 * jnp.arange(1, n + 1, dtype=F32) / n)


def rwkv_branch(p, shift_prev, S0, lp):
    B, T, _ = p.shape
    p_prev = jnp.concatenate([shift_prev[:, None, :].astype(p.dtype), p[:, :-1]], axis=1)
    ps = (p + (p_prev - p) * lp['rw_mu']).astype(F32)
    r, k, v, wl, al, gl = jnp.split(ps, [RW_DIM, 2 * RW_DIM, 3 * RW_DIM, 3 * RW_DIM + W_LORA,
                                        3 * RW_DIM + W_LORA + A_LORA], axis=-1)
    w = -jax.nn.softplus(-(lp['rw_w0'] + jnp.tanh(wl) @ lp['rw_w2'])) - 0.5
    decay = jnp.exp(-jnp.exp(w))
    a = jax.nn.sigmoid(lp['rw_a0'] + al @ lp['rw_a2'])
    g = jax.nn.sigmoid(gl) @ lp['rw_g2']
    heads = lambda t: t.reshape(B, T, RW_HEADS, RW_HEAD)
    kk = heads(k * lp['rw_kk'])
    kk = kk / jnp.maximum(jnp.sqrt(jnp.sum(kk * kk, axis=-1, keepdims=True)), 1e-12)
    k = k * (1.0 + (a - 1.0) * lp['rw_ka'])
    rh, kh, vh, wh, ah = heads(r), heads(k), heads(v), heads(decay), heads(a)
    a_vec, b_vec = -kk, kk * ah

    def step(S, inp):
        r_t, w_t, k_t, v_t, a_t, b_t = inp
        sa = jnp.einsum('bhvk,bhk->bhv', S, a_t)
        S = S * w_t[:, :, None, :] + sa[..., None] * b_t[:, :, None, :] + v_t[..., None] * k_t[:, :, None, :]
        return S, jnp.einsum('bhvk,bhk->bhv', S, r_t)

    seq = tuple(jnp.moveaxis(t, 1, 0) for t in (rh, wh, kh, vh, a_vec, b_vec))
    S_fin, ys = lax.scan(step, S0.astype(F32), seq)
    y = jnp.moveaxis(ys, 0, 1)
    mu = jnp.mean(y, axis=-1, keepdims=True)
    var = jnp.mean(jnp.square(y - mu), axis=-1, keepdims=True)
    y = ((y - mu) * lax.rsqrt(var + RW_LN_EPS)).reshape(B, T, RW_DIM) * lp['rw_ln_g'] + lp['rw_ln_b']
    bonus = (jnp.sum(rh * kh * lp['rw_rk'], axis=-1, keepdims=True) * vh).reshape(B, T, RW_DIM)
    y = ((y + bonus) * g).astype(p.dtype)
    return y @ lp['w_rw_br'], p[:, -1], S_fin.astype(p.dtype)


def diff_attn_prompt(q, k, v, lam):
    B, T = q.shape[:2]
    nb = -(-T // Q_BLOCK)
    Tp = nb * Q_BLOCK
    pad = ((0, 0), (0, Tp - T), (0, 0), (0, 0), (0, 0))
    qp, kp, vp = jnp.pad(q, pad), jnp.pad(k, pad), jnp.pad(v, pad[:4])
    slopes = alibi_slopes(DA_HEADS)[None, :, None, None, None]
    kpos = jnp.arange(Tp)
    qblocks = jnp.moveaxis(qp.reshape(B, nb, Q_BLOCK, DA_HEADS, 2, DA_DK), 1, 0)
    starts = jnp.arange(nb) * Q_BLOCK

    def block(args):
        qb, start = args
        qpos = start + jnp.arange(Q_BLOCK)
        s = jnp.einsum('bqhcd,bkhcd->bhcqk', qb, kp).astype(F32) * (DA_DK ** -0.5)
        dist = (qpos[:, None] - kpos[None, :]).astype(F32)
        s = jnp.where(dist >= 0, s - slopes * dist, -jnp.inf)
        pr = jax.nn.softmax(s, axis=-1)
        att = pr[:, :, 0] - lam * pr[:, :, 1]
        return jnp.einsum('bhqk,bkhv->bqhv', att.astype(vp.dtype), vp)

    o = lax.map(block, (qblocks, starts))
    return jnp.moveaxis(o, 0, 1).reshape(B, Tp, DA_HEADS, DA_DV)[:, :T]


def diff_attn_sample(q, k, v, lam, k_past, v_past):
    S = q.shape[1]
    P = k_past.shape[1]
    scale = DA_DK ** -0.5
    slopes = alibi_slopes(DA_HEADS)[None, :, None, None, None]
    qpos = P + jnp.arange(S)
    s_past = jnp.einsum('bqhcd,bkhcd->bhcqk', q, k_past).astype(F32) * scale
    d_past = (qpos[:, None] - jnp.arange(P)[None, :]).astype(F32)
    s_past = s_past - slopes * d_past
    s_new = jnp.einsum('bqhcd,bkhcd->bhcqk', q, k).astype(F32) * scale
    d_new = (qpos[:, None] - (P + jnp.arange(S))[None, :]).astype(F32)
    s_new = jnp.where(d_new >= 0, s_new - slopes * d_new, -jnp.inf)
    pr = jax.nn.softmax(jnp.concatenate([s_past, s_new], axis=-1), axis=-1)
    att = (pr[:, :, 0] - lam * pr[:, :, 1]).astype(v.dtype)
    return (jnp.einsum('bhqk,bkhv->bqhv', att[..., :P], v_past)
            + jnp.einsum('bhqk,bkhv->bqhv', att[..., P:], v))


def peer_ffn(xn, lp):
    lead = xn.shape[:-1]
    x = xn.reshape(-1, D_MODEL)
    N = x.shape[0]
    q = (x @ lp['peer_wq']).reshape(N, PEER_HEADS, 2, PEER_DHALF)
    s = jnp.einsum('nhcd,ckd->nhck', q, lp['peer_subkeys']).astype(F32)
    v1, i1 = lax.top_k(s[:, :, 0], PEER_TOPK)
    v2, i2 = lax.top_k(s[:, :, 1], PEER_TOPK)
    cand = (v1[..., :, None] + v2[..., None, :]).reshape(N, PEER_HEADS, PEER_TOPK * PEER_TOPK)
    sc, ci = lax.top_k(cand, PEER_TOPK)
    e1 = jnp.take_along_axis(i1, ci // PEER_TOPK, axis=-1)
    e2 = jnp.take_along_axis(i2, ci % PEER_TOPK, axis=-1)
    idx = e1 * N_KEYS + e2
    gate = jax.nn.softmax(sc, axis=-1)
    nb = -(-N // PEER_BLOCK)
    npad = nb * PEER_BLOCK - N
    xr = jnp.pad(x, ((0, npad), (0, 0))).reshape(nb, PEER_BLOCK, D_MODEL)
    ir = jnp.pad(idx, ((0, npad), (0, 0), (0, 0))).reshape(nb, PEER_BLOCK, PEER_HEADS, PEER_TOPK)
    gr = jnp.pad(gate, ((0, npad), (0, 0), (0, 0))).reshape(nb, PEER_BLOCK, PEER_HEADS, PEER_TOPK)
    u_tab, v_tab = lp['peer_u'], lp['peer_v']

    def block(args):
        xb, ib, gb = args
        hid = jnp.einsum('td,thkd->thk', xb, u_tab[ib]).astype(F32)
        hid = jax.nn.gelu(hid, approximate=False) * gb
        return jnp.einsum('thk,thkd->td', hid.astype(xb.dtype), v_tab[ib])

    out = lax.map(block, (xr, ir, gr)).reshape(-1, D_MODEL)[:N]
    return out.reshape(*lead, D_MODEL)


def layer_forward(h, attn_core, shift0, S0, lp, lam_init):
    B, T, _ = h.shape
    xn = rms_norm(h, lp['norm1_g'])
    proj = xn @ lp['w_in']
    o1 = RW_PROJ
    o2 = o1 + DA_QK
    o3 = o2 + DA_QK
    o4 = o3 + DA_V
    o5 = o4 + D_MODEL
    p_rw, q, k, v, g_a, g_b = jnp.split(proj, [o1, o2, o3, o4, o5], axis=-1)
    y_rw, shift_last, S_fin = rwkv_branch(p_rw, shift0, S0, lp)
    q = rms_norm(q.reshape(B, T, DA_HEADS, 2, DA_DK), lp['da_qn_g'])
    k = rms_norm(k.reshape(B, T, DA_HEADS, 2, DA_DK), lp['da_kn_g'])
    v = v.reshape(B, T, DA_HEADS, DA_DV)
    lv = lp['da_lam'].astype(F32)
    lam = jnp.exp(jnp.sum(lv[0] * lv[1])) - jnp.exp(jnp.sum(lv[2] * lv[3])) + lam_init
    o = attn_core(q, k, v, lam)
    o = rms_norm(o, lp['da_subln_g']) * (1.0 - lam_init)
    y_da = o.reshape(B, T, DA_V) @ lp['w_da_br']
    mix = jax.nn.sigmoid(g_a) * y_rw + jax.nn.sigmoid(g_b) * y_da
    h = h + mix @ lp['w_out']
    h = h + peer_ffn(rms_norm(h, lp['norm2_g']), lp)
    return h, k, v, shift_last, S_fin


def setup_inputs(seed: int = 0) -> dict:
    key = jax.random.key(seed)
    ks = iter(jax.random.split(key, 48))
    nrm = lambda shape, scale: jax.random.normal(next(ks), shape, F32) * scale
    uni = lambda shape, lo, hi: jax.random.uniform(next(ks), shape, F32, lo, hi)
    n_pages = PAST_LEN // PAGE_SIZE
    n_used = DEC_BATCH * n_pages
    n_phys = n_used + max(1, n_used // 4)
    page_table = jax.random.permutation(next(ks), n_phys)[:n_used].reshape(DEC_BATCH, n_pages).astype(jnp.int32)
    return {
        'x_prompt': nrm((BATCH, SEQ, D_MODEL), 1.0),
        'x_sample': nrm((DEC_BATCH, DEC_SEQ, D_MODEL), 1.0),
        'cache_k': nrm((DEPTH, n_phys, PAGE_SIZE, DA_HEADS, 2, DA_DK), 1.0),
        'cache_v': nrm((DEPTH, n_phys, PAGE_SIZE, DA_HEADS, DA_DV), 1.0),
        'state_wkv': nrm((DEPTH, DEC_BATCH, RW_HEADS, RW_HEAD, RW_HEAD), 0.1),
        'state_shift': nrm((DEPTH, DEC_BATCH, RW_PROJ), 1.0),
        'page_table': page_table,
        'meta': nrm((N_META, D_MODEL), 1.0),
        'norm1_g': 1.0 + nrm((DEPTH, D_MODEL), 0.02),
        'w_in': nrm((DEPTH, D_MODEL, IN_WIDTH), D_MODEL ** -0.5),
        'rw_mu': uni((DEPTH, RW_PROJ), 0.0, 1.0),
        'rw_w0': uni((DEPTH, RW_DIM), -6.0, 1.0),
        'rw_w2': nrm((DEPTH, W_LORA, RW_DIM), 0.1 * W_LORA ** -0.5),
        'rw_a0': nrm((DEPTH, RW_DIM), 0.1),
        'rw_a2': nrm((DEPTH, A_LORA, RW_DIM), 0.1 * A_LORA ** -0.5),
        'rw_g2': nrm((DEPTH, G_LORA, RW_DIM), G_LORA ** -0.5),
        'rw_kk': 0.85 + nrm((DEPTH, RW_DIM), 0.02),
        'rw_ka': 1.0 + nrm((DEPTH, RW_DIM), 0.02),
        'rw_rk': nrm((DEPTH, RW_HEADS, RW_HEAD), 0.1),
        'rw_ln_g': 1.0 + nrm((DEPTH, RW_DIM), 0.02),
        'rw_ln_b': nrm((DEPTH, RW_DIM), 0.02),
        'w_rw_br': nrm((DEPTH, RW_DIM, D_MODEL), RW_DIM ** -0.5),
        'da_qn_g': 1.0 + nrm((DEPTH, DA_DK), 0.02),
        'da_kn_g': 1.0 + nrm((DEPTH, DA_DK), 0.02),
        'da_lam': nrm((DEPTH, 4, DA_DK), 0.1),
        'da_subln_g': 1.0 + nrm((DEPTH, DA_DV), 0.02),
        'w_da_br': nrm((DEPTH, DA_DIM, D_MODEL), DA_DIM ** -0.5),
        'w_out': nrm((DEPTH, D_MODEL, D_MODEL), D_MODEL ** -0.5),
        'norm2_g': 1.0 + nrm((DEPTH, D_MODEL), 0.02),
        'peer_wq': nrm((DEPTH, D_MODEL, PEER_HEADS * PEER_DQ), D_MODEL ** -0.5),
        'peer_subkeys': nrm((DEPTH, 2, N_KEYS, PEER_DHALF), PEER_DHALF ** -0.5),
        'peer_u': nrm((DEPTH, N_EXPERTS, D_MODEL), D_MODEL ** -0.5),
        'peer_v': nrm((DEPTH, N_EXPERTS, D_MODEL), (PEER_HEADS * PEER_TOPK) ** -0.5),
    }


def reference(x_prompt, x_sample, cache_k, cache_v, state_wkv, state_shift, page_table,
              meta, norm1_g, w_in, rw_mu, rw_w0, rw_w2, rw_a0, rw_a2, rw_g2, rw_kk, rw_ka, rw_rk,
              rw_ln_g, rw_ln_b, w_rw_br, da_qn_g, da_kn_g, da_lam, da_subln_g, w_da_br, w_out,
              norm2_g, peer_wq, peer_subkeys, peer_u, peer_v):
    B = x_prompt.shape[0]
    Bd = x_sample.shape[0]
    hp = jnp.concatenate([jnp.broadcast_to(meta.astype(x_prompt.dtype)[None], (B, N_META, D_MODEL)),
                          x_prompt], axis=1)
    hs = x_sample
    kp_l, vp_l, ks_l, vs_l, Sp_l, Ss_l, shp_l, shs_l = [], [], [], [], [], [], [], []
    for l in range(DEPTH):
        lp = dict(norm1_g=norm1_g[l], w_in=w_in[l], rw_mu=rw_mu[l], rw_w0=rw_w0[l], rw_w2=rw_w2[l],
                  rw_a0=rw_a0[l], rw_a2=rw_a2[l], rw_g2=rw_g2[l], rw_kk=rw_kk[l], rw_ka=rw_ka[l],
                  rw_rk=rw_rk[l], rw_ln_g=rw_ln_g[l], rw_ln_b=rw_ln_b[l], w_rw_br=w_rw_br[l],
                  da_qn_g=da_qn_g[l], da_kn_g=da_kn_g[l], da_lam=da_lam[l], da_subln_g=da_subln_g[l],
                  w_da_br=w_da_br[l], w_out=w_out[l], norm2_g=norm2_g[l], peer_wq=peer_wq[l],
                  peer_subkeys=peer_subkeys[l], peer_u=peer_u[l], peer_v=peer_v[l])
        lam_init = 0.8 - 0.6 * math.exp(-0.3 * l)
        hp, k_p, v_p, sh_p, S_p = layer_forward(
            hp, diff_attn_prompt, jnp.zeros((B, RW_PROJ), hp.dtype),
            jnp.zeros((B, RW_HEADS, RW_HEAD, RW_HEAD), F32), lp, lam_init)
        k_past = cache_k[l][page_table].reshape(Bd, -1, DA_HEADS, 2, DA_DK)
        v_past = cache_v[l][page_table].reshape(Bd, -1, DA_HEADS, DA_DV)
        attn_s = functools.partial(diff_attn_sample, k_past=k_past, v_past=v_past)
        hs, k_s, v_s, sh_s, S_s = layer_forward(hs, attn_s, state_shift[l], state_wkv[l], lp, lam_init)
        kp_l.append(k_p); vp_l.append(v_p); ks_l.append(k_s); vs_l.append(v_s)
        Sp_l.append(S_p); Ss_l.append(S_s); shp_l.append(sh_p); shs_l.append(sh_s)
    y_prompt = hp[:, N_META:]
    y_sample = hs
    return (y_prompt, y_sample, jnp.stack(kp_l), jnp.stack(vp_l), jnp.stack(ks_l), jnp.stack(vs_l),
            jnp.stack(Sp_l), jnp.stack(Ss_l), jnp.stack(shp_l), jnp.stack(shs_l))
```

```python
import functools
import math

import jax
import jax.numpy as jnp
from jax import lax
from jax.experimental import pallas as pl
from jax.experimental.pallas import tpu as pltpu

F32 = jnp.float32
BF16 = jnp.bfloat16

D_MODEL = 1024
N_META = 16
NORM_EPS = 1e-6
RW_HEAD = 64
RW_DIM = D_MODEL // 2
RW_HEADS = RW_DIM // RW_HEAD
W_LORA = 64
A_LORA = 64
G_LORA = 128
RW_PROJ = 3 * RW_DIM + W_LORA + A_LORA + G_LORA
RW_LN_EPS = 64e-5
DA_DK = 64
DA_DV = 2 * DA_DK
DA_DIM = D_MODEL // 2
DA_HEADS = DA_DIM // DA_DV
DA_QK = DA_HEADS * 2 * DA_DK
DA_V = DA_HEADS * DA_DV
N_KEYS = 128
PEER_HEADS = 8
PEER_TOPK = 16
PEER_DQ = 256
PEER_DHALF = PEER_DQ // 2
LAM_INIT = 0.8 - 0.6 * math.exp(-0.3 * 0)

LANES = 128
ROW_TILE = 128
PAD_LEAD = ROW_TILE - N_META
SCAN_CHUNK = 64
NEG_BIG = -1e30
VMEM_LIMIT = 56 * 1024 * 1024

O1 = RW_PROJ
O2 = O1 + DA_QK
O3 = O2 + DA_QK
O4 = O3 + DA_V
O5 = O4 + D_MODEL
O6 = O5 + D_MODEL

_CAND = [(i, j) for i in range(PEER_TOPK) for j in range(PEER_TOPK) if (i + 1) * (j + 1) <= PEER_TOPK]
_CAND_ROWS = -(-len(_CAND) // 8) * 8


def _cparams(sem):
    return pltpu.CompilerParams(dimension_semantics=sem, vmem_limit_bytes=VMEM_LIMIT)


def _split_dot(x, j):
    hi = x.astype(BF16)
    lo = (x - hi.astype(F32)).astype(BF16)
    return (jnp.dot(hi, j, preferred_element_type=F32)
            + jnp.dot(lo, j, preferred_element_type=F32))


def _seg64(x, j):
    outs = [_split_dot(x[:, c * LANES:(c + 1) * LANES], j) for c in range(x.shape[1] // LANES)]
    return outs[0] if len(outs) == 1 else jnp.concatenate(outs, axis=1)


def _seg_ones():
    r = lax.broadcasted_iota(jnp.int32, (LANES, LANES), 0) // 64
    c = lax.broadcasted_iota(jnp.int32, (LANES, LANES), 1) // 64
    return (r == c).astype(BF16)


def _in_proj_kernel(x_ref, g1_ref, w_ref, qg_ref, kg_ref, prw_ref, q_ref, k_ref, v_ref, gt_ref):
    x = x_ref[...]
    ms = jnp.mean(x * x, axis=-1, keepdims=True)
    xn = (x * lax.rsqrt(ms + NORM_EPS) * g1_ref[...]).astype(BF16)

    def mm(a, b):
        return jnp.dot(xn, w_ref[:, a:b], preferred_element_type=F32)

    j = _seg_ones()

    def head_norm(t, g):
        msq = _seg64(t * t, j) * (1.0 / DA_DK)
        return t * lax.rsqrt(msq + NORM_EPS) * g

    prw_ref[...] = mm(0, O1)
    q_ref[...] = head_norm(mm(O1, O2), qg_ref[...]) * (DA_DK ** -0.5)
    k_ref[...] = head_norm(mm(O2, O3), kg_ref[...])
    v_ref[...] = mm(O3, O4)
    gt_ref[...] = jax.nn.sigmoid(mm(O4, O6))


def _in_proj(x, g1, w_bf, qg, kg, tm):
    n = x.shape[0]
    assert n % tm == 0
    row = lambda c: pl.BlockSpec((tm, c), lambda i: (i, 0))
    full = lambda a: pl.BlockSpec(a.shape, lambda i: (0,) * a.ndim)
    widths = (RW_PROJ, DA_QK, DA_QK, DA_V, 2 * D_MODEL)
    return pl.pallas_call(
        _in_proj_kernel,
        grid=(n // tm,),
        in_specs=[row(D_MODEL), full(g1), full(w_bf), full(qg), full(kg)],
        out_specs=[row(c) for c in widths],
        out_shape=[jax.ShapeDtypeStruct((n, c), F32) for c in widths],
        compiler_params=_cparams(("parallel",)),
        name="in_proj",
    )(x, g1, w_bf, qg, kg)


def _rwkv_prep_math(p, p_prev, mu, w0, w2p, a0, a2p, g2, kkp, kap, rk):
    ps = p + (p_prev - p) * mu
    r = ps[:, 0:RW_DIM]
    k = ps[:, RW_DIM:2 * RW_DIM]
    v = ps[:, 2 * RW_DIM:3 * RW_DIM]
    wa = ps[:, 3 * RW_DIM:3 * RW_DIM + W_LORA + A_LORA]
    gl = ps[:, 3 * RW_DIM + W_LORA + A_LORA:RW_PROJ]
    w = -jax.nn.softplus(-(w0 + jnp.dot(jnp.tanh(wa).astype(BF16), w2p, preferred_element_type=F32))) - 0.5
    decay = jnp.exp(-jnp.exp(w))
    a = jax.nn.sigmoid(a0 + jnp.dot(wa.astype(BF16), a2p, preferred_element_type=F32))
    g = jnp.dot(jax.nn.sigmoid(gl).astype(BF16), g2, preferred_element_type=F32)
    j = _seg_ones()
    kk = k * kkp
    kk = kk / jnp.maximum(jnp.sqrt(_seg64(kk * kk, j)), 1e-12)
    k2 = k * (1.0 + (a - 1.0) * kap)
    bonus = _seg64(r * k2 * rk, j) * v
    return r, decay, k2, v, -kk, kk * a, g, bonus


def _rwkv_prep_carry_kernel(p_ref, mu, w0, w2p, a0, a2p, g2, kkp, kap, rk, *rest):
    outs, carry = rest[:-1], rest[-1]

    @pl.when(pl.program_id(1) == 0)
    def _():
        carry[...] = jnp.zeros_like(carry)

    p = p_ref[...]
    rows = lax.broadcasted_iota(jnp.int32, p.shape, 0)
    p_prev = jnp.where(rows == 0, carry[...], pltpu.roll(p, 1, 0))
    carry[...] = p[p.shape[0] - 1:, :]
    res = _rwkv_prep_math(p, p_prev, mu[...], w0[...], w2p[...], a0[...], a2p[...], g2[...],
                          kkp[...], kap[...], rk[...])
    for o, val in zip(outs, res):
        o[...] = val


def _rwkv_prep_given_kernel(p_ref, pp_ref, mu, w0, w2p, a0, a2p, g2, kkp, kap, rk, *outs):
    res = _rwkv_prep_math(p_ref[...], pp_ref[...], mu[...], w0[...], w2p[...], a0[...], a2p[...],
                          g2[...], kkp[...], kap[...], rk[...])
    for o, val in zip(outs, res):
        o[...] = val


def _rwkv_prep_prompt(p3, params, tm):
    b, tp, _ = p3.shape
    assert tp % tm == 0
    full = lambda a: pl.BlockSpec(a.shape, lambda i, j: (0,) * a.ndim)
    return pl.pallas_call(
        _rwkv_prep_carry_kernel,
        grid=(b, tp // tm),
        in_specs=[pl.BlockSpec((None, tm, RW_PROJ), lambda i, j: (i, j, 0))] + [full(a) for a in params],
        out_specs=[pl.BlockSpec((None, tm, RW_DIM), lambda i, j: (i, j, 0))] * 8,
        out_shape=[jax.ShapeDtypeStruct((b, tp, RW_DIM), F32)] * 8,
        scratch_shapes=[pltpu.VMEM((1, RW_PROJ), F32)],
        compiler_params=_cparams(("arbitrary", "arbitrary")),
        name="rwkv_prep_prompt",
    )(p3, *params)


def _rwkv_prep_sample(p, p_prev, params):
    n = p.shape[0]
    full = lambda a: pl.BlockSpec(a.shape, lambda i: (0,) * a.ndim)
    return pl.pallas_call(
        _rwkv_prep_given_kernel,
        grid=(1,),
        in_specs=[full(p), full(p_prev)] + [full(a) for a in params],
        out_specs=[pl.BlockSpec((n, RW_DIM), lambda i: (0, 0))] * 8,
        out_shape=[jax.ShapeDtypeStruct((n, RW_DIM), F32)] * 8,
        compiler_params=_cparams(("arbitrary",)),
        name="rwkv_prep_sample",
    )(p, p_prev, *params)


def _scan_kernel(r_ref, w_ref, k_ref, v_ref, a_ref, b_ref, s0_ref, y_ref, s_ref, *, nb, steps):
    @pl.when(pl.program_id(0) == 0)
    def _():
        s_ref[...] = s0_ref[...]

    lane = lax.broadcasted_iota(jnp.int32, (RW_HEAD, LANES), 1)
    sub = lax.broadcasted_iota(jnp.int32, (RW_HEAD, LANES), 0)
    first = lane < RW_HEAD
    eye2 = ((lane % RW_HEAD) == sub).astype(F32)

    def seg(x):
        lo = jnp.sum(jnp.where(first, x, 0.0), axis=-1, keepdims=True)
        hi = jnp.sum(jnp.where(first, 0.0, x), axis=-1, keepdims=True)
        return jnp.where(first, lo, hi)

    sub_t = min(steps, 8)

    def group(tg, carry):
        t0 = pl.multiple_of(tg * sub_t, sub_t)
        rows = pl.ds(t0, sub_t)
        for b in range(nb):
            for p in range(RW_DIM // LANES):
                cols = pl.ds(p * LANES, LANES)
                ins = [ref[b, rows, cols] for ref in (r_ref, w_ref, k_ref, v_ref, a_ref, b_ref)]
                s = s_ref[b, :, cols]
                ys = []
                for i in range(sub_t):
                    r_t, w_t, k_t, v_t, a_t, b_t = [x[i:i + 1, :] for x in ins]
                    sa = seg(s * a_t)
                    vcol = seg(eye2 * v_t)
                    s = s * w_t + sa * b_t + vcol * k_t
                    ycol = seg(s * r_t)
                    ys.append(jnp.sum(eye2 * ycol, axis=0, keepdims=True))
                s_ref[b, :, cols] = s
                y_ref[b, rows, cols] = ys[0] if sub_t == 1 else jnp.concatenate(ys, axis=0)
        return carry

    if steps == sub_t:
        group(0, 0)
    else:
        lax.fori_loop(0, steps // sub_t, group, 0)


def _rwkv_scan(r, w, k, v, a, b, s0, steps):
    nb, tp, _ = r.shape
    assert tp % steps == 0 and (steps == 1 or steps % 8 == 0)
    seq = pl.BlockSpec((nb, steps, RW_DIM), lambda i: (0, i, 0))
    st = pl.BlockSpec((nb, RW_HEAD, RW_DIM), lambda i: (0, 0, 0))
    return pl.pallas_call(
        functools.partial(_scan_kernel, nb=nb, steps=steps),
        grid=(tp // steps,),
        in_specs=[seq] * 6 + [st],
        out_specs=[seq, st],
        out_shape=[jax.ShapeDtypeStruct((nb, tp, RW_DIM), F32),
                   jax.ShapeDtypeStruct((nb, RW_HEAD, RW_DIM), F32)],
        compiler_params=_cparams(("arbitrary",)),
        name="rwkv_scan",
    )(r, w, k, v, a, b, s0)


def _lambda_value(lam_ref):
    lv = lam_ref[...]
    s01 = jnp.sum(lv[0:1, :] * lv[1:2, :], axis=-1, keepdims=True)
    s23 = jnp.sum(lv[2:3, :] * lv[3:4, :], axis=-1, keepdims=True)
    return jnp.exp(s01) - jnp.exp(s23) + LAM_INIT


def _alibi_slope(h):
    slope = jnp.float32(2.0 ** (-8.0 * DA_HEADS / DA_HEADS))
    for i in range(DA_HEADS - 2, -1, -1):
        slope = jnp.where(h == i, jnp.float32(2.0 ** (-8.0 * (i + 1) / DA_HEADS)), slope)
    return slope


def _sub_norm(o, g):
    return o * lax.rsqrt(jnp.mean(o * o, axis=-1, keepdims=True) + NORM_EPS) * g * (1.0 - LAM_INIT)


def _attn_prompt_kernel(qi_ref, kj_ref, q_ref, k_ref, v_ref, d0_ref, lam_ref, sg_ref, o_ref,
                        m_ref, l_ref, acc_ref, *, blk):
    h = pl.program_id(1)
    s_id = pl.program_id(2)
    qi = qi_ref[s_id]
    kj = kj_ref[s_id]

    @pl.when(kj == 0)
    def _():
        m_ref[...] = jnp.full_like(m_ref, NEG_BIG)
        l_ref[...] = jnp.zeros_like(l_ref)
        acc_ref[...] = jnp.zeros_like(acc_ref)

    lane = lax.broadcasted_iota(jnp.int32, (blk, LANES), 1)
    qb = q_ref[...].astype(BF16)
    kb = k_ref[...].astype(BF16)
    vb = v_ref[...].astype(BF16)
    dist = d0_ref[...] + ((qi - kj) * blk).astype(F32)
    kpos = kj * blk + lax.broadcasted_iota(jnp.int32, (blk, blk), 1)
    valid = (dist >= 0.0) & (kpos >= PAD_LEAD)
    bias = _alibi_slope(h) * dist
    for c in range(2):
        qc = jnp.where((lane // DA_DK) == c, qb, jnp.zeros_like(qb))
        s = lax.dot_general(qc, kb, (((1,), (1,)), ((), ())), preferred_element_type=F32)
        s = jnp.where(valid, s - bias, NEG_BIG)
        m_old = m_ref[c]
        m_new = jnp.maximum(m_old, jnp.max(s, axis=-1, keepdims=True))
        alpha = jnp.exp(m_old - m_new)
        p = jnp.exp(s - m_new)
        l_ref[c] = alpha * l_ref[c] + jnp.sum(p, axis=-1, keepdims=True)
        acc_ref[c] = alpha * acc_ref[c] + jnp.dot(p.astype(BF16), vb, preferred_element_type=F32)
        m_ref[c] = m_new

    @pl.when(kj == qi)
    def _():
        lam = _lambda_value(lam_ref)
        o = acc_ref[0] / l_ref[0] - lam * (acc_ref[1] / l_ref[1])
        o_ref[...] = _sub_norm(o, sg_ref[...])


def _attn_prompt(q, k, v, da_lam, subln_g, blk):
    b, tp, _ = q.shape
    assert tp % blk == 0 and PAD_LEAD < blk
    nblk = tp // blk
    pairs = [(i, j) for i in range(nblk) for j in range(i + 1)]
    qi = jnp.asarray([p[0] for p in pairs], jnp.int32)
    kj = jnp.asarray([p[1] for p in pairs], jnp.int32)
    d0 = (jnp.arange(blk, dtype=F32)[:, None] - jnp.arange(blk, dtype=F32)[None, :])
    qspec = pl.BlockSpec((None, blk, DA_DV), lambda bi, h, s, qi, kj: (bi, qi[s], h))
    kspec = pl.BlockSpec((None, blk, DA_DV), lambda bi, h, s, qi, kj: (bi, kj[s], h))
    full = lambda a: pl.BlockSpec(a.shape, lambda bi, h, s, qi, kj: (0,) * a.ndim)
    return pl.pallas_call(
        functools.partial(_attn_prompt_kernel, blk=blk),
        grid_spec=pltpu.PrefetchScalarGridSpec(
            num_scalar_prefetch=2,
            grid=(b, DA_HEADS, len(pairs)),
            in_specs=[qspec, kspec, kspec, full(d0), full(da_lam), full(subln_g)],
            out_specs=qspec,
            scratch_shapes=[pltpu.VMEM((2, blk, 1), F32), pltpu.VMEM((2, blk, 1), F32),
                            pltpu.VMEM((2, blk, DA_DV), F32)],
        ),
        out_shape=jax.ShapeDtypeStruct((b, tp, DA_V), F32),
        compiler_params=_cparams(("parallel", "parallel", "arbitrary")),
        name="attn_prompt",
    )(qi, kj, q, k, v, d0, da_lam, subln_g)


def _attn_sample_kernel(pt_ref, q_ref, kn_ref, vn_ref, lam_ref, sg_ref, *rest, ppg, page, past):
    k_refs = rest[:ppg]
    v_refs = rest[ppg:2 * ppg]
    o_ref, m_ref, l_ref, acc_ref = rest[2 * ppg:]
    g = pl.program_id(1)
    nrow = 2 * DA_HEADS
    rowi = lax.broadcasted_iota(jnp.int32, (nrow, DA_QK), 0)
    lanei = lax.broadcasted_iota(jnp.int32, (nrow, DA_QK), 1)
    qsel = (lanei // DA_DK) == rowi
    qmat = jnp.where(qsel, q_ref[...], 0.0)
    rh = lax.broadcasted_iota(jnp.int32, (nrow, 1), 0) // 2
    slope = jnp.full((nrow, 1), 2.0 ** (-8.0), F32)
    for i in range(DA_HEADS - 2, -1, -1):
        slope = jnp.where(rh == i, jnp.float32(2.0 ** (-8.0 * (i + 1) / DA_HEADS)), slope)

    @pl.when(g == 0)
    def _():
        m_ref[...] = jnp.full_like(m_ref, NEG_BIG)
        l_ref[...] = jnp.zeros_like(l_ref)
        acc_ref[...] = jnp.zeros_like(acc_ref)

    def absorb(s, vmat):
        m_old = m_ref[...]
        m_new = jnp.maximum(m_old, jnp.max(s, axis=-1, keepdims=True))
        alpha = jnp.exp(m_old - m_new)
        p = jnp.exp(s - m_new)
        l_ref[...] = alpha * l_ref[...] + jnp.sum(p, axis=-1, keepdims=True)
        acc_ref[...] = alpha * acc_ref[...] + jnp.dot(p.astype(BF16), vmat, preferred_element_type=F32)
        m_ref[...] = m_new

    qb = qmat.astype(BF16)
    for i in range(ppg):
        kpos = (g * ppg + i) * page + lax.broadcasted_iota(jnp.int32, (nrow, page), 1)
        dist = (past - kpos).astype(F32)
        s = lax.dot_general(qb, k_refs[i][...].astype(BF16), (((1,), (1,)), ((), ())),
                            preferred_element_type=F32)
        absorb(s - slope * dist, v_refs[i][...].astype(BF16))

    @pl.when(g == pl.num_programs(1) - 1)
    def _():
        kn = kn_ref[...].astype(BF16).astype(F32)
        s_self = jnp.sum(qb.astype(F32) * kn, axis=-1, keepdims=True)
        m_old = m_ref[...]
        m_new = jnp.maximum(m_old, s_self)
        alpha = jnp.exp(m_old - m_new)
        p_self = jnp.exp(s_self - m_new)
        l_fin = alpha * l_ref[...] + p_self
        vn = vn_ref[...].astype(BF16).astype(F32)
        acc = alpha * acc_ref[...] + p_self.astype(BF16).astype(F32) * vn
        lam = _lambda_value(lam_ref)
        ri = lax.broadcasted_iota(jnp.int32, (nrow, DA_V), 0)
        li = lax.broadcasted_iota(jnp.int32, (nrow, DA_V), 1)
        coef = jnp.where(ri % 2 == 0, 1.0, -lam)
        coef = jnp.where((li // DA_DV) == (ri // 2), coef, 0.0)
        o = jnp.sum(acc / l_fin * coef, axis=0, keepdims=True)
        outs = []
        for hh in range(DA_HEADS):
            outs.append(_sub_norm(o[:, hh * DA_DV:(hh + 1) * DA_DV], sg_ref[...]))
        o_ref[...] = jnp.concatenate(outs, axis=1)


def _attn_sample(q, kn, vn, cache_k, cache_v, page_table, da_lam, subln_g, ppg):
    bd = q.shape[0]
    n_pages = page_table.shape[1]
    page = cache_k.shape[1]
    assert n_pages % ppg == 0
    pt = page_table.reshape(-1)
    one = pl.BlockSpec((None, 1, DA_QK), lambda b, g, pt: (b, 0, 0))
    full = lambda a: pl.BlockSpec(a.shape, lambda b, g, pt: (0,) * a.ndim)

    def pspec(i):
        return pl.BlockSpec((None, page, DA_QK),
                            lambda b, g, pt, i=i: (pt[b * n_pages + g * ppg + i], 0, 0))

    out = pl.pallas_call(
        functools.partial(_attn_sample_kernel, ppg=ppg, page=page, past=n_pages * page),
        grid_spec=pltpu.PrefetchScalarGridSpec(
            num_scalar_prefetch=1,
            grid=(bd, n_pages // ppg),
            in_specs=[one, one, one, full(da_lam), full(subln_g)]
                     + [pspec(i) for i in range(ppg)] * 2,
            out_specs=one,
            scratch_shapes=[pltpu.VMEM((2 * DA_HEADS, 1), F32), pltpu.VMEM((2 * DA_HEADS, 1), F32),
                            pltpu.VMEM((2 * DA_HEADS, DA_V), F32)],
        ),
        out_shape=jax.ShapeDtypeStruct((bd, 1, DA_V), F32),
        compiler_params=_cparams(("parallel", "arbitrary")),
        name="attn_sample",
    )(pt, q.reshape(bd, 1, DA_QK), kn.reshape(bd, 1, DA_QK), vn.reshape(bd, 1, DA_V), da_lam, subln_g,
      *([cache_k] * ppg), *([cache_v] * ppg))
    return out.reshape(bd, DA_V)


def _merge_kernel(y_ref, bonus_ref, g_ref, lng_ref, lnb_ref, o_ref, gt_ref, h_ref, wrw_ref, wda_ref,
                  wout_ref, n2_ref, wq_ref, sk_ref, h2_ref, xn_ref, st_ref):
    j = _seg_ones()
    y = y_ref[...]
    mu = _seg64(y, j) * (1.0 / RW_HEAD)
    d = y - mu
    var = _seg64(d * d, j) * (1.0 / RW_HEAD)
    yln = d * lax.rsqrt(var + RW_LN_EPS) * lng_ref[...] + lnb_ref[...]
    yr = ((yln + bonus_ref[...]) * g_ref[...]).astype(BF16)
    y_rw = jnp.dot(yr, wrw_ref[...], preferred_element_type=F32)
    y_da = jnp.dot(o_ref[...].astype(BF16), wda_ref[...], preferred_element_type=F32)
    gt = gt_ref[...]
    mix = gt[:, :D_MODEL] * y_rw + gt[:, D_MODEL:] * y_da
    h2 = h_ref[...] + jnp.dot(mix.astype(BF16), wout_ref[...], preferred_element_type=F32)
    h2_ref[...] = h2
    ms = jnp.mean(h2 * h2, axis=-1, keepdims=True)
    xn = (h2 * lax.rsqrt(ms + NORM_EPS) * n2_ref[...]).astype(BF16)
    xn_ref[...] = xn
    q = jnp.dot(xn, wq_ref[...], preferred_element_type=F32)
    for hc in range(2 * PEER_HEADS):
        qc = q[:, hc * PEER_DHALF:(hc + 1) * PEER_DHALF].astype(BF16)
        st_ref[hc * N_KEYS:(hc + 1) * N_KEYS, :] = lax.dot_general(
            sk_ref[hc % 2], qc, (((1,), (1,)), ((), ())), preferred_element_type=F32)


def _merge(y, bonus, g, o, gates, h, consts, tm, tile_off):
    b, rows, _ = y.shape
    assert rows % tm == 0
    nt = rows // tm - tile_off
    rin = lambda c: pl.BlockSpec((None, tm, c), lambda i, t: (i, t + tile_off, 0))
    full = lambda a: pl.BlockSpec(a.shape, lambda i, t: (0,) * a.ndim)
    ntok = b * nt * tm
    return pl.pallas_call(
        _merge_kernel,
        grid=(b, nt),
        in_specs=[rin(RW_DIM), rin(RW_DIM), rin(RW_DIM), full(consts[0]), full(consts[1]), rin(DA_V),
                  rin(2 * D_MODEL), rin(D_MODEL)] + [full(a) for a in consts[2:]],
        out_specs=[pl.BlockSpec((tm, D_MODEL), lambda i, t: (i * nt + t, 0)),
                   pl.BlockSpec((tm, D_MODEL), lambda i, t: (i * nt + t, 0)),
                   pl.BlockSpec((2 * PEER_HEADS * N_KEYS, tm), lambda i, t: (0, i * nt + t))],
        out_shape=[jax.ShapeDtypeStruct((ntok, D_MODEL), F32),
                   jax.ShapeDtypeStruct((ntok, D_MODEL), BF16),
                   jax.ShapeDtypeStruct((2 * PEER_HEADS * N_KEYS, ntok), F32)],
        compiler_params=_cparams(("parallel", "parallel")),
        name="merge",
    )(y, bonus, g, consts[0], consts[1], o, gates, h, *consts[2:])


def _top16(x):
    nk = x.shape[0]
    iota = lax.broadcasted_iota(jnp.int32, x.shape, 0)
    rank = jnp.full(x.shape, float(nk), F32)
    vals = []
    for r in range(PEER_TOPK):
        m = jnp.max(x, axis=0, keepdims=True)
        cand = jnp.where(x == m, iota, nk)
        sel = cand == jnp.min(cand, axis=0, keepdims=True)
        x = jnp.where(sel, -jnp.inf, x)
        rank = jnp.where(sel, float(r), rank)
        vals.append(m)
    return vals, rank


def _peer_topk_kernel(st_ref, rank2_ref, c2_ref, cnt1_ref, c1_ref):
    n = st_ref.shape[1]
    for h in range(PEER_HEADS):
        s1 = st_ref[(2 * h) * N_KEYS:(2 * h + 1) * N_KEYS, :]
        s2 = st_ref[(2 * h + 1) * N_KEYS:(2 * h + 2) * N_KEYS, :]
        v1, rank1 = _top16(s1)
        v2, rank2 = _top16(s2)
        rows = [v1[i] + v2[j] for (i, j) in _CAND]
        rows += [jnp.full((1, n), -jnp.inf, F32)] * (_CAND_ROWS - len(_CAND))
        cand = jnp.concatenate(rows, axis=0)
        rid = lax.broadcasted_iota(jnp.int32, cand.shape, 0)
        flat = jnp.zeros(cand.shape, jnp.int32) + PEER_TOPK * PEER_TOPK
        irow = jnp.zeros(cand.shape, jnp.int32) + PEER_TOPK
        for ci, (i, j) in enumerate(_CAND):
            flat = jnp.where(rid == ci, i * PEER_TOPK + j, flat)
            irow = jnp.where(rid == ci, i, irow)
        top = v1[0] + v2[0]
        chosen = jnp.zeros(cand.shape, jnp.bool_)
        work = cand
        for _ in range(PEER_TOPK):
            m = jnp.max(work, axis=0, keepdims=True)
            fl = jnp.where(work == m, flat, PEER_TOPK * PEER_TOPK + 1)
            sel = fl == jnp.min(fl, axis=0, keepdims=True)
            work = jnp.where(sel, -jnp.inf, work)
            chosen = chosen | sel
        z = jnp.sum(jnp.where(chosen, jnp.exp(cand - top), 0.0), axis=0, keepdims=True)
        cnt1 = jnp.zeros(s1.shape, F32)
        for i in range(PEER_TOPK):
            m_i = jnp.sum(jnp.where(chosen & (irow == i), 1.0, 0.0), axis=0, keepdims=True)
            cnt1 = jnp.where(rank1 == float(i), m_i, cnt1)
        cnt1_ref[h] = cnt1
        c1_ref[h] = jnp.where(rank1 < float(PEER_TOPK), jnp.exp(s1 - v1[0]) / z, 0.0)
        c2_ref[h] = jnp.where(rank2 < float(PEER_TOPK), jnp.exp(s2 - v2[0]), 0.0)
        rank2_ref[h] = rank2


def _peer_topk(st, tn):
    ntok = st.shape[1]
    assert ntok % tn == 0
    spec = pl.BlockSpec((PEER_HEADS, N_KEYS, tn), lambda i: (0, 0, i))
    return pl.pallas_call(
        _peer_topk_kernel,
        grid=(ntok // tn,),
        in_specs=[pl.BlockSpec((st.shape[0], tn), lambda i: (0, i))],
        out_specs=[spec] * 4,
        out_shape=[jax.ShapeDtypeStruct((PEER_HEADS, N_KEYS, ntok), F32)] * 4,
        compiler_params=_cparams(("parallel",)),
        name="peer_topk",
    )(st)


def _peer_dense_kernel(x_ref, u_ref, vt_ref, rank2_ref, c2_ref, cnt1_ref, c1_ref, h2_ref, y_ref,
                       acc_ref, p_ref, *, e1b):
    j = pl.program_id(1)

    @pl.when(j == 0)
    def _():
        acc_ref[...] = jnp.zeros_like(acc_ref)

    hid = lax.dot_general(u_ref[...], x_ref[...], (((1,), (1,)), ((), ())),
                          preferred_element_type=F32)
    for e in range(e1b):
        e1 = j * e1b + e
        gate = None
        for h in range(PEER_HEADS):
            keep = rank2_ref[h] < cnt1_ref[h, pl.ds(e1, 1), :]
            term = jnp.where(keep, c2_ref[h], 0.0) * c1_ref[h, pl.ds(e1, 1), :]
            gate = term if gate is None else gate + term
        hh = hid[e * N_KEYS:(e + 1) * N_KEYS, :]
        gelu = 0.5 * hh * (1.0 + lax.erf(hh * (2.0 ** -0.5)))
        p_ref[e * N_KEYS:(e + 1) * N_KEYS, :] = (gelu * gate).astype(BF16)
    acc_ref[...] += jnp.dot(vt_ref[...], p_ref[...], preferred_element_type=F32)

    @pl.when(j == pl.num_programs(1) - 1)
    def _():
        y_ref[...] = h2_ref[...] + acc_ref[...].T


def _peer_dense(xn, u_bf, vt_bf, rank2, c2, cnt1, c1, h2, tn, e1b):
    ntok = xn.shape[0]
    assert ntok % tn == 0 and N_KEYS % e1b == 0
    tok = pl.BlockSpec((PEER_HEADS, N_KEYS, tn), lambda i, j: (0, 0, i))
    row = pl.BlockSpec((tn, D_MODEL), lambda i, j: (i, 0))
    return pl.pallas_call(
        functools.partial(_peer_dense_kernel, e1b=e1b),
        grid=(ntok // tn, N_KEYS // e1b),
        in_specs=[row,
                  pl.BlockSpec((e1b * N_KEYS, D_MODEL), lambda i, j: (j, 0)),
                  pl.BlockSpec((D_MODEL, e1b * N_KEYS), lambda i, j: (0, j)),
                  tok, tok, tok, tok, row],
        out_specs=row,
        out_shape=jax.ShapeDtypeStruct((ntok, D_MODEL), F32),
        scratch_shapes=[pltpu.VMEM((D_MODEL, tn), F32), pltpu.VMEM((e1b * N_KEYS, tn), BF16)],
        compiler_params=_cparams(("parallel", "arbitrary")),
        name="peer_dense",
    )(xn, u_bf, vt_bf, rank2, c2, cnt1, c1, h2)


def _state_in(s):
    b = s.shape[0]
    return jnp.transpose(s, (0, 2, 1, 3)).reshape(b, RW_HEAD, RW_DIM)


def _state_out(s):
    b = s.shape[0]
    return jnp.transpose(s.reshape(b, RW_HEAD, RW_HEADS, RW_HEAD), (0, 2, 1, 3))


def _pad_tokens(a, n, axis):
    pad = [(0, 0)] * a.ndim
    pad[axis] = (0, n - a.shape[axis])
    return jnp.pad(a, pad)


def kernel(x_prompt, x_sample, cache_k, cache_v, state_wkv, state_shift, page_table, meta, norm1_g, w_in,
           rw_mu, rw_w0, rw_w2, rw_a0, rw_a2, rw_g2, rw_kk, rw_ka, rw_rk, rw_ln_g, rw_ln_b, w_rw_br,
           da_qn_g, da_kn_g, da_lam, da_subln_g, w_da_br, w_out, norm2_g, peer_wq, peer_subkeys,
           peer_u, peer_v):
    b, seq, _ = x_prompt.shape
    bd = x_sample.shape[0]
    t_real = N_META + seq
    tp = PAD_LEAD + t_real
    assert tp % ROW_TILE == 0 and tp % SCAN_CHUNK == 0 and x_sample.shape[1] == 1
    row2 = lambda a: a.reshape(1, -1)

    w_in_bf = w_in[0].astype(BF16)
    g1 = row2(norm1_g[0])
    qg = row2(jnp.tile(da_qn_g[0], DA_QK // DA_DK))
    kg = row2(jnp.tile(da_kn_g[0], DA_QK // DA_DK))
    zlo = jnp.zeros((W_LORA, RW_DIM), F32)
    prep_params = (row2(rw_mu[0]), row2(rw_w0[0]),
                   jnp.concatenate([rw_w2[0], zlo], axis=0).astype(BF16), row2(rw_a0[0]),
                   jnp.concatenate([zlo, rw_a2[0]], axis=0).astype(BF16), rw_g2[0].astype(BF16),
                   row2(rw_kk[0]), row2(rw_ka[0]), row2(rw_rk[0]))
    merge_consts = (row2(rw_ln_g[0]), row2(rw_ln_b[0]), w_rw_br[0].astype(BF16), w_da_br[0].astype(BF16),
                    w_out[0].astype(BF16), row2(norm2_g[0]), peer_wq[0].astype(BF16),
                    peer_subkeys[0].astype(BF16))
    subln = row2(da_subln_g[0])
    u_bf = peer_u[0].astype(BF16)
    vt_bf = peer_v[0].astype(BF16).T

    hp = jnp.concatenate([jnp.zeros((b, PAD_LEAD, D_MODEL), F32),
                          jnp.broadcast_to(meta[None], (b, N_META, D_MODEL)), x_prompt], axis=1)
    prw, q, k, v, gates = _in_proj(hp.reshape(b * tp, D_MODEL), g1, w_in_bf, qg, kg, 256)
    r3 = lambda a: a.reshape(b, tp, -1)
    pr, pw, pk, pv, pa, pb, pg, pbonus = _rwkv_prep_prompt(r3(prw), prep_params, 384)
    y_scan, s_fin = _rwkv_scan(pr, pw, pk, pv, pa, pb, jnp.zeros((b, RW_HEAD, RW_DIM), F32), SCAN_CHUNK)
    o_attn = _attn_prompt(r3(q), r3(k), r3(v), da_lam[0], subln, 384)
    h2, xn2, st = _merge(y_scan, pbonus, pg, o_attn, r3(gates), hp, merge_consts, ROW_TILE, 1)
    y_prompt = _peer_dense(xn2, u_bf, vt_bf, *_peer_topk(st, 128), h2, 512, 8)

    xs = x_sample.reshape(bd, D_MODEL)
    prw_s, q_s, k_s, v_s, gates_s = _in_proj(xs, g1, w_in_bf, qg, kg, bd)
    sr, sw, sk, sv, sa, sb, sg, sbonus = _rwkv_prep_sample(prw_s, state_shift[0], prep_params)
    e3 = lambda a: a.reshape(bd, 1, -1)
    ys_scan, ss_fin = _rwkv_scan(e3(sr), e3(sw), e3(sk), e3(sv), e3(sa), e3(sb), _state_in(state_wkv[0]), 1)
    n_phys = cache_k.shape[1]
    ck = cache_k[0].reshape(n_phys, -1, DA_QK)
    cv = cache_v[0].reshape(n_phys, -1, DA_V)
    o_s = _attn_sample(q_s, k_s, v_s, ck, cv, page_table, da_lam[0], subln, 4)
    f3 = lambda a: a.reshape(1, bd, -1)
    h2s, xn2s, sts = _merge(ys_scan.reshape(1, bd, RW_DIM), f3(sbonus), f3(sg), f3(o_s), f3(gates_s),
                            f3(xs), merge_consts, bd, 0)
    npad = LANES
    tk_s = _peer_topk(_pad_tokens(sts, npad, 1), npad)
    y_s = _peer_dense(_pad_tokens(xn2s, npad, 0), u_bf, vt_bf, *tk_s, _pad_tokens(h2s, npad, 0), npad, 8)

    y_prompt = y_prompt.reshape(b, seq, D_MODEL)
    y_sample = y_s[:bd].reshape(bd, 1, D_MODEL)
    k_p = r3(k)[:, PAD_LEAD:].reshape(1, b, t_real, DA_HEADS, 2, DA_DK)
    v_p = r3(v)[:, PAD_LEAD:].reshape(1, b, t_real, DA_HEADS, DA_DV)
    return (y_prompt, y_sample, k_p, v_p,
            k_s.reshape(1, bd, 1, DA_HEADS, 2, DA_DK), v_s.reshape(1, bd, 1, DA_HEADS, DA_DV),
            _state_out(s_fin)[None], _state_out(ss_fin)[None],
            r3(prw)[:, -1][None], prw_s[None])
```

```python
import functools
import math

import jax
import jax.numpy as jnp
from jax import lax
from jax.experimental import pallas as pl
from jax.experimental.pallas import tpu as pltpu

F32 = jnp.float32
BF16 = jnp.bfloat16

D_MODEL = 1024
N_META = 16
NORM_EPS = 1e-6
RW_HEAD = 64
RW_DIM = D_MODEL // 2
RW_HEADS = RW_DIM // RW_HEAD
W_LORA = 64
A_LORA = 64
G_LORA = 128
RW_PROJ = 3 * RW_DIM + W_LORA + A_LORA + G_LORA
RW_LN_EPS = 64e-5
DA_DK = 64
DA_DV = 2 * DA_DK
DA_DIM = D_MODEL // 2
DA_HEADS = DA_DIM // DA_DV
DA_QK = DA_HEADS * 2 * DA_DK
DA_V = DA_HEADS * DA_DV
N_KEYS = 128
PEER_HEADS = 8
PEER_TOPK = 16
PEER_DQ = 256
PEER_DHALF = PEER_DQ // 2
LAM_INIT = 0.8 - 0.6 * math.exp(-0.3 * 0)

LANES = 128
ROW_TILE = 128
PAD_LEAD = ROW_TILE - N_META
SCAN_CHUNK = 64
NEG_BIG = -1e30
VMEM_LIMIT = 56 * 1024 * 1024

O1 = RW_PROJ
O2 = O1 + DA_QK
O3 = O2 + DA_QK
O4 = O3 + DA_V
O5 = O4 + D_MODEL
O6 = O5 + D_MODEL

_CAND = [(i, j) for i in range(PEER_TOPK) for j in range(PEER_TOPK) if (i + 1) * (j + 1) <= PEER_TOPK]
_CAND_ROWS = -(-len(_CAND) // 8) * 8


def _cparams(sem):
    return pltpu.CompilerParams(dimension_semantics=sem, vmem_limit_bytes=VMEM_LIMIT)


def _split_dot(x, j):
    hi = x.astype(BF16)
    lo = (x - hi.astype(F32)).astype(BF16)
    return (jnp.dot(hi, j, preferred_element_type=F32)
            + jnp.dot(lo, j, preferred_element_type=F32))


def _seg64(x, j):
    outs = [_split_dot(x[:, c * LANES:(c + 1) * LANES], j) for c in range(x.shape[1] // LANES)]
    return outs[0] if len(outs) == 1 else jnp.concatenate(outs, axis=1)


def _seg_ones():
    r = lax.broadcasted_iota(jnp.int32, (LANES, LANES), 0) // 64
    c = lax.broadcasted_iota(jnp.int32, (LANES, LANES), 1) // 64
    return (r == c).astype(BF16)


def _in_proj_kernel(x_ref, g1_ref, w_ref, qg_ref, kg_ref, prw_ref, q_ref, k_ref, v_ref, gt_ref):
    x = x_ref[...]
    ms = jnp.mean(x * x, axis=-1, keepdims=True)
    xn = (x * lax.rsqrt(ms + NORM_EPS) * g1_ref[...]).astype(BF16)

    def mm(a, b):
        return jnp.dot(xn, w_ref[:, a:b], preferred_element_type=F32)

    j = _seg_ones()

    def head_norm(t, g):
        msq = _seg64(t * t, j) * (1.0 / DA_DK)
        return t * lax.rsqrt(msq + NORM_EPS) * g

    prw_ref[...] = mm(0, O1)
    q_ref[...] = head_norm(mm(O1, O2), qg_ref[...]) * (DA_DK ** -0.5)
    k_ref[...] = head_norm(mm(O2, O3), kg_ref[...])
    v_ref[...] = mm(O3, O4)
    gt_ref[...] = jax.nn.sigmoid(mm(O4, O6))


def _in_proj(x, g1, w_bf, qg, kg, tm):
    n = x.shape[0]
    assert n % tm == 0
    row = lambda c: pl.BlockSpec((tm, c), lambda i: (i, 0))
    full = lambda a: pl.BlockSpec(a.shape, lambda i: (0,) * a.ndim)
    widths = (RW_PROJ, DA_QK, DA_QK, DA_V, 2 * D_MODEL)
    return pl.pallas_call(
        _in_proj_kernel,
        grid=(n // tm,),
        in_specs=[row(D_MODEL), full(g1), full(w_bf), full(qg), full(kg)],
        out_specs=[row(c) for c in widths],
        out_shape=[jax.ShapeDtypeStruct((n, c), F32) for c in widths],
        compiler_params=_cparams(("parallel",)),
        name="in_proj",
    )(x, g1, w_bf, qg, kg)


def _rwkv_prep_math(p, p_prev, mu, w0, w2p, a0, a2p, g2, kkp, kap, rk):
    ps = p + (p_prev - p) * mu
    r = ps[:, 0:RW_DIM]
    k = ps[:, RW_DIM:2 * RW_DIM]
    v = ps[:, 2 * RW_DIM:3 * RW_DIM]
    wa = ps[:, 3 * RW_DIM:3 * RW_DIM + W_LORA + A_LORA]
    gl = ps[:, 3 * RW_DIM + W_LORA + A_LORA:RW_PROJ]
    w = -jax.nn.softplus(-(w0 + jnp.dot(jnp.tanh(wa).astype(BF16), w2p, preferred_element_type=F32))) - 0.5
    decay = jnp.exp(-jnp.exp(w))
    a = jax.nn.sigmoid(a0 + jnp.dot(wa.astype(BF16), a2p, preferred_element_type=F32))
    g = jnp.dot(jax.nn.sigmoid(gl).astype(BF16), g2, preferred_element_type=F32)
    j = _seg_ones()
    kk = k * kkp
    kk = kk / jnp.maximum(jnp.sqrt(_seg64(kk * kk, j)), 1e-12)
    k2 = k * (1.0 + (a - 1.0) * kap)
    bonus = _seg64(r * k2 * rk, j) * v
    return r, decay, k2, v, -kk, kk * a, g, bonus


def _rwkv_prep_carry_kernel(p_ref, mu, w0, w2p, a0, a2p, g2, kkp, kap, rk, *rest):
    outs, carry = rest[:-1], rest[-1]

    @pl.when(pl.program_id(1) == 0)
    def _():
        carry[...] = jnp.zeros_like(carry)

    p = p_ref[...]
    rows = lax.broadcasted_iota(jnp.int32, p.shape, 0)
    p_prev = jnp.where(rows == 0, carry[...], pltpu.roll(p, 1, 0))
    carry[...] = p[p.shape[0] - 1:, :]
    res = _rwkv_prep_math(p, p_prev, mu[...], w0[...], w2p[...], a0[...], a2p[...], g2[...],
                          kkp[...], kap[...], rk[...])
    for o, val in zip(outs, res):
        o[...] = val


def _rwkv_prep_given_kernel(p_ref, pp_ref, mu, w0, w2p, a0, a2p, g2, kkp, kap, rk, *outs):
    res = _rwkv_prep_math(p_ref[...], pp_ref[...], mu[...], w0[...], w2p[...], a0[...], a2p[...],
                          g2[...], kkp[...], kap[...], rk[...])
    for o, val in zip(outs, res):
        o[...] = val


def _rwkv_prep_prompt(p3, params, tm):
    b, tp, _ = p3.shape
    assert tp % tm == 0
    full = lambda a: pl.BlockSpec(a.shape, lambda i, j: (0,) * a.ndim)
    return pl.pallas_call(
        _rwkv_prep_carry_kernel,
        grid=(b, tp // tm),
        in_specs=[pl.BlockSpec((None, tm, RW_PROJ), lambda i, j: (i, j, 0))] + [full(a) for a in params],
        out_specs=[pl.BlockSpec((None, tm, RW_DIM), lambda i, j: (i, j, 0))] * 8,
        out_shape=[jax.ShapeDtypeStruct((b, tp, RW_DIM), F32)] * 8,
        scratch_shapes=[pltpu.VMEM((1, RW_PROJ), F32)],
        compiler_params=_cparams(("arbitrary", "arbitrary")),
        name="rwkv_prep_prompt",
    )(p3, *params)


def _rwkv_prep_sample(p, p_prev, params):
    n = p.shape[0]
    full = lambda a: pl.BlockSpec(a.shape, lambda i: (0,) * a.ndim)
    return pl.pallas_call(
        _rwkv_prep_given_kernel,
        grid=(1,),
        in_specs=[full(p), full(p_prev)] + [full(a) for a in params],
        out_specs=[pl.BlockSpec((n, RW_DIM), lambda i: (0, 0))] * 8,
        out_shape=[jax.ShapeDtypeStruct((n, RW_DIM), F32)] * 8,
        compiler_params=_cparams(("arbitrary",)),
        name="rwkv_prep_sample",
    )(p, p_prev, *params)


def _scan_consts():
    lane = lax.broadcasted_iota(jnp.int32, (RW_HEAD, LANES), 1)
    sub = lax.broadcasted_iota(jnp.int32, (RW_HEAD, LANES), 0)
    first = lane < RW_HEAD
    eye_lo = (lane == sub).astype(F32)
    eye_hi = (lane - RW_HEAD == sub).astype(F32)
    return lane, first, eye_lo, eye_hi


def _seg(first, x_lo, x_hi):
    lo = jnp.sum(x_lo, axis=-1, keepdims=True)
    hi = jnp.sum(x_hi, axis=-1, keepdims=True)
    return jnp.where(first, lo, hi)


def _split_row(first, row):
    lo = jnp.where(first[0:1, :], row, 0.0)
    return lo, row - lo


def _scan_step_kernel(r_ref, w_ref, k_ref, v_ref, a_ref, b_ref, s0_ref, y_ref, s_ref, *, nb):
    _, first, eye_lo, eye_hi = _scan_consts()
    for b in range(nb):
        for p in range(RW_DIM // LANES):
            cols = pl.ds(p * LANES, LANES)
            r_t, w_t, k_t, v_t, a_t, b_t = [ref[pl.ds(b, 1), cols]
                                            for ref in (r_ref, w_ref, k_ref, v_ref, a_ref, b_ref)]
            s = s0_ref[b, :, cols]
            a_lo, a_hi = _split_row(first, a_t)
            r_lo, r_hi = _split_row(first, r_t)
            sa = _seg(first, s * a_lo, s * a_hi)
            vcol = _seg(first, eye_lo * v_t, eye_hi * v_t)
            s = s * w_t + sa * b_t + vcol * k_t
            s_ref[b, :, cols] = s
            ycol = _seg(first, s * r_lo, s * r_hi)
            y_ref[pl.ds(b, 1), cols] = jnp.sum((eye_lo + eye_hi) * ycol, axis=0, keepdims=True)


def _rwkv_step(r, w, k, v, a, b, s0):
    nb = r.shape[0]
    full = lambda x: pl.BlockSpec(x.shape, lambda i: (0,) * x.ndim)
    return pl.pallas_call(
        functools.partial(_scan_step_kernel, nb=nb),
        grid=(1,),
        in_specs=[full(r)] * 6 + [full(s0)],
        out_specs=[full(r), full(s0)],
        out_shape=[jax.ShapeDtypeStruct(r.shape, F32), jax.ShapeDtypeStruct(s0.shape, F32)],
        compiler_params=_cparams(("arbitrary",)),
        name="rwkv_step",
    )(r, w, k, v, a, b, s0)


def _scan_kernel(r_ref, w_ref, k_ref, v_ref, a_ref, b_ref, s0_ref, y_ref, sfin_ref,
                 s_scr, sa_scr, vc_scr, yacc_scr, *, nb):
    npair = RW_DIM // LANES
    chains = [(b, p) for b in range(nb) for p in range(npair)]
    lane, first, eye_lo, eye_hi = _scan_consts()
    sub_t = 8

    @pl.when(pl.program_id(0) == 0)
    def _():
        for ci, (b, p) in enumerate(chains):
            s_scr[ci] = s0_ref[b, :, p * LANES:(p + 1) * LANES]

    yacc_scr[...] = jnp.zeros_like(yacc_scr)

    def prepare(ci, s, a_t, v_t):
        a_lo, a_hi = _split_row(first, a_t)
        sa_scr[ci] = _seg(first, s * a_lo, s * a_hi)
        vc_scr[ci] = _seg(first, eye_lo * v_t, eye_hi * v_t)

    def group(tg, carry):
        rows = pl.ds(pl.multiple_of(tg * sub_t, sub_t), sub_t)

        def row(ref, ci, i):
            b, p = chains[ci]
            return ref[b, rows, pl.ds(p * LANES, LANES)][i:i + 1, :]

        for ci in range(len(chains)):
            prepare(ci, s_scr[ci], row(a_ref, ci, 0), row(v_ref, ci, 0))
        for i in range(sub_t):
            place = (lane % RW_HEAD) == tg * sub_t + i
            for ci in range(len(chains)):
                s = s_scr[ci] * row(w_ref, ci, i) + sa_scr[ci] * row(b_ref, ci, i) \
                    + vc_scr[ci] * row(k_ref, ci, i)
                s_scr[ci] = s
                if i + 1 < sub_t:
                    prepare(ci, s, row(a_ref, ci, i + 1), row(v_ref, ci, i + 1))
                r_lo, r_hi = _split_row(first, row(r_ref, ci, i))
                ycol = _seg(first, s * r_lo, s * r_hi)
                yacc_scr[ci] = jnp.where(place, ycol, yacc_scr[ci])
        return carry

    lax.fori_loop(0, SCAN_CHUNK // sub_t, group, 0)

    zeros = jnp.zeros((LANES - RW_HEAD, LANES), F32)
    for ci, (b, p) in enumerate(chains):
        cols = pl.ds(p * LANES, LANES)
        tr = jnp.concatenate([yacc_scr[ci], zeros], axis=0).T
        y_ref[b, :, cols] = jnp.where(first, tr[0:RW_HEAD], pltpu.roll(tr[RW_HEAD:], RW_HEAD, 1))
        sfin_ref[b, :, cols] = s_scr[ci]


def _rwkv_scan(r, w, k, v, a, b, s0):
    nb, tp, _ = r.shape
    assert tp % SCAN_CHUNK == 0 and SCAN_CHUNK == RW_HEAD
    nch = nb * (RW_DIM // LANES)
    seq = pl.BlockSpec((nb, SCAN_CHUNK, RW_DIM), lambda i: (0, i, 0))
    st = pl.BlockSpec((nb, RW_HEAD, RW_DIM), lambda i: (0, 0, 0))
    return pl.pallas_call(
        functools.partial(_scan_kernel, nb=nb),
        grid=(tp // SCAN_CHUNK,),
        in_specs=[seq] * 6 + [st],
        out_specs=[seq, st],
        out_shape=[jax.ShapeDtypeStruct((nb, tp, RW_DIM), F32),
                   jax.ShapeDtypeStruct((nb, RW_HEAD, RW_DIM), F32)],
        scratch_shapes=[pltpu.VMEM((nch, RW_HEAD, LANES), F32)] * 4,
        compiler_params=_cparams(("arbitrary",)),
        name="rwkv_scan",
    )(r, w, k, v, a, b, s0)


def _lambda_value(lam_ref):
    lv = lam_ref[...]
    s01 = jnp.sum(lv[0:1, :] * lv[1:2, :], axis=-1, keepdims=True)
    s23 = jnp.sum(lv[2:3, :] * lv[3:4, :], axis=-1, keepdims=True)
    return jnp.exp(s01) - jnp.exp(s23) + LAM_INIT


def _alibi_slope(h):
    slope = jnp.float32(2.0 ** (-8.0 * DA_HEADS / DA_HEADS))
    for i in range(DA_HEADS - 2, -1, -1):
        slope = jnp.where(h == i, jnp.float32(2.0 ** (-8.0 * (i + 1) / DA_HEADS)), slope)
    return slope


def _sub_norm(o, g):
    return o * lax.rsqrt(jnp.mean(o * o, axis=-1, keepdims=True) + NORM_EPS) * g * (1.0 - LAM_INIT)


def _attn_prompt_kernel(qi_ref, kj_ref, q_ref, k_ref, v_ref, d0_ref, lam_ref, sg_ref, o_ref,
                        m_ref, l_ref, acc_ref, *, blk):
    h = pl.program_id(1)
    s_id = pl.program_id(2)
    qi = qi_ref[s_id]
    kj = kj_ref[s_id]

    @pl.when(kj == 0)
    def _():
        m_ref[...] = jnp.full_like(m_ref, NEG_BIG)
        l_ref[...] = jnp.zeros_like(l_ref)
        acc_ref[...] = jnp.zeros_like(acc_ref)

    lane = lax.broadcasted_iota(jnp.int32, (blk, LANES), 1)
    qb = q_ref[...].astype(BF16)
    kb = k_ref[...].astype(BF16)
    vb = v_ref[...].astype(BF16)
    dist = d0_ref[...] + ((qi - kj) * blk).astype(F32)
    kpos = kj * blk + lax.broadcasted_iota(jnp.int32, (blk, blk), 1)
    valid = (dist >= 0.0) & (kpos >= PAD_LEAD)
    bias = _alibi_slope(h) * dist
    for c in range(2):
        qc = jnp.where((lane // DA_DK) == c, qb, jnp.zeros_like(qb))
        s = lax.dot_general(qc, kb, (((1,), (1,)), ((), ())), preferred_element_type=F32)
        s = jnp.where(valid, s - bias, NEG_BIG)
        m_old = m_ref[c]
        m_new = jnp.maximum(m_old, jnp.max(s, axis=-1, keepdims=True))
        alpha = jnp.exp(m_old - m_new)
        p = jnp.exp(s - m_new)
        l_ref[c] = alpha * l_ref[c] + jnp.sum(p, axis=-1, keepdims=True)
        acc_ref[c] = alpha * acc_ref[c] + jnp.dot(p.astype(BF16), vb, preferred_element_type=F32)
        m_ref[c] = m_new

    @pl.when(kj == qi)
    def _():
        lam = _lambda_value(lam_ref)
        o = acc_ref[0] / l_ref[0] - lam * (acc_ref[1] / l_ref[1])
        o_ref[...] = _sub_norm(o, sg_ref[...])


def _attn_prompt(q, k, v, da_lam, subln_g, blk):
    b, tp, _ = q.shape
    assert tp % blk == 0 and PAD_LEAD < blk
    nblk = tp // blk
    pairs = [(i, j) for i in range(nblk) for j in range(i + 1)]
    qi = jnp.asarray([p[0] for p in pairs], jnp.int32)
    kj = jnp.asarray([p[1] for p in pairs], jnp.int32)
    d0 = (jnp.arange(blk, dtype=F32)[:, None] - jnp.arange(blk, dtype=F32)[None, :])
    qspec = pl.BlockSpec((None, blk, DA_DV), lambda bi, h, s, qi, kj: (bi, qi[s], h))
    kspec = pl.BlockSpec((None, blk, DA_DV), lambda bi, h, s, qi, kj: (bi, kj[s], h))
    full = lambda a: pl.BlockSpec(a.shape, lambda bi, h, s, qi, kj: (0,) * a.ndim)
    return pl.pallas_call(
        functools.partial(_attn_prompt_kernel, blk=blk),
        grid_spec=pltpu.PrefetchScalarGridSpec(
            num_scalar_prefetch=2,
            grid=(b, DA_HEADS, len(pairs)),
            in_specs=[qspec, kspec, kspec, full(d0), full(da_lam), full(subln_g)],
            out_specs=qspec,
            scratch_shapes=[pltpu.VMEM((2, blk, 1), F32), pltpu.VMEM((2, blk, 1), F32),
                            pltpu.VMEM((2, blk, DA_DV), F32)],
        ),
        out_shape=jax.ShapeDtypeStruct((b, tp, DA_V), F32),
        compiler_params=_cparams(("parallel", "parallel", "arbitrary")),
        name="attn_prompt",
    )(qi, kj, q, k, v, d0, da_lam, subln_g)


def _attn_sample_kernel(pt_ref, q_ref, kn_ref, vn_ref, lam_ref, sg_ref, *rest, ppg, page, past):
    k_refs = rest[:ppg]
    v_refs = rest[ppg:2 * ppg]
    o_ref, m_ref, l_ref, acc_ref = rest[2 * ppg:]
    g = pl.program_id(1)
    nrow = 2 * DA_HEADS
    rowi = lax.broadcasted_iota(jnp.int32, (nrow, DA_QK), 0)
    lanei = lax.broadcasted_iota(jnp.int32, (nrow, DA_QK), 1)
    qsel = (lanei // DA_DK) == rowi
    qmat = jnp.where(qsel, q_ref[...], 0.0)
    rh = lax.broadcasted_iota(jnp.int32, (nrow, 1), 0) // 2
    slope = jnp.full((nrow, 1), 2.0 ** (-8.0), F32)
    for i in range(DA_HEADS - 2, -1, -1):
        slope = jnp.where(rh == i, jnp.float32(2.0 ** (-8.0 * (i + 1) / DA_HEADS)), slope)

    @pl.when(g == 0)
    def _():
        m_ref[...] = jnp.full_like(m_ref, NEG_BIG)
        l_ref[...] = jnp.zeros_like(l_ref)
        acc_ref[...] = jnp.zeros_like(acc_ref)

    qb = qmat.astype(BF16)
    kpos = g * (ppg * page) + lax.broadcasted_iota(jnp.int32, (nrow, ppg * page), 1)
    s = jnp.concatenate([jnp.dot(qb, k_refs[i][...].astype(BF16), preferred_element_type=F32)
                         for i in range(ppg)], axis=1)
    s = s - slope * (past - kpos).astype(F32)
    m_old = m_ref[...]
    m_new = jnp.maximum(m_old, jnp.max(s, axis=-1, keepdims=True))
    alpha = jnp.exp(m_old - m_new)
    p = jnp.exp(s - m_new)
    l_ref[...] = alpha * l_ref[...] + jnp.sum(p, axis=-1, keepdims=True)
    pb = p.astype(BF16)
    pv = [jnp.zeros((nrow, DA_DV), F32)] * DA_HEADS
    for i in range(ppg):
        for hh in range(DA_HEADS):
            vh = v_refs[i][pl.ds(hh, page, stride=DA_HEADS), :].astype(BF16)
            pv[hh] = pv[hh] + jnp.dot(pb[:, i * page:(i + 1) * page], vh, preferred_element_type=F32)
    tot = pv[DA_HEADS - 1]
    for hh in range(DA_HEADS - 1):
        tot = jnp.where(rh == hh, pv[hh], tot)
    acc_ref[...] = alpha * acc_ref[...] + tot
    m_ref[...] = m_new

    @pl.when(g == pl.num_programs(1) - 1)
    def _():
        kn = kn_ref[...].astype(BF16).astype(F32)
        s_self = jnp.sum(qb.astype(F32) * kn, axis=-1, keepdims=True)
        m_old = m_ref[...]
        m_new = jnp.maximum(m_old, s_self)
        alpha = jnp.exp(m_old - m_new)
        p_self = jnp.exp(s_self - m_new)
        l_fin = alpha * l_ref[...] + p_self
        vn = vn_ref[...].astype(BF16).astype(F32)
        vrow = jnp.zeros((nrow, DA_DV), F32)
        for hh in range(DA_HEADS):
            vrow = jnp.where(rh == hh, vn[:, hh * DA_DV:(hh + 1) * DA_DV], vrow)
        acc = alpha * acc_ref[...] + p_self.astype(BF16).astype(F32) * vrow
        lam = _lambda_value(lam_ref)
        ri = lax.broadcasted_iota(jnp.int32, (nrow, 1), 0)
        t = acc / l_fin * jnp.where(ri % 2 == 0, 1.0, -lam)
        outs = []
        for hh in range(DA_HEADS):
            o = t[2 * hh:2 * hh + 1, :] + t[2 * hh + 1:2 * hh + 2, :]
            outs.append(_sub_norm(o, sg_ref[...]))
        o_ref[...] = jnp.concatenate(outs, axis=1)


def _attn_sample(q, kn, vn, cache_kt, cache_v, page_table, da_lam, subln_g, ppg):
    bd = q.shape[0]
    n_pages = page_table.shape[1]
    page = cache_kt.shape[2]
    assert n_pages % ppg == 0 and cache_v.shape[1:] == (page * DA_HEADS, DA_DV)
    pt = page_table.reshape(-1)
    one = pl.BlockSpec((None, 1, DA_QK), lambda b, g, pt: (b, 0, 0))
    full = lambda a: pl.BlockSpec(a.shape, lambda b, g, pt: (0,) * a.ndim)

    def pspec(i):
        return pl.BlockSpec((None, DA_QK, page),
                            lambda b, g, pt, i=i: (pt[b * n_pages + g * ppg + i], 0, 0))

    out = pl.pallas_call(
        functools.partial(_attn_sample_kernel, ppg=ppg, page=page, past=n_pages * page),
        grid_spec=pltpu.PrefetchScalarGridSpec(
            num_scalar_prefetch=1,
            grid=(bd, n_pages // ppg),
            in_specs=[one, one, one, full(da_lam), full(subln_g)]
                     + [pspec(i) for i in range(ppg)] * 2,
            out_specs=one,
            scratch_shapes=[pltpu.VMEM((2 * DA_HEADS, 1), F32), pltpu.VMEM((2 * DA_HEADS, 1), F32),
                            pltpu.VMEM((2 * DA_HEADS, DA_DV), F32)],
        ),
        out_shape=jax.ShapeDtypeStruct((bd, 1, DA_V), F32),
        compiler_params=_cparams(("parallel", "arbitrary")),
        name="attn_sample",
    )(pt, q.reshape(bd, 1, DA_QK), kn.reshape(bd, 1, DA_QK), vn.reshape(bd, 1, DA_V), da_lam, subln_g,
      *([cache_kt] * ppg), *([cache_v] * ppg))
    return out.reshape(bd, DA_V)


def _merge_kernel(y_ref, bonus_ref, g_ref, lng_ref, lnb_ref, o_ref, gt_ref, h_ref, wrw_ref, wda_ref,
                  wout_ref, n2_ref, wq_ref, sk_ref, h2_ref, xn_ref, st_ref):
    j = _seg_ones()
    y = y_ref[...]
    mu = _seg64(y, j) * (1.0 / RW_HEAD)
    d = y - mu
    var = _seg64(d * d, j) * (1.0 / RW_HEAD)
    yln = d * lax.rsqrt(var + RW_LN_EPS) * lng_ref[...] + lnb_ref[...]
    yr = ((yln + bonus_ref[...]) * g_ref[...]).astype(BF16)
    y_rw = jnp.dot(yr, wrw_ref[...], preferred_element_type=F32)
    y_da = jnp.dot(o_ref[...].astype(BF16), wda_ref[...], preferred_element_type=F32)
    gt = gt_ref[...]
    mix = gt[:, :D_MODEL] * y_rw + gt[:, D_MODEL:] * y_da
    h2 = h_ref[...] + jnp.dot(mix.astype(BF16), wout_ref[...], preferred_element_type=F32)
    h2_ref[...] = h2
    ms = jnp.mean(h2 * h2, axis=-1, keepdims=True)
    xn = (h2 * lax.rsqrt(ms + NORM_EPS) * n2_ref[...]).astype(BF16)
    xn_ref[...] = xn
    q = jnp.dot(xn, wq_ref[...], preferred_element_type=F32)
    for hc in range(2 * PEER_HEADS):
        qc = q[:, hc * PEER_DHALF:(hc + 1) * PEER_DHALF].astype(BF16)
        st_ref[hc * N_KEYS:(hc + 1) * N_KEYS, :] = lax.dot_general(
            sk_ref[hc % 2], qc, (((1,), (1,)), ((), ())), preferred_element_type=F32)


def _merge(y, bonus, g, o, gates, h, consts, tm, tile_off):
    b, rows, _ = y.shape
    assert rows % tm == 0
    nt = rows // tm - tile_off
    rin = lambda c: pl.BlockSpec((None, tm, c), lambda i, t: (i, t + tile_off, 0))
    full = lambda a: pl.BlockSpec(a.shape, lambda i, t: (0,) * a.ndim)
    ntok = b * nt * tm
    return pl.pallas_call(
        _merge_kernel,
        grid=(b, nt),
        in_specs=[rin(RW_DIM), rin(RW_DIM), rin(RW_DIM), full(consts[0]), full(consts[1]), rin(DA_V),
                  rin(2 * D_MODEL), rin(D_MODEL)] + [full(a) for a in consts[2:]],
        out_specs=[pl.BlockSpec((tm, D_MODEL), lambda i, t: (i * nt + t, 0)),
                   pl.BlockSpec((tm, D_MODEL), lambda i, t: (i * nt + t, 0)),
                   pl.BlockSpec((2 * PEER_HEADS * N_KEYS, tm), lambda i, t: (0, i * nt + t))],
        out_shape=[jax.ShapeDtypeStruct((ntok, D_MODEL), F32),
                   jax.ShapeDtypeStruct((ntok, D_MODEL), BF16),
                   jax.ShapeDtypeStruct((2 * PEER_HEADS * N_KEYS, ntok), F32)],
        compiler_params=_cparams(("parallel", "parallel")),
        name="merge",
    )(y, bonus, g, consts[0], consts[1], o, gates, h, *consts[2:])


def _top16(x):
    nk = x.shape[0]
    iota = lax.broadcasted_iota(jnp.int32, x.shape, 0)
    rank = jnp.full(x.shape, float(nk), F32)
    vals = []
    for r in range(PEER_TOPK):
        m = jnp.max(x, axis=0, keepdims=True)
        cand = jnp.where(x == m, iota, nk)
        sel = cand == jnp.min(cand, axis=0, keepdims=True)
        x = jnp.where(sel, -jnp.inf, x)
        rank = jnp.where(sel, float(r), rank)
        vals.append(m)
    return vals, rank


def _peer_topk_kernel(st_ref, rank2_ref, c2_ref, cnt1_ref, c1_ref):
    n = st_ref.shape[1]
    for h in range(PEER_HEADS):
        s1 = st_ref[(2 * h) * N_KEYS:(2 * h + 1) * N_KEYS, :]
        s2 = st_ref[(2 * h + 1) * N_KEYS:(2 * h + 2) * N_KEYS, :]
        v1, rank1 = _top16(s1)
        v2, rank2 = _top16(s2)
        rows = [v1[i] + v2[j] for (i, j) in _CAND]
        rows += [jnp.full((1, n), -jnp.inf, F32)] * (_CAND_ROWS - len(_CAND))
        cand = jnp.concatenate(rows, axis=0)
        rid = lax.broadcasted_iota(jnp.int32, cand.shape, 0)
        flat = jnp.zeros(cand.shape, jnp.int32) + PEER_TOPK * PEER_TOPK
        irow = jnp.zeros(cand.shape, jnp.int32) + PEER_TOPK
        for ci, (i, j) in enumerate(_CAND):
            flat = jnp.where(rid == ci, i * PEER_TOPK + j, flat)
            irow = jnp.where(rid == ci, i, irow)
        top = v1[0] + v2[0]
        chosen = jnp.zeros(cand.shape, jnp.bool_)
        work = cand
        for _ in range(PEER_TOPK):
            m = jnp.max(work, axis=0, keepdims=True)
            fl = jnp.where(work == m, flat, PEER_TOPK * PEER_TOPK + 1)
            sel = fl == jnp.min(fl, axis=0, keepdims=True)
            work = jnp.where(sel, -jnp.inf, work)
            chosen = chosen | sel
        z = jnp.sum(jnp.where(chosen, jnp.exp(cand - top), 0.0), axis=0, keepdims=True)
        cnt1 = jnp.zeros(s1.shape, F32)
        for i in range(PEER_TOPK):
            m_i = jnp.sum(jnp.where(chosen & (irow == i), 1.0, 0.0), axis=0, keepdims=True)
            cnt1 = jnp.where(rank1 == float(i), m_i, cnt1)
        cnt1_ref[h] = cnt1
        c1_ref[h] = jnp.where(rank1 < float(PEER_TOPK), jnp.exp(s1 - v1[0]) / z, 0.0)
        c2_ref[h] = jnp.where(rank2 < float(PEER_TOPK), jnp.exp(s2 - v2[0]), 0.0)
        rank2_ref[h] = rank2


def _peer_topk(st, tn):
    ntok = st.shape[1]
    assert ntok % tn == 0
    spec = pl.BlockSpec((PEER_HEADS, N_KEYS, tn), lambda i: (0, 0, i))
    return pl.pallas_call(
        _peer_topk_kernel,
        grid=(ntok // tn,),
        in_specs=[pl.BlockSpec((st.shape[0], tn), lambda i: (0, i))],
        out_specs=[spec] * 4,
        out_shape=[jax.ShapeDtypeStruct((PEER_HEADS, N_KEYS, ntok), F32)] * 4,
        compiler_params=_cparams(("parallel",)),
        name="peer_topk",
    )(st)


def _peer_dense_kernel(x_ref, u_ref, vt_ref, rank2_ref, c2_ref, cnt1_ref, c1_ref, h2_ref, y_ref,
                       acc_ref, p_ref, *, e1b):
    j = pl.program_id(1)

    @pl.when(j == 0)
    def _():
        acc_ref[...] = jnp.zeros_like(acc_ref)

    hid = lax.dot_general(u_ref[...], x_ref[...], (((1,), (1,)), ((), ())),
                          preferred_element_type=F32)
    for e in range(e1b):
        e1 = j * e1b + e
        gate = None
        for h in range(PEER_HEADS):
            keep = rank2_ref[h] < cnt1_ref[h, pl.ds(e1, 1), :]
            term = jnp.where(keep, c2_ref[h], 0.0) * c1_ref[h, pl.ds(e1, 1), :]
            gate = term if gate is None else gate + term
        hh = hid[e * N_KEYS:(e + 1) * N_KEYS, :]
        gelu = 0.5 * hh * (1.0 + lax.erf(hh * (2.0 ** -0.5)))
        p_ref[e * N_KEYS:(e + 1) * N_KEYS, :] = (gelu * gate).astype(BF16)
    acc_ref[...] += jnp.dot(vt_ref[...], p_ref[...], preferred_element_type=F32)

    @pl.when(j == pl.num_programs(1) - 1)
    def _():
        y_ref[...] = h2_ref[...] + acc_ref[...].T


def _peer_dense(xn, u_bf, vt_bf, rank2, c2, cnt1, c1, h2, tn, e1b):
    ntok = xn.shape[0]
    assert ntok % tn == 0 and N_KEYS % e1b == 0
    tok = pl.BlockSpec((PEER_HEADS, N_KEYS, tn), lambda i, j: (0, 0, i))
    row = pl.BlockSpec((tn, D_MODEL), lambda i, j: (i, 0))
    return pl.pallas_call(
        functools.partial(_peer_dense_kernel, e1b=e1b),
        grid=(ntok // tn, N_KEYS // e1b),
        in_specs=[row,
                  pl.BlockSpec((e1b * N_KEYS, D_MODEL), lambda i, j: (j, 0)),
                  pl.BlockSpec((D_MODEL, e1b * N_KEYS), lambda i, j: (0, j)),
                  tok, tok, tok, tok, row],
        out_specs=row,
        out_shape=jax.ShapeDtypeStruct((ntok, D_MODEL), F32),
        scratch_shapes=[pltpu.VMEM((D_MODEL, tn), F32), pltpu.VMEM((e1b * N_KEYS, tn), BF16)],
        compiler_params=_cparams(("parallel", "arbitrary")),
        name="peer_dense",
    )(xn, u_bf, vt_bf, rank2, c2, cnt1, c1, h2)


def _state_in(s):
    b = s.shape[0]
    return jnp.transpose(s, (0, 2, 1, 3)).reshape(b, RW_HEAD, RW_DIM)


def _state_out(s):
    b = s.shape[0]
    return jnp.transpose(s.reshape(b, RW_HEAD, RW_HEADS, RW_HEAD), (0, 2, 1, 3))


def _pad_tokens(a, n, axis):
    pad = [(0, 0)] * a.ndim
    pad[axis] = (0, n - a.shape[axis])
    return jnp.pad(a, pad)


def kernel(x_prompt, x_sample, cache_k, cache_v, state_wkv, state_shift, page_table, meta, norm1_g, w_in,
           rw_mu, rw_w0, rw_w2, rw_a0, rw_a2, rw_g2, rw_kk, rw_ka, rw_rk, rw_ln_g, rw_ln_b, w_rw_br,
           da_qn_g, da_kn_g, da_lam, da_subln_g, w_da_br, w_out, norm2_g, peer_wq, peer_subkeys,
           peer_u, peer_v):
    b, seq, _ = x_prompt.shape
    bd = x_sample.shape[0]
    t_real = N_META + seq
    tp = PAD_LEAD + t_real
    assert tp % ROW_TILE == 0 and tp % SCAN_CHUNK == 0 and x_sample.shape[1] == 1
    row2 = lambda a: a.reshape(1, -1)

    w_in_bf = w_in[0].astype(BF16)
    g1 = row2(norm1_g[0])
    qg = row2(jnp.tile(da_qn_g[0], DA_QK // DA_DK))
    kg = row2(jnp.tile(da_kn_g[0], DA_QK // DA_DK))
    zlo = jnp.zeros((W_LORA, RW_DIM), F32)
    prep_params = (row2(rw_mu[0]), row2(rw_w0[0]),
                   jnp.concatenate([rw_w2[0], zlo], axis=0).astype(BF16), row2(rw_a0[0]),
                   jnp.concatenate([zlo, rw_a2[0]], axis=0).astype(BF16), rw_g2[0].astype(BF16),
                   row2(rw_kk[0]), row2(rw_ka[0]), row2(rw_rk[0]))
    merge_consts = (row2(rw_ln_g[0]), row2(rw_ln_b[0]), w_rw_br[0].astype(BF16), w_da_br[0].astype(BF16),
                    w_out[0].astype(BF16), row2(norm2_g[0]), peer_wq[0].astype(BF16),
                    peer_subkeys[0].astype(BF16))
    subln = row2(da_subln_g[0])
    u_bf = peer_u[0].astype(BF16)
    vt_bf = peer_v[0].astype(BF16).T

    hp = jnp.concatenate([jnp.zeros((b, PAD_LEAD, D_MODEL), F32),
                          jnp.broadcast_to(meta[None], (b, N_META, D_MODEL)), x_prompt], axis=1)
    prw, q, k, v, gates = _in_proj(hp.reshape(b * tp, D_MODEL), g1, w_in_bf, qg, kg, 256)
    r3 = lambda a: a.reshape(b, tp, -1)
    pr, pw, pk, pv, pa, pb, pg, pbonus = _rwkv_prep_prompt(r3(prw), prep_params, 384)
    y_scan, s_fin = _rwkv_scan(pr, pw, pk, pv, pa, pb, jnp.zeros((b, RW_HEAD, RW_DIM), F32))
    o_attn = _attn_prompt(r3(q), r3(k), r3(v), da_lam[0], subln, 384)
    h2, xn2, st = _merge(y_scan, pbonus, pg, o_attn, r3(gates), hp, merge_consts, ROW_TILE, 1)
    y_prompt = _peer_dense(xn2, u_bf, vt_bf, *_peer_topk(st, 128), h2, 512, 8)

    xs = x_sample.reshape(bd, D_MODEL)
    prw_s, q_s, k_s, v_s, gates_s = _in_proj(xs, g1, w_in_bf, qg, kg, bd)
    sr, sw, sk, sv, sa, sb, sg, sbonus = _rwkv_prep_sample(prw_s, state_shift[0], prep_params)
    ys_scan, ss_fin = _rwkv_step(sr, sw, sk, sv, sa, sb, _state_in(state_wkv[0]))
    n_phys = cache_k.shape[1]
    ckt = jnp.transpose(cache_k[0].reshape(n_phys, -1, DA_QK), (0, 2, 1))
    cv = cache_v[0].reshape(n_phys, -1, DA_DV)
    o_s = _attn_sample(q_s, k_s, v_s, ckt, cv, page_table, da_lam[0], subln, 8)
    f3 = lambda a: a.reshape(1, bd, -1)
    h2s, xn2s, sts = _merge(ys_scan.reshape(1, bd, RW_DIM), f3(sbonus), f3(sg), f3(o_s), f3(gates_s),
                            f3(xs), merge_consts, bd, 0)
    npad = LANES
    tk_s = _peer_topk(_pad_tokens(sts, npad, 1), npad)
    y_s = _peer_dense(_pad_tokens(xn2s, npad, 0), u_bf, vt_bf, *tk_s, _pad_tokens(h2s, npad, 0), npad, 8)

    y_prompt = y_prompt.reshape(b, seq, D_MODEL)
    y_sample = y_s[:bd].reshape(bd, 1, D_MODEL)
    k_p = r3(k)[:, PAD_LEAD:].reshape(1, b, t_real, DA_HEADS, 2, DA_DK)
    v_p = r3(v)[:, PAD_LEAD:].reshape(1, b, t_real, DA_HEADS, DA_DV)
    return (y_prompt, y_sample, k_p, v_p,
            k_s.reshape(1, bd, 1, DA_HEADS, 2, DA_DK), v_s.reshape(1, bd, 1, DA_HEADS, DA_DV),
            _state_out(s_fin)[None], _state_out(ss_fin)[None],
            r3(prw)[:, -1][None], prw_s[None])
```

```python
import functools
import math

import jax
import jax.numpy as jnp
from jax import lax
from jax.experimental import pallas as pl
from jax.experimental.pallas import tpu as pltpu

F32 = jnp.float32
BF16 = jnp.bfloat16

D_MODEL = 1024
N_META = 16
NORM_EPS = 1e-6
RW_HEAD = 64
RW_DIM = D_MODEL // 2
RW_HEADS = RW_DIM // RW_HEAD
W_LORA = 64
A_LORA = 64
G_LORA = 128
RW_PROJ = 3 * RW_DIM + W_LORA + A_LORA + G_LORA
RW_LN_EPS = 64e-5
DA_DK = 64
DA_DV = 2 * DA_DK
DA_DIM = D_MODEL // 2
DA_HEADS = DA_DIM // DA_DV
DA_QK = DA_HEADS * 2 * DA_DK
DA_V = DA_HEADS * DA_DV
N_KEYS = 128
PEER_HEADS = 8
PEER_TOPK = 16
PEER_DQ = 256
PEER_DHALF = PEER_DQ // 2
LAM_INIT = 0.8 - 0.6 * math.exp(-0.3 * 0)

LANES = 128
BF16_ROWS = 16
ROW_TILE = 128
PAD_LEAD = ROW_TILE - N_META
SCAN_CHUNK = 64
NEG_BIG = -1e30
VMEM_LIMIT = 56 * 1024 * 1024

O1 = RW_PROJ
O2 = O1 + DA_QK
O3 = O2 + DA_QK
O4 = O3 + DA_V
O5 = O4 + D_MODEL
O6 = O5 + D_MODEL

_CAND = [(i, j) for i in range(PEER_TOPK) for j in range(PEER_TOPK) if (i + 1) * (j + 1) <= PEER_TOPK]
_CAND_ROWS = -(-len(_CAND) // 8) * 8


def _cparams(sem):
    return pltpu.CompilerParams(dimension_semantics=sem, vmem_limit_bytes=VMEM_LIMIT)


def _split_dot(x, j):
    hi = x.astype(BF16)
    lo = (x - hi.astype(F32)).astype(BF16)
    return (jnp.dot(hi, j, preferred_element_type=F32)
            + jnp.dot(lo, j, preferred_element_type=F32))


def _seg64(x, j):
    outs = [_split_dot(x[:, c * LANES:(c + 1) * LANES], j) for c in range(x.shape[1] // LANES)]
    return outs[0] if len(outs) == 1 else jnp.concatenate(outs, axis=1)


def _seg_ones():
    r = lax.broadcasted_iota(jnp.int32, (LANES, LANES), 0) // 64
    c = lax.broadcasted_iota(jnp.int32, (LANES, LANES), 1) // 64
    return (r == c).astype(BF16)


def _in_proj_kernel(x_ref, g1_ref, w_ref, qg_ref, kg_ref, prw_ref, q_ref, k_ref, v_ref, gt_ref):
    x = x_ref[...]
    ms = jnp.mean(x * x, axis=-1, keepdims=True)
    xn = (x * lax.rsqrt(ms + NORM_EPS) * g1_ref[...]).astype(BF16)

    def mm(a, b):
        return jnp.dot(xn, w_ref[:, a:b], preferred_element_type=F32)

    j = _seg_ones()

    def head_norm(t, g):
        msq = _seg64(t * t, j) * (1.0 / DA_DK)
        return t * lax.rsqrt(msq + NORM_EPS) * g

    prw_ref[...] = mm(0, O1)
    q_ref[...] = head_norm(mm(O1, O2), qg_ref[...]) * (DA_DK ** -0.5)
    k_ref[...] = head_norm(mm(O2, O3), kg_ref[...])
    v_ref[...] = mm(O3, O4)
    gt_ref[...] = jax.nn.sigmoid(mm(O4, O6))


def _in_proj(x, g1, w_bf, qg, kg, tm):
    n = x.shape[0]
    assert n % tm == 0
    row = lambda c: pl.BlockSpec((tm, c), lambda i: (i, 0))
    full = lambda a: pl.BlockSpec(a.shape, lambda i: (0,) * a.ndim)
    widths = (RW_PROJ, DA_QK, DA_QK, DA_V, 2 * D_MODEL)
    return pl.pallas_call(
        _in_proj_kernel,
        grid=(n // tm,),
        in_specs=[row(D_MODEL), full(g1), full(w_bf), full(qg), full(kg)],
        out_specs=[row(c) for c in widths],
        out_shape=[jax.ShapeDtypeStruct((n, c), F32) for c in widths],
        compiler_params=_cparams(("parallel",)),
        name="in_proj",
    )(x, g1, w_bf, qg, kg)


def _rwkv_prep_math(p, p_prev, mu, w0, w2p, a0, a2p, g2, kkp, kap, rk):
    ps = p + (p_prev - p) * mu
    r = ps[:, 0:RW_DIM]
    k = ps[:, RW_DIM:2 * RW_DIM]
    v = ps[:, 2 * RW_DIM:3 * RW_DIM]
    wa = ps[:, 3 * RW_DIM:3 * RW_DIM + W_LORA + A_LORA]
    gl = ps[:, 3 * RW_DIM + W_LORA + A_LORA:RW_PROJ]
    w = -jax.nn.softplus(-(w0 + jnp.dot(jnp.tanh(wa).astype(BF16), w2p, preferred_element_type=F32))) - 0.5
    decay = jnp.exp(-jnp.exp(w))
    a = jax.nn.sigmoid(a0 + jnp.dot(wa.astype(BF16), a2p, preferred_element_type=F32))
    g = jnp.dot(jax.nn.sigmoid(gl).astype(BF16), g2, preferred_element_type=F32)
    j = _seg_ones()
    kk = k * kkp
    kk = kk / jnp.maximum(jnp.sqrt(_seg64(kk * kk, j)), 1e-12)
    k2 = k * (1.0 + (a - 1.0) * kap)
    bonus = _seg64(r * k2 * rk, j) * v
    return r, decay, k2, v, -kk, kk * a, g, bonus


def _rwkv_prep_carry_kernel(p_ref, mu, w0, w2p, a0, a2p, g2, kkp, kap, rk, *rest):
    outs, carry = rest[:-1], rest[-1]

    @pl.when(pl.program_id(1) == 0)
    def _():
        carry[...] = jnp.zeros_like(carry)

    p = p_ref[...]
    rows = lax.broadcasted_iota(jnp.int32, p.shape, 0)
    p_prev = jnp.where(rows == 0, carry[...], pltpu.roll(p, 1, 0))
    carry[...] = p[p.shape[0] - 1:, :]
    res = _rwkv_prep_math(p, p_prev, mu[...], w0[...], w2p[...], a0[...], a2p[...], g2[...],
                          kkp[...], kap[...], rk[...])
    for o, val in zip(outs, res):
        o[...] = val


def _rwkv_prep_given_kernel(p_ref, pp_ref, mu, w0, w2p, a0, a2p, g2, kkp, kap, rk, *outs):
    res = _rwkv_prep_math(p_ref[...], pp_ref[...], mu[...], w0[...], w2p[...], a0[...], a2p[...],
                          g2[...], kkp[...], kap[...], rk[...])
    for o, val in zip(outs, res):
        o[...] = val


def _rwkv_prep_prompt(p3, params, tm):
    b, tp, _ = p3.shape
    assert tp % tm == 0
    full = lambda a: pl.BlockSpec(a.shape, lambda i, j: (0,) * a.ndim)
    return pl.pallas_call(
        _rwkv_prep_carry_kernel,
        grid=(b, tp // tm),
        in_specs=[pl.BlockSpec((None, tm, RW_PROJ), lambda i, j: (i, j, 0))] + [full(a) for a in params],
        out_specs=[pl.BlockSpec((None, tm, RW_DIM), lambda i, j: (i, j, 0))] * 8,
        out_shape=[jax.ShapeDtypeStruct((b, tp, RW_DIM), F32)] * 8,
        scratch_shapes=[pltpu.VMEM((1, RW_PROJ), F32)],
        compiler_params=_cparams(("arbitrary", "arbitrary")),
        name="rwkv_prep_prompt",
    )(p3, *params)


def _rwkv_prep_sample(p, p_prev, params):
    n = p.shape[0]
    full = lambda a: pl.BlockSpec(a.shape, lambda i: (0,) * a.ndim)
    return pl.pallas_call(
        _rwkv_prep_given_kernel,
        grid=(1,),
        in_specs=[full(p), full(p_prev)] + [full(a) for a in params],
        out_specs=[pl.BlockSpec((n, RW_DIM), lambda i: (0, 0))] * 8,
        out_shape=[jax.ShapeDtypeStruct((n, RW_DIM), F32)] * 8,
        compiler_params=_cparams(("arbitrary",)),
        name="rwkv_prep_sample",
    )(p, p_prev, *params)


def _scan_consts():
    lane = lax.broadcasted_iota(jnp.int32, (RW_HEAD, LANES), 1)
    sub = lax.broadcasted_iota(jnp.int32, (RW_HEAD, LANES), 0)
    first = lane < RW_HEAD
    eye_lo = (lane == sub).astype(F32)
    eye_hi = (lane - RW_HEAD == sub).astype(F32)
    return lane, first, eye_lo, eye_hi


def _seg(first, x_lo, x_hi):
    lo = jnp.sum(x_lo, axis=-1, keepdims=True)
    hi = jnp.sum(x_hi, axis=-1, keepdims=True)
    return jnp.where(first, lo, hi)


def _split_row(first, row):
    lo = jnp.where(first[0:1, :], row, 0.0)
    return lo, row - lo


def _scan_step_kernel(r_ref, w_ref, k_ref, v_ref, a_ref, b_ref, s0_ref, y_ref, s_ref, *, nb):
    _, first, eye_lo, eye_hi = _scan_consts()
    for b in range(nb):
        for p in range(RW_DIM // LANES):
            cols = pl.ds(p * LANES, LANES)
            r_t, w_t, k_t, v_t, a_t, b_t = [ref[pl.ds(b, 1), cols]
                                            for ref in (r_ref, w_ref, k_ref, v_ref, a_ref, b_ref)]
            s = s0_ref[b, :, cols]
            a_lo, a_hi = _split_row(first, a_t)
            r_lo, r_hi = _split_row(first, r_t)
            sa = _seg(first, s * a_lo, s * a_hi)
            vcol = _seg(first, eye_lo * v_t, eye_hi * v_t)
            s = s * w_t + sa * b_t + vcol * k_t
            s_ref[b, :, cols] = s
            ycol = _seg(first, s * r_lo, s * r_hi)
            y_ref[pl.ds(b, 1), cols] = jnp.sum((eye_lo + eye_hi) * ycol, axis=0, keepdims=True)


def _rwkv_step(r, w, k, v, a, b, s0):
    nb = r.shape[0]
    full = lambda x: pl.BlockSpec(x.shape, lambda i: (0,) * x.ndim)
    return pl.pallas_call(
        functools.partial(_scan_step_kernel, nb=nb),
        grid=(1,),
        in_specs=[full(r)] * 6 + [full(s0)],
        out_specs=[full(r), full(s0)],
        out_shape=[jax.ShapeDtypeStruct(r.shape, F32), jax.ShapeDtypeStruct(s0.shape, F32)],
        compiler_params=_cparams(("arbitrary",)),
        name="rwkv_step",
    )(r, w, k, v, a, b, s0)


def _scan_kernel(r_ref, w_ref, k_ref, v_ref, a_ref, b_ref, s0_ref, y_ref, sfin_ref,
                 s_scr, sa_scr, vc_scr, yacc_scr, *, nb):
    npair = RW_DIM // LANES
    chains = [(b, p) for b in range(nb) for p in range(npair)]
    lane, first, eye_lo, eye_hi = _scan_consts()
    sub_t = 8

    @pl.when(pl.program_id(0) == 0)
    def _():
        for ci, (b, p) in enumerate(chains):
            s_scr[ci] = s0_ref[b, :, p * LANES:(p + 1) * LANES]

    yacc_scr[...] = jnp.zeros_like(yacc_scr)

    def prepare(ci, s, a_t, v_t):
        a_lo, a_hi = _split_row(first, a_t)
        sa_scr[ci] = _seg(first, s * a_lo, s * a_hi)
        vc_scr[ci] = _seg(first, eye_lo * v_t, eye_hi * v_t)

    def group(tg, carry):
        rows = pl.ds(pl.multiple_of(tg * sub_t, sub_t), sub_t)

        def row(ref, ci, i):
            b, p = chains[ci]
            return ref[b, rows, pl.ds(p * LANES, LANES)][i:i + 1, :]

        for ci in range(len(chains)):
            prepare(ci, s_scr[ci], row(a_ref, ci, 0), row(v_ref, ci, 0))
        for i in range(sub_t):
            place = (lane % RW_HEAD) == tg * sub_t + i
            for ci in range(len(chains)):
                s = s_scr[ci] * row(w_ref, ci, i) + sa_scr[ci] * row(b_ref, ci, i) \
                    + vc_scr[ci] * row(k_ref, ci, i)
                s_scr[ci] = s
                if i + 1 < sub_t:
                    prepare(ci, s, row(a_ref, ci, i + 1), row(v_ref, ci, i + 1))
                r_lo, r_hi = _split_row(first, row(r_ref, ci, i))
                ycol = _seg(first, s * r_lo, s * r_hi)
                yacc_scr[ci] = jnp.where(place, ycol, yacc_scr[ci])
        return carry

    lax.fori_loop(0, SCAN_CHUNK // sub_t, group, 0)

    zeros = jnp.zeros((LANES - RW_HEAD, LANES), F32)
    for ci, (b, p) in enumerate(chains):
        cols = pl.ds(p * LANES, LANES)
        tr = jnp.concatenate([yacc_scr[ci], zeros], axis=0).T
        y_ref[b, :, cols] = jnp.where(first, tr[0:RW_HEAD], pltpu.roll(tr[RW_HEAD:], RW_HEAD, 1))
        sfin_ref[b, :, cols] = s_scr[ci]


def _rwkv_scan(r, w, k, v, a, b, s0):
    nb, tp, _ = r.shape
    assert tp % SCAN_CHUNK == 0 and SCAN_CHUNK == RW_HEAD
    nch = nb * (RW_DIM // LANES)
    seq = pl.BlockSpec((nb, SCAN_CHUNK, RW_DIM), lambda i: (0, i, 0))
    st = pl.BlockSpec((nb, RW_HEAD, RW_DIM), lambda i: (0, 0, 0))
    return pl.pallas_call(
        functools.partial(_scan_kernel, nb=nb),
        grid=(tp // SCAN_CHUNK,),
        in_specs=[seq] * 6 + [st],
        out_specs=[seq, st],
        out_shape=[jax.ShapeDtypeStruct((nb, tp, RW_DIM), F32),
                   jax.ShapeDtypeStruct((nb, RW_HEAD, RW_DIM), F32)],
        scratch_shapes=[pltpu.VMEM((nch, RW_HEAD, LANES), F32)] * 4,
        compiler_params=_cparams(("arbitrary",)),
        name="rwkv_scan",
    )(r, w, k, v, a, b, s0)


def _lambda_value(lam_ref):
    lv = lam_ref[...]
    s01 = jnp.sum(lv[0:1, :] * lv[1:2, :], axis=-1, keepdims=True)
    s23 = jnp.sum(lv[2:3, :] * lv[3:4, :], axis=-1, keepdims=True)
    return jnp.exp(s01) - jnp.exp(s23) + LAM_INIT


def _alibi_slope(h):
    slope = jnp.float32(2.0 ** (-8.0 * DA_HEADS / DA_HEADS))
    for i in range(DA_HEADS - 2, -1, -1):
        slope = jnp.where(h == i, jnp.float32(2.0 ** (-8.0 * (i + 1) / DA_HEADS)), slope)
    return slope


def _sub_norm(o, g):
    return o * lax.rsqrt(jnp.mean(o * o, axis=-1, keepdims=True) + NORM_EPS) * g * (1.0 - LAM_INIT)


ATT_ONES = 16


def _attn_prompt_kernel(q_ref, k_ref, v_ref, d0_ref, lam_ref, sg_ref, o_ref,
                        qc_ref, vt_ref, bias_ref, m_ref, acc_ref, *, blk):
    h = pl.program_id(1)
    qi = pl.program_id(2)
    slope = _alibi_slope(h)
    tp = k_ref.shape[0]
    sub = blk // LANES

    @pl.when(qi == 0)
    def _():
        for jb in range(tp // LANES):
            vt = v_ref[jb * LANES:(jb + 1) * LANES, :].T.astype(BF16)
            vt_ref[jb // sub, 0:DA_DV, (jb % sub) * LANES:(jb % sub + 1) * LANES] = vt
        vt_ref[:, DA_DV:, :] = jnp.ones((tp // blk, ATT_ONES, blk), BF16)

    lane = lax.broadcasted_iota(jnp.int32, (blk, LANES), 1)
    qb = q_ref[...].astype(BF16)
    for c in range(2):
        qc_ref[c] = jnp.where((lane // DA_DK) == c, qb, jnp.zeros_like(qb))
    bias_ref[...] = slope * d0_ref[...]
    m_ref[...] = jnp.full_like(m_ref, NEG_BIG)
    acc_ref[...] = jnp.zeros_like(acc_ref)

    def chunk(j, masked):
        kb = k_ref[pl.ds(pl.multiple_of(j * blk, blk), blk), :].astype(BF16)
        off = ((qi - j) * blk).astype(F32)
        c0 = slope * off
        if masked:
            kpos = j * blk + lax.broadcasted_iota(jnp.int32, (blk, blk), 0)
            valid = (d0_ref[...] + off >= 0.0) & (kpos >= PAD_LEAD)
        scores = [lax.dot_general(kb, qc_ref[c], (((1,), (1,)), ((), ())), preferred_element_type=F32)
                  for c in range(2)]
        for c in range(2):
            t = scores[c] - bias_ref[...]
            if masked:
                t = jnp.where(valid, t, NEG_BIG)
            m_old = m_ref[c]
            m_new = jnp.maximum(m_old, jnp.max(t, axis=0, keepdims=True) - c0)
            alpha = jnp.exp(m_old - m_new)
            p = jnp.exp(t - (m_new + c0)).astype(BF16)
            acc_ref[c] = alpha * acc_ref[c] + jnp.dot(vt_ref[j], p, preferred_element_type=F32)
            m_ref[c] = m_new

    chunk(0, True)
    lax.fori_loop(1, qi, lambda j, carry: (chunk(j, False), carry)[1], 0)

    @pl.when(qi > 0)
    def _():
        chunk(qi, True)

    lam = _lambda_value(lam_ref)
    a0 = acc_ref[0]
    a1 = acc_ref[1]
    ot = a0[:DA_DV] / a0[DA_DV:DA_DV + 1] - lam * (a1[:DA_DV] / a1[DA_DV:DA_DV + 1])
    o_ref[...] = _sub_norm(ot.T, sg_ref[...])


def _attn_prompt(q, k, v, da_lam, subln_g, blk):
    b, tp, _ = q.shape
    assert tp % blk == 0 and blk % LANES == 0 and PAD_LEAD <= blk
    d0t = (jnp.arange(blk, dtype=F32)[None, :] - jnp.arange(blk, dtype=F32)[:, None])
    qspec = pl.BlockSpec((None, blk, DA_DV), lambda bi, h, i: (bi, i, h))
    kspec = pl.BlockSpec((None, tp, DA_DV), lambda bi, h, i: (bi, 0, h))
    full = lambda a: pl.BlockSpec(a.shape, lambda bi, h, i: (0,) * a.ndim)
    return pl.pallas_call(
        functools.partial(_attn_prompt_kernel, blk=blk),
        grid=(b, DA_HEADS, tp // blk),
        in_specs=[qspec, kspec, kspec, full(d0t), full(da_lam), full(subln_g)],
        out_specs=qspec,
        scratch_shapes=[pltpu.VMEM((2, blk, DA_DV), BF16),
                        pltpu.VMEM((tp // blk, DA_DV + ATT_ONES, blk), BF16),
                        pltpu.VMEM((blk, blk), F32), pltpu.VMEM((2, 1, blk), F32),
                        pltpu.VMEM((2, DA_DV + ATT_ONES, blk), F32)],
        out_shape=jax.ShapeDtypeStruct((b, tp, DA_V), F32),
        compiler_params=_cparams(("parallel", "parallel", "arbitrary")),
        name="attn_prompt",
    )(q, k, v, d0t, da_lam, subln_g)


def _attn_sample_kernel(pt_ref, q_ref, kn_ref, vn_ref, lam_ref, sg_ref, *rest, ppg, page, past):
    k_refs = rest[:ppg]
    v_refs = rest[ppg:2 * ppg]
    o_ref, m_ref, l_ref, acc_ref = rest[2 * ppg:]
    g = pl.program_id(1)
    nrow = 2 * DA_HEADS
    rowi = lax.broadcasted_iota(jnp.int32, (nrow, DA_QK), 0)
    lanei = lax.broadcasted_iota(jnp.int32, (nrow, DA_QK), 1)
    qsel = (lanei // DA_DK) == rowi
    qmat = jnp.where(qsel, q_ref[...], 0.0)
    rh = lax.broadcasted_iota(jnp.int32, (nrow, 1), 0) // 2
    slope = jnp.full((nrow, 1), 2.0 ** (-8.0), F32)
    for i in range(DA_HEADS - 2, -1, -1):
        slope = jnp.where(rh == i, jnp.float32(2.0 ** (-8.0 * (i + 1) / DA_HEADS)), slope)

    @pl.when(g == 0)
    def _():
        m_ref[...] = jnp.full_like(m_ref, NEG_BIG)
        l_ref[...] = jnp.zeros_like(l_ref)
        acc_ref[...] = jnp.zeros_like(acc_ref)

    qb = qmat.astype(BF16)
    kpos = g * (ppg * page) + lax.broadcasted_iota(jnp.int32, (nrow, ppg * page), 1)
    s = jnp.concatenate([jnp.dot(qb, k_refs[i][...].astype(BF16), preferred_element_type=F32)
                         for i in range(ppg)], axis=1)
    s = s - slope * (past - kpos).astype(F32)
    m_old = m_ref[...]
    m_new = jnp.maximum(m_old, jnp.max(s, axis=-1, keepdims=True))
    alpha = jnp.exp(m_old - m_new)
    p = jnp.exp(s - m_new)
    l_ref[...] = alpha * l_ref[...] + jnp.sum(p, axis=-1, keepdims=True)
    pb = p.astype(BF16)
    pv = [jnp.zeros((nrow, DA_DV), F32)] * DA_HEADS
    for i in range(ppg):
        for hh in range(DA_HEADS):
            vh = v_refs[i][pl.ds(hh, page, stride=DA_HEADS), :].astype(BF16)
            pv[hh] = pv[hh] + jnp.dot(pb[:, i * page:(i + 1) * page], vh, preferred_element_type=F32)
    tot = pv[DA_HEADS - 1]
    for hh in range(DA_HEADS - 1):
        tot = jnp.where(rh == hh, pv[hh], tot)
    acc_ref[...] = alpha * acc_ref[...] + tot
    m_ref[...] = m_new

    @pl.when(g == pl.num_programs(1) - 1)
    def _():
        kn = kn_ref[...].astype(BF16).astype(F32)
        s_self = jnp.sum(qb.astype(F32) * kn, axis=-1, keepdims=True)
        m_old = m_ref[...]
        m_new = jnp.maximum(m_old, s_self)
        alpha = jnp.exp(m_old - m_new)
        p_self = jnp.exp(s_self - m_new)
        l_fin = alpha * l_ref[...] + p_self
        vn = vn_ref[...].astype(BF16).astype(F32)
        vrow = jnp.zeros((nrow, DA_DV), F32)
        for hh in range(DA_HEADS):
            vrow = jnp.where(rh == hh, vn[:, hh * DA_DV:(hh + 1) * DA_DV], vrow)
        acc = alpha * acc_ref[...] + p_self.astype(BF16).astype(F32) * vrow
        lam = _lambda_value(lam_ref)
        ri = lax.broadcasted_iota(jnp.int32, (nrow, 1), 0)
        t = acc / l_fin * jnp.where(ri % 2 == 0, 1.0, -lam)
        outs = []
        for hh in range(DA_HEADS):
            o = t[2 * hh:2 * hh + 1, :] + t[2 * hh + 1:2 * hh + 2, :]
            outs.append(_sub_norm(o, sg_ref[...]))
        o_ref[...] = jnp.concatenate(outs, axis=1)


def _attn_sample(q, kn, vn, cache_kt, cache_v, page_table, da_lam, subln_g, ppg):
    bd = q.shape[0]
    n_pages = page_table.shape[1]
    page = cache_kt.shape[2]
    assert n_pages % ppg == 0 and cache_v.shape[1:] == (page * DA_HEADS, DA_DV)
    pt = page_table.reshape(-1)
    one = pl.BlockSpec((None, 1, DA_QK), lambda b, g, pt: (b, 0, 0))
    full = lambda a: pl.BlockSpec(a.shape, lambda b, g, pt: (0,) * a.ndim)

    def pspec(i):
        return pl.BlockSpec((None, DA_QK, page),
                            lambda b, g, pt, i=i: (pt[b * n_pages + g * ppg + i], 0, 0))

    out = pl.pallas_call(
        functools.partial(_attn_sample_kernel, ppg=ppg, page=page, past=n_pages * page),
        grid_spec=pltpu.PrefetchScalarGridSpec(
            num_scalar_prefetch=1,
            grid=(bd, n_pages // ppg),
            in_specs=[one, one, one, full(da_lam), full(subln_g)]
                     + [pspec(i) for i in range(ppg)] * 2,
            out_specs=one,
            scratch_shapes=[pltpu.VMEM((2 * DA_HEADS, 1), F32), pltpu.VMEM((2 * DA_HEADS, 1), F32),
                            pltpu.VMEM((2 * DA_HEADS, DA_DV), F32)],
        ),
        out_shape=jax.ShapeDtypeStruct((bd, 1, DA_V), F32),
        compiler_params=_cparams(("parallel", "arbitrary")),
        name="attn_sample",
    )(pt, q.reshape(bd, 1, DA_QK), kn.reshape(bd, 1, DA_QK), vn.reshape(bd, 1, DA_V), da_lam, subln_g,
      *([cache_kt] * ppg), *([cache_v] * ppg))
    return out.reshape(bd, DA_V)


def _merge_kernel(y_ref, bonus_ref, g_ref, lng_ref, lnb_ref, o_ref, gt_ref, h_ref, wrw_ref, wda_ref,
                  wout_ref, n2_ref, wq_ref, sk_ref, h2_ref, xn_ref, st_ref):
    j = _seg_ones()
    y = y_ref[...]
    mu = _seg64(y, j) * (1.0 / RW_HEAD)
    d = y - mu
    var = _seg64(d * d, j) * (1.0 / RW_HEAD)
    yln = d * lax.rsqrt(var + RW_LN_EPS) * lng_ref[...] + lnb_ref[...]
    yr = ((yln + bonus_ref[...]) * g_ref[...]).astype(BF16)
    y_rw = jnp.dot(yr, wrw_ref[...], preferred_element_type=F32)
    y_da = jnp.dot(o_ref[...].astype(BF16), wda_ref[...], preferred_element_type=F32)
    gt = gt_ref[...]
    mix = gt[:, :D_MODEL] * y_rw + gt[:, D_MODEL:] * y_da
    h2 = h_ref[...] + jnp.dot(mix.astype(BF16), wout_ref[...], preferred_element_type=F32)
    h2_ref[...] = h2
    ms = jnp.mean(h2 * h2, axis=-1, keepdims=True)
    xn = (h2 * lax.rsqrt(ms + NORM_EPS) * n2_ref[...]).astype(BF16)
    xn_ref[...] = xn
    q = jnp.dot(xn, wq_ref[...], preferred_element_type=F32)
    for hc in range(2 * PEER_HEADS):
        qc = q[:, hc * PEER_DHALF:(hc + 1) * PEER_DHALF].astype(BF16)
        st_ref[hc * N_KEYS:(hc + 1) * N_KEYS, :] = lax.dot_general(
            sk_ref[hc % 2], qc, (((1,), (1,)), ((), ())), preferred_element_type=F32)


def _merge(y, bonus, g, o, gates, h, consts, tm, tile_off):
    b, rows, _ = y.shape
    assert rows % tm == 0
    nt = rows // tm - tile_off
    rin = lambda c: pl.BlockSpec((None, tm, c), lambda i, t: (i, t + tile_off, 0))
    full = lambda a: pl.BlockSpec(a.shape, lambda i, t: (0,) * a.ndim)
    ntok = b * nt * tm
    return pl.pallas_call(
        _merge_kernel,
        grid=(b, nt),
        in_specs=[rin(RW_DIM), rin(RW_DIM), rin(RW_DIM), full(consts[0]), full(consts[1]), rin(DA_V),
                  rin(2 * D_MODEL), rin(D_MODEL)] + [full(a) for a in consts[2:]],
        out_specs=[pl.BlockSpec((tm, D_MODEL), lambda i, t: (i * nt + t, 0)),
                   pl.BlockSpec((tm, D_MODEL), lambda i, t: (i * nt + t, 0)),
                   pl.BlockSpec((2 * PEER_HEADS * N_KEYS, tm), lambda i, t: (0, i * nt + t))],
        out_shape=[jax.ShapeDtypeStruct((ntok, D_MODEL), F32),
                   jax.ShapeDtypeStruct((ntok, D_MODEL), BF16),
                   jax.ShapeDtypeStruct((2 * PEER_HEADS * N_KEYS, ntok), F32)],
        compiler_params=_cparams(("parallel", "parallel")),
        name="merge",
    )(y, bonus, g, consts[0], consts[1], o, gates, h, *consts[2:])


def _top16(x):
    nk = x.shape[0]
    iota = lax.broadcasted_iota(jnp.int32, x.shape, 0).astype(F32)
    rank = jnp.full(x.shape, float(nk), F32)
    vals = []
    for r in range(PEER_TOPK):
        m = jnp.max(x, axis=0, keepdims=True)
        cand = jnp.where(x == m, iota, float(nk))
        sel = cand == jnp.min(cand, axis=0, keepdims=True)
        x = jnp.where(sel, -jnp.inf, x)
        rank = jnp.where(sel, float(r), rank)
        vals.append(m)
    return vals, rank


def _peer_topk_kernel(st_ref, rank2_ref, c2_ref, cnt1_ref, c1_ref):
    n = st_ref.shape[1]
    rid = lax.broadcasted_iota(jnp.int32, (_CAND_ROWS, n), 0)
    flat = jnp.zeros((_CAND_ROWS, n), jnp.int32) + PEER_TOPK * PEER_TOPK
    irow = jnp.zeros((_CAND_ROWS, n), jnp.int32) + PEER_TOPK
    for ci, (i, j) in enumerate(_CAND):
        flat = jnp.where(rid == ci, i * PEER_TOPK + j, flat)
        irow = jnp.where(rid == ci, i, irow)
    for h in range(PEER_HEADS):
        s1 = st_ref[(2 * h) * N_KEYS:(2 * h + 1) * N_KEYS, :]
        s2 = st_ref[(2 * h + 1) * N_KEYS:(2 * h + 2) * N_KEYS, :]
        v1, rank1 = _top16(s1)
        v2, rank2 = _top16(s2)
        rows = [v1[i] + v2[j] for (i, j) in _CAND]
        rows += [jnp.full((1, n), -jnp.inf, F32)] * (_CAND_ROWS - len(_CAND))
        cand = jnp.concatenate(rows, axis=0)
        top = v1[0] + v2[0]
        chosen = jnp.zeros(cand.shape, jnp.bool_)
        work = cand
        for _ in range(PEER_TOPK):
            m = jnp.max(work, axis=0, keepdims=True)
            fl = jnp.where(work == m, flat, PEER_TOPK * PEER_TOPK + 1)
            sel = fl == jnp.min(fl, axis=0, keepdims=True)
            work = jnp.where(sel, -jnp.inf, work)
            chosen = chosen | sel
        z = jnp.sum(jnp.where(chosen, jnp.exp(cand - top), 0.0), axis=0, keepdims=True)
        cnt1 = jnp.zeros(s1.shape, F32)
        for i in range(PEER_TOPK):
            m_i = jnp.sum(jnp.where(chosen & (irow == i), 1.0, 0.0), axis=0, keepdims=True)
            cnt1 = jnp.where(rank1 == float(i), m_i, cnt1)
        cnt1_ref[h] = cnt1
        c1_ref[h] = jnp.where(rank1 < float(PEER_TOPK), jnp.exp(s1 - v1[0]) / z, 0.0)
        c2_ref[h] = jnp.where(rank2 < float(PEER_TOPK), jnp.exp(s2 - v2[0]), 0.0).astype(BF16)
        rank2_ref[h] = rank2.astype(BF16)


def _peer_topk(st, tn):
    ntok = st.shape[1]
    assert ntok % tn == 0
    spec = pl.BlockSpec((PEER_HEADS, N_KEYS, tn), lambda i: (0, 0, i))
    return pl.pallas_call(
        _peer_topk_kernel,
        grid=(ntok // tn,),
        in_specs=[pl.BlockSpec((st.shape[0], tn), lambda i: (0, i))],
        out_specs=[spec] * 4,
        out_shape=[jax.ShapeDtypeStruct((PEER_HEADS, N_KEYS, ntok), dt) for dt in (BF16, BF16, F32, F32)],
        compiler_params=_cparams(("parallel",)),
        name="peer_topk",
    )(st)


def _peer_dense_kernel(x_ref, u_ref, vt_ref, rank2_ref, c2_ref, cnt1_ref, c1_ref, h2_ref, y_ref,
                       acc_ref, hid_ref, p_ref, *, e1b):
    j = pl.program_id(1)
    tn = x_ref.shape[0]
    ngrp = N_KEYS // BF16_ROWS

    @pl.when(j == 0)
    def _():
        acc_ref[...] = jnp.zeros_like(acc_ref)

    hid_ref[...] = lax.dot_general(u_ref[...], x_ref[...], (((1,), (1,)), ((), ())),
                                   preferred_element_type=F32)
    for e in range(e1b):
        e1 = j * e1b + e
        gate = [None] * ngrp
        for h in range(PEER_HEADS):
            cnt = jnp.broadcast_to(cnt1_ref[h, pl.ds(e1, 1), :], (BF16_ROWS, tn)).astype(BF16)
            c1 = jnp.broadcast_to(c1_ref[h, pl.ds(e1, 1), :], (BF16_ROWS, tn)).astype(BF16)
            for g in range(ngrp):
                rows = slice(g * BF16_ROWS, (g + 1) * BF16_ROWS)
                c2 = c2_ref[h, rows, :]
                term = jnp.where(rank2_ref[h, rows, :] < cnt, c2, jnp.zeros_like(c2)) * c1
                gate[g] = term if gate[g] is None else gate[g] + term
        hh = hid_ref[e * N_KEYS:(e + 1) * N_KEYS, :]
        gelu = (0.5 * hh * (1.0 + lax.erf(hh * (2.0 ** -0.5)))).astype(BF16)
        for g in range(ngrp):
            r0 = e * N_KEYS + g * BF16_ROWS
            p_ref[r0:r0 + BF16_ROWS, :] = gelu[g * BF16_ROWS:(g + 1) * BF16_ROWS] * gate[g]
    acc_ref[...] += jnp.dot(vt_ref[...], p_ref[...], preferred_element_type=F32)

    @pl.when(j == pl.num_programs(1) - 1)
    def _():
        y_ref[...] = h2_ref[...] + acc_ref[...].T


def _peer_dense(xn, u_bf, vt_bf, rank2, c2, cnt1, c1, h2, tn, e1b):
    ntok = xn.shape[0]
    assert ntok % tn == 0 and N_KEYS % e1b == 0
    tok = pl.BlockSpec((PEER_HEADS, N_KEYS, tn), lambda i, j: (0, 0, i))
    row = pl.BlockSpec((tn, D_MODEL), lambda i, j: (i, 0))
    return pl.pallas_call(
        functools.partial(_peer_dense_kernel, e1b=e1b),
        grid=(ntok // tn, N_KEYS // e1b),
        in_specs=[row,
                  pl.BlockSpec((e1b * N_KEYS, D_MODEL), lambda i, j: (j, 0)),
                  pl.BlockSpec((D_MODEL, e1b * N_KEYS), lambda i, j: (0, j)),
                  tok, tok, tok, tok, row],
        out_specs=row,
        out_shape=jax.ShapeDtypeStruct((ntok, D_MODEL), F32),
        scratch_shapes=[pltpu.VMEM((D_MODEL, tn), F32), pltpu.VMEM((e1b * N_KEYS, tn), F32),
                        pltpu.VMEM((e1b * N_KEYS, tn), BF16)],
        compiler_params=_cparams(("parallel", "arbitrary")),
        name="peer_dense",
    )(xn, u_bf, vt_bf, rank2, c2, cnt1, c1, h2)


def _state_in(s):
    b = s.shape[0]
    return jnp.transpose(s, (0, 2, 1, 3)).reshape(b, RW_HEAD, RW_DIM)


def _state_out(s):
    b = s.shape[0]
    return jnp.transpose(s.reshape(b, RW_HEAD, RW_HEADS, RW_HEAD), (0, 2, 1, 3))


def _pad_tokens(a, n, axis):
    pad = [(0, 0)] * a.ndim
    pad[axis] = (0, n - a.shape[axis])
    return jnp.pad(a, pad)


def kernel(x_prompt, x_sample, cache_k, cache_v, state_wkv, state_shift, page_table, meta, norm1_g, w_in,
           rw_mu, rw_w0, rw_w2, rw_a0, rw_a2, rw_g2, rw_kk, rw_ka, rw_rk, rw_ln_g, rw_ln_b, w_rw_br,
           da_qn_g, da_kn_g, da_lam, da_subln_g, w_da_br, w_out, norm2_g, peer_wq, peer_subkeys,
           peer_u, peer_v):
    b, seq, _ = x_prompt.shape
    bd = x_sample.shape[0]
    t_real = N_META + seq
    tp = PAD_LEAD + t_real
    assert tp % ROW_TILE == 0 and tp % SCAN_CHUNK == 0 and x_sample.shape[1] == 1
    row2 = lambda a: a.reshape(1, -1)

    w_in_bf = w_in[0].astype(BF16)
    g1 = row2(norm1_g[0])
    qg = row2(jnp.tile(da_qn_g[0], DA_QK // DA_DK))
    kg = row2(jnp.tile(da_kn_g[0], DA_QK // DA_DK))
    zlo = jnp.zeros((W_LORA, RW_DIM), F32)
    prep_params = (row2(rw_mu[0]), row2(rw_w0[0]),
                   jnp.concatenate([rw_w2[0], zlo], axis=0).astype(BF16), row2(rw_a0[0]),
                   jnp.concatenate([zlo, rw_a2[0]], axis=0).astype(BF16), rw_g2[0].astype(BF16),
                   row2(rw_kk[0]), row2(rw_ka[0]), row2(rw_rk[0]))
    merge_consts = (row2(rw_ln_g[0]), row2(rw_ln_b[0]), w_rw_br[0].astype(BF16), w_da_br[0].astype(BF16),
                    w_out[0].astype(BF16), row2(norm2_g[0]), peer_wq[0].astype(BF16),
                    peer_subkeys[0].astype(BF16))
    subln = row2(da_subln_g[0])
    u_bf = peer_u[0].astype(BF16)
    vt_bf = peer_v[0].astype(BF16).T

    hp = jnp.concatenate([jnp.zeros((b, PAD_LEAD, D_MODEL), F32),
                          jnp.broadcast_to(meta[None], (b, N_META, D_MODEL)), x_prompt], axis=1)
    prw, q, k, v, gates = _in_proj(hp.reshape(b * tp, D_MODEL), g1, w_in_bf, qg, kg, 256)
    r3 = lambda a: a.reshape(b, tp, -1)
    pr, pw, pk, pv, pa, pb, pg, pbonus = _rwkv_prep_prompt(r3(prw), prep_params, 384)
    y_scan, s_fin = _rwkv_scan(pr, pw, pk, pv, pa, pb, jnp.zeros((b, RW_HEAD, RW_DIM), F32))
    o_attn = _attn_prompt(r3(q), r3(k), r3(v), da_lam[0], subln, 384)
    h2, xn2, st = _merge(y_scan, pbonus, pg, o_attn, r3(gates), hp, merge_consts, ROW_TILE, 1)
    y_prompt = _peer_dense(xn2, u_bf, vt_bf, *_peer_topk(st, 128), h2, 512, 8)

    xs = x_sample.reshape(bd, D_MODEL)
    prw_s, q_s, k_s, v_s, gates_s = _in_proj(xs, g1, w_in_bf, qg, kg, bd)
    sr, sw, sk, sv, sa, sb, sg, sbonus = _rwkv_prep_sample(prw_s, state_shift[0], prep_params)
    ys_scan, ss_fin = _rwkv_step(sr, sw, sk, sv, sa, sb, _state_in(state_wkv[0]))
    n_phys = cache_k.shape[1]
    ckt = jnp.transpose(cache_k[0].reshape(n_phys, -1, DA_QK), (0, 2, 1))
    cv = cache_v[0].reshape(n_phys, -1, DA_DV)
    o_s = _attn_sample(q_s, k_s, v_s, ckt, cv, page_table, da_lam[0], subln, 8)
    f3 = lambda a: a.reshape(1, bd, -1)
    h2s, xn2s, sts = _merge(ys_scan.reshape(1, bd, RW_DIM), f3(sbonus), f3(sg), f3(o_s), f3(gates_s),
                            f3(xs), merge_consts, bd, 0)
    npad = LANES
    tk_s = _peer_topk(_pad_tokens(sts, npad, 1), npad)
    y_s = _peer_dense(_pad_tokens(xn2s, npad, 0), u_bf, vt_bf, *tk_s, _pad_tokens(h2s, npad, 0), npad, 8)

    y_prompt = y_prompt.reshape(b, seq, D_MODEL)
    y_sample = y_s[:bd].reshape(bd, 1, D_MODEL)
    k_p = r3(k)[:, PAD_LEAD:].reshape(1, b, t_real, DA_HEADS, 2, DA_DK)
    v_p = r3(v)[:, PAD_LEAD:].reshape(1, b, t_real, DA_HEADS, DA_DV)
    return (y_prompt, y_sample, k_p, v_p,
            k_s.reshape(1, bd, 1, DA_HEADS, 2, DA_DK), v_s.reshape(1, bd, 1, DA_HEADS, DA_DV),
            _state_out(s_fin)[None], _state_out(ss_fin)[None],
            r3(prw)[:, -1][None], prw_s[None])
```

```python
import functools
import math

import jax
import jax.numpy as jnp
from jax import lax
from jax.experimental import pallas as pl
from jax.experimental.pallas import tpu as pltpu

F32 = jnp.float32
BF16 = jnp.bfloat16

D_MODEL = 1024
N_META = 16
NORM_EPS = 1e-6
RW_HEAD = 64
RW_DIM = D_MODEL // 2
RW_HEADS = RW_DIM // RW_HEAD
W_LORA = 64
A_LORA = 64
G_LORA = 128
RW_PROJ = 3 * RW_DIM + W_LORA + A_LORA + G_LORA
RW_LN_EPS = 64e-5
DA_DK = 64
DA_DV = 2 * DA_DK
DA_DIM = D_MODEL // 2
DA_HEADS = DA_DIM // DA_DV
DA_QK = DA_HEADS * 2 * DA_DK
DA_V = DA_HEADS * DA_DV
N_KEYS = 128
PEER_HEADS = 8
PEER_TOPK = 16
PEER_DQ = 256
PEER_DHALF = PEER_DQ // 2
LAM_INIT = 0.8 - 0.6 * math.exp(-0.3 * 0)

LANES = 128
BF16_ROWS = 16
ROW_TILE = 128
PAD_LEAD = ROW_TILE - N_META
SCAN_CHUNK = 64
NEG_BIG = -1e30
VMEM_LIMIT = 56 * 1024 * 1024

O1 = RW_PROJ
O2 = O1 + DA_QK
O3 = O2 + DA_QK
O4 = O3 + DA_V
O5 = O4 + D_MODEL
O6 = O5 + D_MODEL

_CAND = [(i, j) for i in range(PEER_TOPK) for j in range(PEER_TOPK) if (i + 1) * (j + 1) <= PEER_TOPK]
_CAND_ROWS = -(-len(_CAND) // 8) * 8


def _cparams(sem):
    return pltpu.CompilerParams(dimension_semantics=sem, vmem_limit_bytes=VMEM_LIMIT)


def _split_dot(x, j):
    hi = x.astype(BF16)
    lo = (x - hi.astype(F32)).astype(BF16)
    return (jnp.dot(hi, j, preferred_element_type=F32)
            + jnp.dot(lo, j, preferred_element_type=F32))


def _seg64(x, j):
    outs = [_split_dot(x[:, c * LANES:(c + 1) * LANES], j) for c in range(x.shape[1] // LANES)]
    return outs[0] if len(outs) == 1 else jnp.concatenate(outs, axis=1)


def _seg_ones():
    r = lax.broadcasted_iota(jnp.int32, (LANES, LANES), 0) // 64
    c = lax.broadcasted_iota(jnp.int32, (LANES, LANES), 1) // 64
    return (r == c).astype(BF16)


def _in_proj_kernel(x_ref, g1_ref, w_ref, qg_ref, kg_ref, prw_ref, q_ref, k_ref, v_ref, gt_ref):
    x = x_ref[...]
    ms = jnp.mean(x * x, axis=-1, keepdims=True)
    xn = (x * lax.rsqrt(ms + NORM_EPS) * g1_ref[...]).astype(BF16)

    def mm(a, b):
        return jnp.dot(xn, w_ref[:, a:b], preferred_element_type=F32)

    j = _seg_ones()

    def head_norm(t, g):
        msq = _seg64(t * t, j) * (1.0 / DA_DK)
        return t * lax.rsqrt(msq + NORM_EPS) * g

    prw_ref[...] = mm(0, O1)
    q_ref[...] = head_norm(mm(O1, O2), qg_ref[...]) * (DA_DK ** -0.5)
    k_ref[...] = head_norm(mm(O2, O3), kg_ref[...])
    v_ref[...] = mm(O3, O4)
    gt_ref[...] = jax.nn.sigmoid(mm(O4, O6))


def _in_proj(x, g1, w_bf, qg, kg, tm):
    n = x.shape[0]
    assert n % tm == 0
    row = lambda c: pl.BlockSpec((tm, c), lambda i: (i, 0))
    full = lambda a: pl.BlockSpec(a.shape, lambda i: (0,) * a.ndim)
    widths = (RW_PROJ, DA_QK, DA_QK, DA_V, 2 * D_MODEL)
    return pl.pallas_call(
        _in_proj_kernel,
        grid=(n // tm,),
        in_specs=[row(D_MODEL), full(g1), full(w_bf), full(qg), full(kg)],
        out_specs=[row(c) for c in widths],
        out_shape=[jax.ShapeDtypeStruct((n, c), F32) for c in widths],
        compiler_params=_cparams(("parallel",)),
        name="in_proj",
    )(x, g1, w_bf, qg, kg)


def _rwkv_prep_math(p, p_prev, mu, w0, w2p, a0, a2p, g2, kkp, kap, rk):
    ps = p + (p_prev - p) * mu
    r = ps[:, 0:RW_DIM]
    k = ps[:, RW_DIM:2 * RW_DIM]
    v = ps[:, 2 * RW_DIM:3 * RW_DIM]
    wa = ps[:, 3 * RW_DIM:3 * RW_DIM + W_LORA + A_LORA]
    gl = ps[:, 3 * RW_DIM + W_LORA + A_LORA:RW_PROJ]
    w = -jax.nn.softplus(-(w0 + jnp.dot(jnp.tanh(wa).astype(BF16), w2p, preferred_element_type=F32))) - 0.5
    decay = jnp.exp(-jnp.exp(w))
    a = jax.nn.sigmoid(a0 + jnp.dot(wa.astype(BF16), a2p, preferred_element_type=F32))
    g = jnp.dot(jax.nn.sigmoid(gl).astype(BF16), g2, preferred_element_type=F32)
    j = _seg_ones()
    kk = k * kkp
    kk = kk / jnp.maximum(jnp.sqrt(_seg64(kk * kk, j)), 1e-12)
    k2 = k * (1.0 + (a - 1.0) * kap)
    bonus = _seg64(r * k2 * rk, j) * v
    return r, decay, k2, v, -kk, kk * a, g, bonus


def _rwkv_prep_carry_kernel(p_ref, mu, w0, w2p, a0, a2p, g2, kkp, kap, rk, *rest):
    outs, carry = rest[:-1], rest[-1]

    @pl.when(pl.program_id(1) == 0)
    def _():
        carry[...] = jnp.zeros_like(carry)

    p = p_ref[...]
    rows = lax.broadcasted_iota(jnp.int32, p.shape, 0)
    p_prev = jnp.where(rows == 0, carry[...], pltpu.roll(p, 1, 0))
    carry[...] = p[p.shape[0] - 1:, :]
    res = _rwkv_prep_math(p, p_prev, mu[...], w0[...], w2p[...], a0[...], a2p[...], g2[...],
                          kkp[...], kap[...], rk[...])
    for o, val in zip(outs, res):
        o[...] = val


def _rwkv_prep_given_kernel(p_ref, pp_ref, mu, w0, w2p, a0, a2p, g2, kkp, kap, rk, *outs):
    res = _rwkv_prep_math(p_ref[...], pp_ref[...], mu[...], w0[...], w2p[...], a0[...], a2p[...],
                          g2[...], kkp[...], kap[...], rk[...])
    for o, val in zip(outs, res):
        o[...] = val


def _rwkv_prep_prompt(p3, params, tm):
    b, tp, _ = p3.shape
    assert tp % tm == 0
    full = lambda a: pl.BlockSpec(a.shape, lambda i, j: (0,) * a.ndim)
    return pl.pallas_call(
        _rwkv_prep_carry_kernel,
        grid=(b, tp // tm),
        in_specs=[pl.BlockSpec((None, tm, RW_PROJ), lambda i, j: (i, j, 0))] + [full(a) for a in params],
        out_specs=[pl.BlockSpec((None, tm, RW_DIM), lambda i, j: (i, j, 0))] * 8,
        out_shape=[jax.ShapeDtypeStruct((b, tp, RW_DIM), F32)] * 8,
        scratch_shapes=[pltpu.VMEM((1, RW_PROJ), F32)],
        compiler_params=_cparams(("arbitrary", "arbitrary")),
        name="rwkv_prep_prompt",
    )(p3, *params)


def _rwkv_prep_sample(p, p_prev, params):
    n = p.shape[0]
    full = lambda a: pl.BlockSpec(a.shape, lambda i: (0,) * a.ndim)
    return pl.pallas_call(
        _rwkv_prep_given_kernel,
        grid=(1,),
        in_specs=[full(p), full(p_prev)] + [full(a) for a in params],
        out_specs=[pl.BlockSpec((n, RW_DIM), lambda i: (0, 0))] * 8,
        out_shape=[jax.ShapeDtypeStruct((n, RW_DIM), F32)] * 8,
        compiler_params=_cparams(("arbitrary",)),
        name="rwkv_prep_sample",
    )(p, p_prev, *params)


def _scan_consts():
    lane = lax.broadcasted_iota(jnp.int32, (RW_HEAD, LANES), 1)
    sub = lax.broadcasted_iota(jnp.int32, (RW_HEAD, LANES), 0)
    first = lane < RW_HEAD
    eye_lo = (lane == sub).astype(F32)
    eye_hi = (lane - RW_HEAD == sub).astype(F32)
    return lane, first, eye_lo, eye_hi


def _seg(first, x_lo, x_hi):
    lo = jnp.sum(x_lo, axis=-1, keepdims=True)
    hi = jnp.sum(x_hi, axis=-1, keepdims=True)
    return jnp.where(first, lo, hi)


def _split_row(first, row):
    lo = jnp.where(first[0:1, :], row, 0.0)
    return lo, row - lo


def _scan_step_kernel(r_ref, w_ref, k_ref, v_ref, a_ref, b_ref, s0_ref, y_ref, s_ref, *, nb):
    _, first, eye_lo, eye_hi = _scan_consts()
    for b in range(nb):
        for p in range(RW_DIM // LANES):
            cols = pl.ds(p * LANES, LANES)
            r_t, w_t, k_t, v_t, a_t, b_t = [ref[pl.ds(b, 1), cols]
                                            for ref in (r_ref, w_ref, k_ref, v_ref, a_ref, b_ref)]
            s = s0_ref[b, :, cols]
            a_lo, a_hi = _split_row(first, a_t)
            r_lo, r_hi = _split_row(first, r_t)
            sa = _seg(first, s * a_lo, s * a_hi)
            vcol = _seg(first, eye_lo * v_t, eye_hi * v_t)
            s = s * w_t + sa * b_t + vcol * k_t
            s_ref[b, :, cols] = s
            ycol = _seg(first, s * r_lo, s * r_hi)
            y_ref[pl.ds(b, 1), cols] = jnp.sum((eye_lo + eye_hi) * ycol, axis=0, keepdims=True)


def _rwkv_step(r, w, k, v, a, b, s0):
    nb = r.shape[0]
    full = lambda x: pl.BlockSpec(x.shape, lambda i: (0,) * x.ndim)
    return pl.pallas_call(
        functools.partial(_scan_step_kernel, nb=nb),
        grid=(1,),
        in_specs=[full(r)] * 6 + [full(s0)],
        out_specs=[full(r), full(s0)],
        out_shape=[jax.ShapeDtypeStruct(r.shape, F32), jax.ShapeDtypeStruct(s0.shape, F32)],
        compiler_params=_cparams(("arbitrary",)),
        name="rwkv_step",
    )(r, w, k, v, a, b, s0)


def _scan_kernel(r_ref, w_ref, k_ref, v_ref, a_ref, b_ref, s0_ref, y_ref, sfin_ref,
                 s_scr, sa_scr, vc_scr, yacc_scr, vstage, zstage, yres, *, nb):
    npair = RW_DIM // LANES
    chains = [(b, p) for b in range(nb) for p in range(npair)]
    lane, first, eye_lo, eye_hi = _scan_consts()
    sub_t = 8

    @pl.when(pl.program_id(0) == 0)
    def _():
        for ci, (b, p) in enumerate(chains):
            s_scr[ci] = s0_ref[b, :, p * LANES:(p + 1) * LANES]

    yacc_scr[...] = jnp.zeros_like(yacc_scr)
    nch = len(chains)
    eye2 = eye_lo + eye_hi
    seg_ones = _seg_ones()
    zero_b = jnp.zeros((LANES, LANES), BF16)
    j_stack = jnp.concatenate([seg_ones, seg_ones], axis=0)
    j_pair = jnp.concatenate([jnp.concatenate([seg_ones, zero_b], axis=1),
                              jnp.concatenate([zero_b, seg_ones], axis=1)], axis=0)

    def prepare(ci, s, a_t):
        a_lo, a_hi = _split_row(first, a_t)
        sa_scr[ci] = _seg(first, s * a_lo, s * a_hi)

    def group(tg, carry):
        rows = pl.ds(pl.multiple_of(tg * sub_t, sub_t), sub_t)

        def row(ref, ci, i):
            b, p = chains[ci]
            return ref[b, rows, pl.ds(p * LANES, LANES)][i:i + 1, :]

        for i in range(sub_t):
            for ci in range(nch):
                v_t = row(v_ref, ci, i)
                v_hi = v_t.astype(BF16).astype(F32)
                lhs = jnp.concatenate([eye2 * v_hi, eye2 * (v_t - v_hi)], axis=1)
                vstage[pl.ds((i * nch + ci) * RW_HEAD, RW_HEAD), :] = lhs.astype(BF16)
        vc_scr[...] = jnp.dot(vstage[...], j_stack, preferred_element_type=F32).reshape(vc_scr.shape)

        for ci in range(nch):
            prepare(ci, s_scr[ci], row(a_ref, ci, 0))
        for i in range(sub_t):
            for ci in range(nch):
                s = s_scr[ci] * row(w_ref, ci, i) + sa_scr[ci] * row(b_ref, ci, i) \
                    + vc_scr[i * nch + ci] * row(k_ref, ci, i)
                s_scr[ci] = s
                if i + 1 < sub_t:
                    prepare(ci, s, row(a_ref, ci, i + 1))
                z = (s * row(r_ref, ci, i)).astype(BF16)
                zstage[pl.ds((i * (nch // 2) + ci // 2) * RW_HEAD, RW_HEAD),
                       pl.ds((ci % 2) * LANES, LANES)] = z
        yres[...] = jnp.dot(zstage[...], j_pair, preferred_element_type=F32)
        for i in range(sub_t):
            place = (lane % RW_HEAD) == tg * sub_t + i
            for ci in range(nch):
                ycol = yres[pl.ds((i * (nch // 2) + ci // 2) * RW_HEAD, RW_HEAD),
                            pl.ds((ci % 2) * LANES, LANES)]
                yacc_scr[ci] = jnp.where(place, ycol, yacc_scr[ci])
        return carry

    lax.fori_loop(0, SCAN_CHUNK // sub_t, group, 0)

    zeros = jnp.zeros((LANES - RW_HEAD, LANES), F32)
    for ci, (b, p) in enumerate(chains):
        cols = pl.ds(p * LANES, LANES)
        tr = jnp.concatenate([yacc_scr[ci], zeros], axis=0).T
        y_ref[b, :, cols] = jnp.where(first, tr[0:RW_HEAD], pltpu.roll(tr[RW_HEAD:], RW_HEAD, 1))
        sfin_ref[b, :, cols] = s_scr[ci]


def _rwkv_scan(r, w, k, v, a, b, s0):
    nb, tp, _ = r.shape
    assert tp % SCAN_CHUNK == 0 and SCAN_CHUNK == RW_HEAD
    nch = nb * (RW_DIM // LANES)
    seq = pl.BlockSpec((nb, SCAN_CHUNK, RW_DIM), lambda i: (0, i, 0))
    st = pl.BlockSpec((nb, RW_HEAD, RW_DIM), lambda i: (0, 0, 0))
    return pl.pallas_call(
        functools.partial(_scan_kernel, nb=nb),
        grid=(tp // SCAN_CHUNK,),
        in_specs=[seq] * 6 + [st],
        out_specs=[seq, st],
        out_shape=[jax.ShapeDtypeStruct((nb, tp, RW_DIM), F32),
                   jax.ShapeDtypeStruct((nb, RW_HEAD, RW_DIM), F32)],
        scratch_shapes=[pltpu.VMEM((nch, RW_HEAD, LANES), F32),
                        pltpu.VMEM((nch, RW_HEAD, LANES), F32),
                        pltpu.VMEM((8 * nch, RW_HEAD, LANES), F32),
                        pltpu.VMEM((nch, RW_HEAD, LANES), F32),
                        pltpu.VMEM((8 * nch * RW_HEAD, 2 * LANES), BF16),
                        pltpu.VMEM((4 * nch * RW_HEAD, 2 * LANES), BF16),
                        pltpu.VMEM((4 * nch * RW_HEAD, 2 * LANES), F32)],
        compiler_params=_cparams(("arbitrary",)),
        name="rwkv_scan",
    )(r, w, k, v, a, b, s0)


def _lambda_value(lam_ref):
    lv = lam_ref[...]
    s01 = jnp.sum(lv[0:1, :] * lv[1:2, :], axis=-1, keepdims=True)
    s23 = jnp.sum(lv[2:3, :] * lv[3:4, :], axis=-1, keepdims=True)
    return jnp.exp(s01) - jnp.exp(s23) + LAM_INIT


def _alibi_slope(h):
    slope = jnp.float32(2.0 ** (-8.0 * DA_HEADS / DA_HEADS))
    for i in range(DA_HEADS - 2, -1, -1):
        slope = jnp.where(h == i, jnp.float32(2.0 ** (-8.0 * (i + 1) / DA_HEADS)), slope)
    return slope


def _sub_norm(o, g):
    return o * lax.rsqrt(jnp.mean(o * o, axis=-1, keepdims=True) + NORM_EPS) * g * (1.0 - LAM_INIT)


ATT_ONES = 16


def _attn_prompt_kernel(q_ref, k_ref, v_ref, d0_ref, lam_ref, sg_ref, o_ref,
                        qc_ref, vt_ref, bias_ref, m_ref, acc_ref, *, blk):
    h = pl.program_id(1)
    qi = pl.program_id(2)
    slope = _alibi_slope(h)
    tp = k_ref.shape[0]
    sub = blk // LANES

    @pl.when(qi == 0)
    def _():
        for jb in range(tp // LANES):
            vt = v_ref[jb * LANES:(jb + 1) * LANES, :].T.astype(BF16)
            vt_ref[jb // sub, 0:DA_DV, (jb % sub) * LANES:(jb % sub + 1) * LANES] = vt
        vt_ref[:, DA_DV:, :] = jnp.ones((tp // blk, ATT_ONES, blk), BF16)

    lane = lax.broadcasted_iota(jnp.int32, (blk, LANES), 1)
    qb = q_ref[...].astype(BF16)
    for c in range(2):
        qc_ref[c] = jnp.where((lane // DA_DK) == c, qb, jnp.zeros_like(qb))
    bias_ref[...] = slope * d0_ref[...]
    m_ref[...] = jnp.full_like(m_ref, NEG_BIG)
    acc_ref[...] = jnp.zeros_like(acc_ref)

    def chunk(j, masked):
        kb = k_ref[pl.ds(pl.multiple_of(j * blk, blk), blk), :].astype(BF16)
        off = ((qi - j) * blk).astype(F32)
        c0 = slope * off
        if masked:
            kpos = j * blk + lax.broadcasted_iota(jnp.int32, (blk, blk), 0)
            valid = (d0_ref[...] + off >= 0.0) & (kpos >= PAD_LEAD)
        scores = [lax.dot_general(kb, qc_ref[c], (((1,), (1,)), ((), ())), preferred_element_type=F32)
                  for c in range(2)]
        for c in range(2):
            t = scores[c] - bias_ref[...]
            if masked:
                t = jnp.where(valid, t, NEG_BIG)
            m_old = m_ref[c]
            m_new = jnp.maximum(m_old, jnp.max(t, axis=0, keepdims=True) - c0)
            alpha = jnp.exp(m_old - m_new)
            p = jnp.exp(t - (m_new + c0)).astype(BF16)
            acc_ref[c] = alpha * acc_ref[c] + jnp.dot(vt_ref[j], p, preferred_element_type=F32)
            m_ref[c] = m_new

    chunk(0, True)
    lax.fori_loop(1, qi, lambda j, carry: (chunk(j, False), carry)[1], 0)

    @pl.when(qi > 0)
    def _():
        chunk(qi, True)

    lam = _lambda_value(lam_ref)
    a0 = acc_ref[0]
    a1 = acc_ref[1]
    ot = a0[:DA_DV] / a0[DA_DV:DA_DV + 1] - lam * (a1[:DA_DV] / a1[DA_DV:DA_DV + 1])
    o_ref[...] = _sub_norm(ot.T, sg_ref[...])


def _attn_prompt(q, k, v, da_lam, subln_g, blk):
    b, tp, _ = q.shape
    assert tp % blk == 0 and blk % LANES == 0 and PAD_LEAD <= blk
    d0t = (jnp.arange(blk, dtype=F32)[None, :] - jnp.arange(blk, dtype=F32)[:, None])
    qspec = pl.BlockSpec((None, blk, DA_DV), lambda bi, h, i: (bi, i, h))
    kspec = pl.BlockSpec((None, tp, DA_DV), lambda bi, h, i: (bi, 0, h))
    full = lambda a: pl.BlockSpec(a.shape, lambda bi, h, i: (0,) * a.ndim)
    return pl.pallas_call(
        functools.partial(_attn_prompt_kernel, blk=blk),
        grid=(b, DA_HEADS, tp // blk),
        in_specs=[qspec, kspec, kspec, full(d0t), full(da_lam), full(subln_g)],
        out_specs=qspec,
        scratch_shapes=[pltpu.VMEM((2, blk, DA_DV), BF16),
                        pltpu.VMEM((tp // blk, DA_DV + ATT_ONES, blk), BF16),
                        pltpu.VMEM((blk, blk), F32), pltpu.VMEM((2, 1, blk), F32),
                        pltpu.VMEM((2, DA_DV + ATT_ONES, blk), F32)],
        out_shape=jax.ShapeDtypeStruct((b, tp, DA_V), F32),
        compiler_params=_cparams(("parallel", "parallel", "arbitrary")),
        name="attn_prompt",
    )(q, k, v, d0t, da_lam, subln_g)


def _attn_sample_kernel(pt_ref, q_ref, kn_ref, vn_ref, lam_ref, sg_ref, *rest, ppg, page, past):
    k_refs = rest[:ppg]
    v_refs = rest[ppg:2 * ppg]
    o_ref, m_ref, l_ref, acc_ref = rest[2 * ppg:]
    g = pl.program_id(1)
    nrow = 2 * DA_HEADS
    rowi = lax.broadcasted_iota(jnp.int32, (nrow, DA_QK), 0)
    lanei = lax.broadcasted_iota(jnp.int32, (nrow, DA_QK), 1)
    qsel = (lanei // DA_DK) == rowi
    qmat = jnp.where(qsel, q_ref[...], 0.0)
    rh = lax.broadcasted_iota(jnp.int32, (nrow, 1), 0) // 2
    slope = jnp.full((nrow, 1), 2.0 ** (-8.0), F32)
    for i in range(DA_HEADS - 2, -1, -1):
        slope = jnp.where(rh == i, jnp.float32(2.0 ** (-8.0 * (i + 1) / DA_HEADS)), slope)

    @pl.when(g == 0)
    def _():
        m_ref[...] = jnp.full_like(m_ref, NEG_BIG)
        l_ref[...] = jnp.zeros_like(l_ref)
        acc_ref[...] = jnp.zeros_like(acc_ref)

    qb = qmat.astype(BF16)
    kpos = g * (ppg * page) + lax.broadcasted_iota(jnp.int32, (nrow, ppg * page), 1)
    s = jnp.concatenate([jnp.dot(qb, k_refs[i][...].astype(BF16), preferred_element_type=F32)
                         for i in range(ppg)], axis=1)
    s = s - slope * (past - kpos).astype(F32)
    m_old = m_ref[...]
    m_new = jnp.maximum(m_old, jnp.max(s, axis=-1, keepdims=True))
    alpha = jnp.exp(m_old - m_new)
    p = jnp.exp(s - m_new)
    l_ref[...] = alpha * l_ref[...] + jnp.sum(p, axis=-1, keepdims=True)
    pb = p.astype(BF16)
    pv = [jnp.zeros((nrow, DA_DV), F32)] * DA_HEADS
    for i in range(ppg):
        for hh in range(DA_HEADS):
            vh = v_refs[i][pl.ds(hh, page, stride=DA_HEADS), :].astype(BF16)
            pv[hh] = pv[hh] + jnp.dot(pb[:, i * page:(i + 1) * page], vh, preferred_element_type=F32)
    tot = pv[DA_HEADS - 1]
    for hh in range(DA_HEADS - 1):
        tot = jnp.where(rh == hh, pv[hh], tot)
    acc_ref[...] = alpha * acc_ref[...] + tot
    m_ref[...] = m_new

    @pl.when(g == pl.num_programs(1) - 1)
    def _():
        kn = kn_ref[...].astype(BF16).astype(F32)
        s_self = jnp.sum(qb.astype(F32) * kn, axis=-1, keepdims=True)
        m_old = m_ref[...]
        m_new = jnp.maximum(m_old, s_self)
        alpha = jnp.exp(m_old - m_new)
        p_self = jnp.exp(s_self - m_new)
        l_fin = alpha * l_ref[...] + p_self
        vn = vn_ref[...].astype(BF16).astype(F32)
        vrow = jnp.zeros((nrow, DA_DV), F32)
        for hh in range(DA_HEADS):
            vrow = jnp.where(rh == hh, vn[:, hh * DA_DV:(hh + 1) * DA_DV], vrow)
        acc = alpha * acc_ref[...] + p_self.astype(BF16).astype(F32) * vrow
        lam = _lambda_value(lam_ref)
        ri = lax.broadcasted_iota(jnp.int32, (nrow, 1), 0)
        t = acc / l_fin * jnp.where(ri % 2 == 0, 1.0, -lam)
        outs = []
        for hh in range(DA_HEADS):
            o = t[2 * hh:2 * hh + 1, :] + t[2 * hh + 1:2 * hh + 2, :]
            outs.append(_sub_norm(o, sg_ref[...]))
        o_ref[...] = jnp.concatenate(outs, axis=1)


def _attn_sample(q, kn, vn, cache_kt, cache_v, page_table, da_lam, subln_g, ppg):
    bd = q.shape[0]
    n_pages = page_table.shape[1]
    page = cache_kt.shape[2]
    assert n_pages % ppg == 0 and cache_v.shape[1:] == (page * DA_HEADS, DA_DV)
    pt = page_table.reshape(-1)
    one = pl.BlockSpec((None, 1, DA_QK), lambda b, g, pt: (b, 0, 0))
    full = lambda a: pl.BlockSpec(a.shape, lambda b, g, pt: (0,) * a.ndim)

    def pspec(i):
        return pl.BlockSpec((None, DA_QK, page),
                            lambda b, g, pt, i=i: (pt[b * n_pages + g * ppg + i], 0, 0))

    out = pl.pallas_call(
        functools.partial(_attn_sample_kernel, ppg=ppg, page=page, past=n_pages * page),
        grid_spec=pltpu.PrefetchScalarGridSpec(
            num_scalar_prefetch=1,
            grid=(bd, n_pages // ppg),
            in_specs=[one, one, one, full(da_lam), full(subln_g)]
                     + [pspec(i) for i in range(ppg)] * 2,
            out_specs=one,
            scratch_shapes=[pltpu.VMEM((2 * DA_HEADS, 1), F32), pltpu.VMEM((2 * DA_HEADS, 1), F32),
                            pltpu.VMEM((2 * DA_HEADS, DA_DV), F32)],
        ),
        out_shape=jax.ShapeDtypeStruct((bd, 1, DA_V), F32),
        compiler_params=_cparams(("parallel", "arbitrary")),
        name="attn_sample",
    )(pt, q.reshape(bd, 1, DA_QK), kn.reshape(bd, 1, DA_QK), vn.reshape(bd, 1, DA_V), da_lam, subln_g,
      *([cache_kt] * ppg), *([cache_v] * ppg))
    return out.reshape(bd, DA_V)


def _merge_kernel(y_ref, bonus_ref, g_ref, lng_ref, lnb_ref, o_ref, gt_ref, h_ref, wrw_ref, wda_ref,
                  wout_ref, n2_ref, wq_ref, sk_ref, h2_ref, xn_ref, st_ref):
    j = _seg_ones()
    y = y_ref[...]
    mu = _seg64(y, j) * (1.0 / RW_HEAD)
    d = y - mu
    var = _seg64(d * d, j) * (1.0 / RW_HEAD)
    yln = d * lax.rsqrt(var + RW_LN_EPS) * lng_ref[...] + lnb_ref[...]
    yr = ((yln + bonus_ref[...]) * g_ref[...]).astype(BF16)
    y_rw = jnp.dot(yr, wrw_ref[...], preferred_element_type=F32)
    y_da = jnp.dot(o_ref[...].astype(BF16), wda_ref[...], preferred_element_type=F32)
    gt = gt_ref[...]
    mix = gt[:, :D_MODEL] * y_rw + gt[:, D_MODEL:] * y_da
    h2 = h_ref[...] + jnp.dot(mix.astype(BF16), wout_ref[...], preferred_element_type=F32)
    h2_ref[...] = h2
    ms = jnp.mean(h2 * h2, axis=-1, keepdims=True)
    xn = (h2 * lax.rsqrt(ms + NORM_EPS) * n2_ref[...]).astype(BF16)
    xn_ref[...] = xn
    q = jnp.dot(xn, wq_ref[...], preferred_element_type=F32)
    for hc in range(2 * PEER_HEADS):
        qc = q[:, hc * PEER_DHALF:(hc + 1) * PEER_DHALF].astype(BF16)
        st_ref[hc * N_KEYS:(hc + 1) * N_KEYS, :] = lax.dot_general(
            sk_ref[hc % 2], qc, (((1,), (1,)), ((), ())), preferred_element_type=F32)


def _merge(y, bonus, g, o, gates, h, consts, tm, tile_off):
    b, rows, _ = y.shape
    assert rows % tm == 0
    nt = rows // tm - tile_off
    rin = lambda c: pl.BlockSpec((None, tm, c), lambda i, t: (i, t + tile_off, 0))
    full = lambda a: pl.BlockSpec(a.shape, lambda i, t: (0,) * a.ndim)
    ntok = b * nt * tm
    return pl.pallas_call(
        _merge_kernel,
        grid=(b, nt),
        in_specs=[rin(RW_DIM), rin(RW_DIM), rin(RW_DIM), full(consts[0]), full(consts[1]), rin(DA_V),
                  rin(2 * D_MODEL), rin(D_MODEL)] + [full(a) for a in consts[2:]],
        out_specs=[pl.BlockSpec((tm, D_MODEL), lambda i, t: (i * nt + t, 0)),
                   pl.BlockSpec((tm, D_MODEL), lambda i, t: (i * nt + t, 0)),
                   pl.BlockSpec((2 * PEER_HEADS * N_KEYS, tm), lambda i, t: (0, i * nt + t))],
        out_shape=[jax.ShapeDtypeStruct((ntok, D_MODEL), F32),
                   jax.ShapeDtypeStruct((ntok, D_MODEL), BF16),
                   jax.ShapeDtypeStruct((2 * PEER_HEADS * N_KEYS, ntok), F32)],
        compiler_params=_cparams(("parallel", "parallel")),
        name="merge",
    )(y, bonus, g, consts[0], consts[1], o, gates, h, *consts[2:])


def _top16(x):
    nk = x.shape[0]
    iota = lax.broadcasted_iota(jnp.int32, x.shape, 0).astype(F32)
    rank = jnp.full(x.shape, float(nk), F32)
    vals = []
    for r in range(PEER_TOPK):
        m = jnp.max(x, axis=0, keepdims=True)
        cand = jnp.where(x == m, iota, float(nk))
        sel = cand == jnp.min(cand, axis=0, keepdims=True)
        x = jnp.where(sel, -jnp.inf, x)
        rank = jnp.where(sel, float(r), rank)
        vals.append(m)
    return vals, rank


def _peer_topk_kernel(st_ref, rank2_ref, c2_ref, cnt1_ref, c1_ref):
    n = st_ref.shape[1]
    rid = lax.broadcasted_iota(jnp.int32, (_CAND_ROWS, n), 0)
    flat = jnp.zeros((_CAND_ROWS, n), jnp.int32) + PEER_TOPK * PEER_TOPK
    irow = jnp.zeros((_CAND_ROWS, n), jnp.int32) + PEER_TOPK
    for ci, (i, j) in enumerate(_CAND):
        flat = jnp.where(rid == ci, i * PEER_TOPK + j, flat)
        irow = jnp.where(rid == ci, i, irow)
    for h in range(PEER_HEADS):
        s1 = st_ref[(2 * h) * N_KEYS:(2 * h + 1) * N_KEYS, :]
        s2 = st_ref[(2 * h + 1) * N_KEYS:(2 * h + 2) * N_KEYS, :]
        v1, rank1 = _top16(s1)
        v2, rank2 = _top16(s2)
        rows = [v1[i] + v2[j] for (i, j) in _CAND]
        rows += [jnp.full((1, n), -jnp.inf, F32)] * (_CAND_ROWS - len(_CAND))
        cand = jnp.concatenate(rows, axis=0)
        top = v1[0] + v2[0]
        chosen = jnp.zeros(cand.shape, jnp.bool_)
        work = cand
        for _ in range(PEER_TOPK):
            m = jnp.max(work, axis=0, keepdims=True)
            fl = jnp.where(work == m, flat, PEER_TOPK * PEER_TOPK + 1)
            sel = fl == jnp.min(fl, axis=0, keepdims=True)
            work = jnp.where(sel, -jnp.inf, work)
            chosen = chosen | sel
        z = jnp.sum(jnp.where(chosen, jnp.exp(cand - top), 0.0), axis=0, keepdims=True)
        cnt1 = jnp.zeros(s1.shape, F32)
        for i in range(PEER_TOPK):
            m_i = jnp.sum(jnp.where(chosen & (irow == i), 1.0, 0.0), axis=0, keepdims=True)
            cnt1 = jnp.where(rank1 == float(i), m_i, cnt1)
        cnt1_ref[h] = cnt1
        c1_ref[h] = jnp.where(rank1 < float(PEER_TOPK), jnp.exp(s1 - v1[0]) / z, 0.0)
        c2_ref[h] = jnp.where(rank2 < float(PEER_TOPK), jnp.exp(s2 - v2[0]), 0.0).astype(BF16)
        rank2_ref[h] = rank2.astype(BF16)


def _peer_topk(st, tn):
    ntok = st.shape[1]
    assert ntok % tn == 0
    spec = pl.BlockSpec((PEER_HEADS, N_KEYS, tn), lambda i: (0, 0, i))
    return pl.pallas_call(
        _peer_topk_kernel,
        grid=(ntok // tn,),
        in_specs=[pl.BlockSpec((st.shape[0], tn), lambda i: (0, i))],
        out_specs=[spec] * 4,
        out_shape=[jax.ShapeDtypeStruct((PEER_HEADS, N_KEYS, ntok), dt) for dt in (BF16, BF16, F32, F32)],
        compiler_params=_cparams(("parallel",)),
        name="peer_topk",
    )(st)


def _peer_dense_kernel(x_ref, u_ref, vt_ref, rank2_ref, c2_ref, cnt1_ref, c1_ref, h2_ref, y_ref,
                       acc_ref, hid_ref, p_ref, *, e1b):
    j = pl.program_id(1)
    tn = x_ref.shape[0]
    ngrp = N_KEYS // BF16_ROWS

    @pl.when(j == 0)
    def _():
        acc_ref[...] = jnp.zeros_like(acc_ref)

    hid_ref[...] = lax.dot_general(u_ref[...], x_ref[...], (((1,), (1,)), ((), ())),
                                   preferred_element_type=F32)
    for e in range(e1b):
        e1 = j * e1b + e
        gate = [None] * ngrp
        for h in range(PEER_HEADS):
            cnt = jnp.broadcast_to(cnt1_ref[h, pl.ds(e1, 1), :], (BF16_ROWS, tn)).astype(BF16)
            c1 = jnp.broadcast_to(c1_ref[h, pl.ds(e1, 1), :], (BF16_ROWS, tn)).astype(BF16)
            for g in range(ngrp):
                rows = slice(g * BF16_ROWS, (g + 1) * BF16_ROWS)
                c2 = c2_ref[h, rows, :]
                term = jnp.where(rank2_ref[h, rows, :] < cnt, c2, jnp.zeros_like(c2)) * c1
                gate[g] = term if gate[g] is None else gate[g] + term
        hh = hid_ref[e * N_KEYS:(e + 1) * N_KEYS, :]
        gelu = (0.5 * hh * (1.0 + lax.erf(hh * (2.0 ** -0.5)))).astype(BF16)
        for g in range(ngrp):
            r0 = e * N_KEYS + g * BF16_ROWS
            p_ref[r0:r0 + BF16_ROWS, :] = gelu[g * BF16_ROWS:(g + 1) * BF16_ROWS] * gate[g]
    acc_ref[...] += jnp.dot(vt_ref[...], p_ref[...], preferred_element_type=F32)

    @pl.when(j == pl.num_programs(1) - 1)
    def _():
        y_ref[...] = h2_ref[...] + acc_ref[...].T


def _peer_dense(xn, u_bf, vt_bf, rank2, c2, cnt1, c1, h2, tn, e1b):
    ntok = xn.shape[0]
    assert ntok % tn == 0 and N_KEYS % e1b == 0
    tok = pl.BlockSpec((PEER_HEADS, N_KEYS, tn), lambda i, j: (0, 0, i))
    row = pl.BlockSpec((tn, D_MODEL), lambda i, j: (i, 0))
    return pl.pallas_call(
        functools.partial(_peer_dense_kernel, e1b=e1b),
        grid=(ntok // tn, N_KEYS // e1b),
        in_specs=[row,
                  pl.BlockSpec((e1b * N_KEYS, D_MODEL), lambda i, j: (j, 0)),
                  pl.BlockSpec((D_MODEL, e1b * N_KEYS), lambda i, j: (0, j)),
                  tok, tok, tok, tok, row],
        out_specs=row,
        out_shape=jax.ShapeDtypeStruct((ntok, D_MODEL), F32),
        scratch_shapes=[pltpu.VMEM((D_MODEL, tn), F32), pltpu.VMEM((e1b * N_KEYS, tn), F32),
                        pltpu.VMEM((e1b * N_KEYS, tn), BF16)],
        compiler_params=_cparams(("parallel", "arbitrary")),
        name="peer_dense",
    )(xn, u_bf, vt_bf, rank2, c2, cnt1, c1, h2)


def _state_in(s):
    b = s.shape[0]
    return jnp.transpose(s, (0, 2, 1, 3)).reshape(b, RW_HEAD, RW_DIM)


def _state_out(s):
    b = s.shape[0]
    return jnp.transpose(s.reshape(b, RW_HEAD, RW_HEADS, RW_HEAD), (0, 2, 1, 3))


def _pad_tokens(a, n, axis):
    pad = [(0, 0)] * a.ndim
    pad[axis] = (0, n - a.shape[axis])
    return jnp.pad(a, pad)


def kernel(x_prompt, x_sample, cache_k, cache_v, state_wkv, state_shift, page_table, meta, norm1_g, w_in,
           rw_mu, rw_w0, rw_w2, rw_a0, rw_a2, rw_g2, rw_kk, rw_ka, rw_rk, rw_ln_g, rw_ln_b, w_rw_br,
           da_qn_g, da_kn_g, da_lam, da_subln_g, w_da_br, w_out, norm2_g, peer_wq, peer_subkeys,
           peer_u, peer_v):
    b, seq, _ = x_prompt.shape
    bd = x_sample.shape[0]
    t_real = N_META + seq
    tp = PAD_LEAD + t_real
    assert tp % ROW_TILE == 0 and tp % SCAN_CHUNK == 0 and x_sample.shape[1] == 1
    row2 = lambda a: a.reshape(1, -1)

    w_in_bf = w_in[0].astype(BF16)
    g1 = row2(norm1_g[0])
    qg = row2(jnp.tile(da_qn_g[0], DA_QK // DA_DK))
    kg = row2(jnp.tile(da_kn_g[0], DA_QK // DA_DK))
    zlo = jnp.zeros((W_LORA, RW_DIM), F32)
    prep_params = (row2(rw_mu[0]), row2(rw_w0[0]),
                   jnp.concatenate([rw_w2[0], zlo], axis=0).astype(BF16), row2(rw_a0[0]),
                   jnp.concatenate([zlo, rw_a2[0]], axis=0).astype(BF16), rw_g2[0].astype(BF16),
                   row2(rw_kk[0]), row2(rw_ka[0]), row2(rw_rk[0]))
    merge_consts = (row2(rw_ln_g[0]), row2(rw_ln_b[0]), w_rw_br[0].astype(BF16), w_da_br[0].astype(BF16),
                    w_out[0].astype(BF16), row2(norm2_g[0]), peer_wq[0].astype(BF16),
                    peer_subkeys[0].astype(BF16))
    subln = row2(da_subln_g[0])
    u_bf = peer_u[0].astype(BF16)
    vt_bf = peer_v[0].astype(BF16).T

    hp = jnp.concatenate([jnp.zeros((b, PAD_LEAD, D_MODEL), F32),
                          jnp.broadcast_to(meta[None], (b, N_META, D_MODEL)), x_prompt], axis=1)
    prw, q, k, v, gates = _in_proj(hp.reshape(b * tp, D_MODEL), g1, w_in_bf, qg, kg, 256)
    r3 = lambda a: a.reshape(b, tp, -1)
    pr, pw, pk, pv, pa, pb, pg, pbonus = _rwkv_prep_prompt(r3(prw), prep_params, 384)
    y_scan, s_fin = _rwkv_scan(pr, pw, pk, pv, pa, pb, jnp.zeros((b, RW_HEAD, RW_DIM), F32))
    o_attn = _attn_prompt(r3(q), r3(k), r3(v), da_lam[0], subln, 384)
    h2, xn2, st = _merge(y_scan, pbonus, pg, o_attn, r3(gates), hp, merge_consts, ROW_TILE, 1)
    y_prompt = _peer_dense(xn2, u_bf, vt_bf, *_peer_topk(st, 128), h2, 512, 8)

    xs = x_sample.reshape(bd, D_MODEL)
    prw_s, q_s, k_s, v_s, gates_s = _in_proj(xs, g1, w_in_bf, qg, kg, bd)
    sr, sw, sk, sv, sa, sb, sg, sbonus = _rwkv_prep_sample(prw_s, state_shift[0], prep_params)
    ys_scan, ss_fin = _rwkv_step(sr, sw, sk, sv, sa, sb, _state_in(state_wkv[0]))
    n_phys = cache_k.shape[1]
    ckt = jnp.transpose(cache_k[0].reshape(n_phys, -1, DA_QK), (0, 2, 1))
    cv = cache_v[0].reshape(n_phys, -1, DA_DV)
    o_s = _attn_sample(q_s, k_s, v_s, ckt, cv, page_table, da_lam[0], subln, 8)
    f3 = lambda a: a.reshape(1, bd, -1)
    h2s, xn2s, sts = _merge(ys_scan.reshape(1, bd, RW_DIM), f3(sbonus), f3(sg), f3(o_s), f3(gates_s),
                            f3(xs), merge_consts, bd, 0)
    npad = LANES
    tk_s = _peer_topk(_pad_tokens(sts, npad, 1), npad)
    y_s = _peer_dense(_pad_tokens(xn2s, npad, 0), u_bf, vt_bf, *tk_s, _pad_tokens(h2s, npad, 0), npad, 8)

    y_prompt = y_prompt.reshape(b, seq, D_MODEL)
    y_sample = y_s[:bd].reshape(bd, 1, D_MODEL)
    k_p = r3(k)[:, PAD_LEAD:].reshape(1, b, t_real, DA_HEADS, 2, DA_DK)
    v_p = r3(v)[:, PAD_LEAD:].reshape(1, b, t_real, DA_HEADS, DA_DV)
    return (y_prompt, y_sample, k_p, v_p,
            k_s.reshape(1, bd, 1, DA_HEADS, 2, DA_DK), v_s.reshape(1, bd, 1, DA_HEADS, DA_DV),
            _state_out(s_fin)[None], _state_out(ss_fin)[None],
            r3(prw)[:, -1][None], prw_s[None])
```

```python
import functools
import math

import jax
import jax.numpy as jnp
from jax import lax
from jax.experimental import pallas as pl
from jax.experimental.pallas import tpu as pltpu

F32 = jnp.float32
BF16 = jnp.bfloat16

D_MODEL = 1024
N_META = 16
NORM_EPS = 1e-6
RW_HEAD = 64
RW_DIM = D_MODEL // 2
RW_HEADS = RW_DIM // RW_HEAD
W_LORA = 64
A_LORA = 64
G_LORA = 128
RW_PROJ = 3 * RW_DIM + W_LORA + A_LORA + G_LORA
RW_LN_EPS = 64e-5
DA_DK = 64
DA_DV = 2 * DA_DK
DA_DIM = D_MODEL // 2
DA_HEADS = DA_DIM // DA_DV
DA_QK = DA_HEADS * 2 * DA_DK
DA_V = DA_HEADS * DA_DV
N_KEYS = 128
PEER_HEADS = 8
PEER_TOPK = 16
PEER_DQ = 256
PEER_DHALF = PEER_DQ // 2
LAM_INIT = 0.8 - 0.6 * math.exp(-0.3 * 0)

LANES = 128
BF16_ROWS = 16
ROW_TILE = 128
PAD_LEAD = ROW_TILE - N_META
SCAN_CHUNK = 64
NEG_BIG = -1e30
VMEM_LIMIT = 56 * 1024 * 1024

O1 = RW_PROJ
O2 = O1 + DA_QK
O3 = O2 + DA_QK
O4 = O3 + DA_V
O5 = O4 + D_MODEL
O6 = O5 + D_MODEL

_CAND = [(i, j) for i in range(PEER_TOPK) for j in range(PEER_TOPK) if (i + 1) * (j + 1) <= PEER_TOPK]
_CAND_ROWS = -(-len(_CAND) // 8) * 8


def _cparams(sem):
    return pltpu.CompilerParams(dimension_semantics=sem, vmem_limit_bytes=VMEM_LIMIT)


def _split_dot(x, j):
    hi = x.astype(BF16)
    lo = (x - hi.astype(F32)).astype(BF16)
    return (jnp.dot(hi, j, preferred_element_type=F32)
            + jnp.dot(lo, j, preferred_element_type=F32))


def _seg64(x, j):
    outs = [_split_dot(x[:, c * LANES:(c + 1) * LANES], j) for c in range(x.shape[1] // LANES)]
    return outs[0] if len(outs) == 1 else jnp.concatenate(outs, axis=1)


def _seg_ones():
    r = lax.broadcasted_iota(jnp.int32, (LANES, LANES), 0) // 64
    c = lax.broadcasted_iota(jnp.int32, (LANES, LANES), 1) // 64
    return (r == c).astype(BF16)


def _in_proj_kernel(x_ref, g1_ref, w_ref, qg_ref, kg_ref, prw_ref, q_ref, k_ref, v_ref, gt_ref):
    x = x_ref[...]
    ms = jnp.mean(x * x, axis=-1, keepdims=True)
    xn = (x * lax.rsqrt(ms + NORM_EPS) * g1_ref[...]).astype(BF16)

    def mm(a, b):
        return jnp.dot(xn, w_ref[:, a:b], preferred_element_type=F32)

    j = _seg_ones()

    def head_norm(t, g):
        msq = _seg64(t * t, j) * (1.0 / DA_DK)
        return t * lax.rsqrt(msq + NORM_EPS) * g

    prw_ref[...] = mm(0, O1)
    q_ref[...] = head_norm(mm(O1, O2), qg_ref[...]) * (DA_DK ** -0.5)
    k_ref[...] = head_norm(mm(O2, O3), kg_ref[...])
    v_ref[...] = mm(O3, O4)
    gt_ref[...] = jax.nn.sigmoid(mm(O4, O6))


def _in_proj(x, g1, w_bf, qg, kg, tm):
    n = x.shape[0]
    assert n % tm == 0
    row = lambda c: pl.BlockSpec((tm, c), lambda i: (i, 0))
    full = lambda a: pl.BlockSpec(a.shape, lambda i: (0,) * a.ndim)
    widths = (RW_PROJ, DA_QK, DA_QK, DA_V, 2 * D_MODEL)
    return pl.pallas_call(
        _in_proj_kernel,
        grid=(n // tm,),
        in_specs=[row(D_MODEL), full(g1), full(w_bf), full(qg), full(kg)],
        out_specs=[row(c) for c in widths],
        out_shape=[jax.ShapeDtypeStruct((n, c), F32) for c in widths],
        compiler_params=_cparams(("parallel",)),
        name="in_proj",
    )(x, g1, w_bf, qg, kg)


def _rwkv_prep_math(p, p_prev, mu, w0, w2p, a0, a2p, g2, kkp, kap, rk):
    ps = p + (p_prev - p) * mu
    r = ps[:, 0:RW_DIM]
    k = ps[:, RW_DIM:2 * RW_DIM]
    v = ps[:, 2 * RW_DIM:3 * RW_DIM]
    wa = ps[:, 3 * RW_DIM:3 * RW_DIM + W_LORA + A_LORA]
    gl = ps[:, 3 * RW_DIM + W_LORA + A_LORA:RW_PROJ]
    w = -jax.nn.softplus(-(w0 + jnp.dot(jnp.tanh(wa).astype(BF16), w2p, preferred_element_type=F32))) - 0.5
    decay = jnp.exp(-jnp.exp(w))
    a = jax.nn.sigmoid(a0 + jnp.dot(wa.astype(BF16), a2p, preferred_element_type=F32))
    g = jnp.dot(jax.nn.sigmoid(gl).astype(BF16), g2, preferred_element_type=F32)
    j = _seg_ones()
    kk = k * kkp
    kk = kk / jnp.maximum(jnp.sqrt(_seg64(kk * kk, j)), 1e-12)
    k2 = k * (1.0 + (a - 1.0) * kap)
    bonus = _seg64(r * k2 * rk, j) * v
    return r, decay, k2, v, -kk, kk * a, g, bonus


def _rwkv_prep_carry_kernel(p_ref, mu, w0, w2p, a0, a2p, g2, kkp, kap, rk, *rest):
    outs, carry = rest[:-1], rest[-1]

    @pl.when(pl.program_id(1) == 0)
    def _():
        carry[...] = jnp.zeros_like(carry)

    p = p_ref[...]
    rows = lax.broadcasted_iota(jnp.int32, p.shape, 0)
    p_prev = jnp.where(rows == 0, carry[...], pltpu.roll(p, 1, 0))
    carry[...] = p[p.shape[0] - 1:, :]
    res = _rwkv_prep_math(p, p_prev, mu[...], w0[...], w2p[...], a0[...], a2p[...], g2[...],
                          kkp[...], kap[...], rk[...])
    for o, val in zip(outs, res):
        o[...] = val


def _rwkv_prep_given_kernel(p_ref, pp_ref, mu, w0, w2p, a0, a2p, g2, kkp, kap, rk, *outs):
    res = _rwkv_prep_math(p_ref[...], pp_ref[...], mu[...], w0[...], w2p[...], a0[...], a2p[...],
                          g2[...], kkp[...], kap[...], rk[...])
    for o, val in zip(outs, res):
        o[...] = val


def _rwkv_prep_prompt(p3, params, tm):
    b, tp, _ = p3.shape
    assert tp % tm == 0
    full = lambda a: pl.BlockSpec(a.shape, lambda i, j: (0,) * a.ndim)
    return pl.pallas_call(
        _rwkv_prep_carry_kernel,
        grid=(b, tp // tm),
        in_specs=[pl.BlockSpec((None, tm, RW_PROJ), lambda i, j: (i, j, 0))] + [full(a) for a in params],
        out_specs=[pl.BlockSpec((None, tm, RW_DIM), lambda i, j: (i, j, 0))] * 8,
        out_shape=[jax.ShapeDtypeStruct((b, tp, RW_DIM), F32)] * 8,
        scratch_shapes=[pltpu.VMEM((1, RW_PROJ), F32)],
        compiler_params=_cparams(("arbitrary", "arbitrary")),
        name="rwkv_prep_prompt",
    )(p3, *params)


def _rwkv_prep_sample(p, p_prev, params):
    n = p.shape[0]
    full = lambda a: pl.BlockSpec(a.shape, lambda i: (0,) * a.ndim)
    return pl.pallas_call(
        _rwkv_prep_given_kernel,
        grid=(1,),
        in_specs=[full(p), full(p_prev)] + [full(a) for a in params],
        out_specs=[pl.BlockSpec((n, RW_DIM), lambda i: (0, 0))] * 8,
        out_shape=[jax.ShapeDtypeStruct((n, RW_DIM), F32)] * 8,
        compiler_params=_cparams(("arbitrary",)),
        name="rwkv_prep_sample",
    )(p, p_prev, *params)


def _scan_consts():
    lane = lax.broadcasted_iota(jnp.int32, (RW_HEAD, LANES), 1)
    sub = lax.broadcasted_iota(jnp.int32, (RW_HEAD, LANES), 0)
    first = lane < RW_HEAD
    eye_lo = (lane == sub).astype(F32)
    eye_hi = (lane - RW_HEAD == sub).astype(F32)
    return lane, first, eye_lo, eye_hi


def _seg(first, x_lo, x_hi):
    lo = jnp.sum(x_lo, axis=-1, keepdims=True)
    hi = jnp.sum(x_hi, axis=-1, keepdims=True)
    return jnp.where(first, lo, hi)


def _split_row(first, row):
    lo = jnp.where(first[0:1, :], row, 0.0)
    return lo, row - lo


def _scan_step_kernel(r_ref, w_ref, k_ref, v_ref, a_ref, b_ref, s0_ref, y_ref, s_ref, *, nb):
    _, first, eye_lo, eye_hi = _scan_consts()
    for b in range(nb):
        for p in range(RW_DIM // LANES):
            cols = pl.ds(p * LANES, LANES)
            r_t, w_t, k_t, v_t, a_t, b_t = [ref[pl.ds(b, 1), cols]
                                            for ref in (r_ref, w_ref, k_ref, v_ref, a_ref, b_ref)]
            s = s0_ref[b, :, cols]
            a_lo, a_hi = _split_row(first, a_t)
            r_lo, r_hi = _split_row(first, r_t)
            sa = _seg(first, s * a_lo, s * a_hi)
            vcol = _seg(first, eye_lo * v_t, eye_hi * v_t)
            s = s * w_t + sa * b_t + vcol * k_t
            s_ref[b, :, cols] = s
            ycol = _seg(first, s * r_lo, s * r_hi)
            y_ref[pl.ds(b, 1), cols] = jnp.sum((eye_lo + eye_hi) * ycol, axis=0, keepdims=True)


def _rwkv_step(r, w, k, v, a, b, s0):
    nb = r.shape[0]
    full = lambda x: pl.BlockSpec(x.shape, lambda i: (0,) * x.ndim)
    return pl.pallas_call(
        functools.partial(_scan_step_kernel, nb=nb),
        grid=(1,),
        in_specs=[full(r)] * 6 + [full(s0)],
        out_specs=[full(r), full(s0)],
        out_shape=[jax.ShapeDtypeStruct(r.shape, F32), jax.ShapeDtypeStruct(s0.shape, F32)],
        compiler_params=_cparams(("arbitrary",)),
        name="rwkv_step",
    )(r, w, k, v, a, b, s0)


def _scan_kernel(r_ref, w_ref, k_ref, v_ref, a_ref, b_ref, s0_ref, y_ref, sfin_ref,
                 s_scr, sa_scr, vc_scr, yacc_scr, vstage, zstage, yres, *, nb):
    npair = RW_DIM // LANES
    chains = [(b, p) for b in range(nb) for p in range(npair)]
    lane, first, eye_lo, eye_hi = _scan_consts()
    sub_t = 8

    @pl.when(pl.program_id(0) == 0)
    def _():
        for ci, (b, p) in enumerate(chains):
            s_scr[ci] = s0_ref[b, :, p * LANES:(p + 1) * LANES]

    yacc_scr[...] = jnp.zeros_like(yacc_scr)
    nch = len(chains)
    eye2 = eye_lo + eye_hi
    seg_ones = _seg_ones()
    zero_b = jnp.zeros((LANES, LANES), BF16)
    j_stack = jnp.concatenate([seg_ones, seg_ones], axis=0)
    j_pair = jnp.concatenate([jnp.concatenate([seg_ones, zero_b], axis=1),
                              jnp.concatenate([zero_b, seg_ones], axis=1)], axis=0)

    def prepare(ci, s, a_t):
        a_lo, a_hi = _split_row(first, a_t)
        sa_scr[ci] = _seg(first, s * a_lo, s * a_hi)

    def group(tg, carry):
        rows = pl.ds(pl.multiple_of(tg * sub_t, sub_t), sub_t)

        def row(ref, ci, i):
            b, p = chains[ci]
            return ref[b, rows, pl.ds(p * LANES, LANES)][i:i + 1, :]

        for i in range(sub_t):
            for ci in range(nch):
                v_t = row(v_ref, ci, i)
                v_hi = v_t.astype(BF16).astype(F32)
                lhs = jnp.concatenate([eye2 * v_hi, eye2 * (v_t - v_hi)], axis=1)
                vstage[pl.ds((i * nch + ci) * RW_HEAD, RW_HEAD), :] = lhs.astype(BF16)
        vc_scr[...] = jnp.dot(vstage[...], j_stack, preferred_element_type=F32).reshape(vc_scr.shape)

        for ci in range(nch):
            prepare(ci, s_scr[ci], row(a_ref, ci, 0))
        for i in range(sub_t):
            for ci in range(nch):
                s = s_scr[ci] * row(w_ref, ci, i) + sa_scr[ci] * row(b_ref, ci, i) \
                    + vc_scr[i * nch + ci] * row(k_ref, ci, i)
                s_scr[ci] = s
                if i + 1 < sub_t:
                    prepare(ci, s, row(a_ref, ci, i + 1))
                z = (s * row(r_ref, ci, i)).astype(BF16)
                zstage[pl.ds((i * (nch // 2) + ci // 2) * RW_HEAD, RW_HEAD),
                       pl.ds((ci % 2) * LANES, LANES)] = z
        yres[...] = jnp.dot(zstage[...], j_pair, preferred_element_type=F32)
        for i in range(sub_t):
            place = (lane % RW_HEAD) == tg * sub_t + i
            for ci in range(nch):
                ycol = yres[pl.ds((i * (nch // 2) + ci // 2) * RW_HEAD, RW_HEAD),
                            pl.ds((ci % 2) * LANES, LANES)]
                yacc_scr[ci] = jnp.where(place, ycol, yacc_scr[ci])
        return carry

    lax.fori_loop(0, SCAN_CHUNK // sub_t, group, 0)

    zeros = jnp.zeros((LANES - RW_HEAD, LANES), F32)
    for ci, (b, p) in enumerate(chains):
        cols = pl.ds(p * LANES, LANES)
        tr = jnp.concatenate([yacc_scr[ci], zeros], axis=0).T
        y_ref[b, :, cols] = jnp.where(first, tr[0:RW_HEAD], pltpu.roll(tr[RW_HEAD:], RW_HEAD, 1))
        sfin_ref[b, :, cols] = s_scr[ci]


def _rwkv_scan(r, w, k, v, a, b, s0):
    nb, tp, _ = r.shape
    assert tp % SCAN_CHUNK == 0 and SCAN_CHUNK == RW_HEAD
    nch = nb * (RW_DIM // LANES)
    seq = pl.BlockSpec((nb, SCAN_CHUNK, RW_DIM), lambda i: (0, i, 0))
    st = pl.BlockSpec((nb, RW_HEAD, RW_DIM), lambda i: (0, 0, 0))
    return pl.pallas_call(
        functools.partial(_scan_kernel, nb=nb),
        grid=(tp // SCAN_CHUNK,),
        in_specs=[seq] * 6 + [st],
        out_specs=[seq, st],
        out_shape=[jax.ShapeDtypeStruct((nb, tp, RW_DIM), F32),
                   jax.ShapeDtypeStruct((nb, RW_HEAD, RW_DIM), F32)],
        scratch_shapes=[pltpu.VMEM((nch, RW_HEAD, LANES), F32),
                        pltpu.VMEM((nch, RW_HEAD, LANES), F32),
                        pltpu.VMEM((8 * nch, RW_HEAD, LANES), F32),
                        pltpu.VMEM((nch, RW_HEAD, LANES), F32),
                        pltpu.VMEM((8 * nch * RW_HEAD, 2 * LANES), BF16),
                        pltpu.VMEM((4 * nch * RW_HEAD, 2 * LANES), BF16),
                        pltpu.VMEM((4 * nch * RW_HEAD, 2 * LANES), F32)],
        compiler_params=_cparams(("arbitrary",)),
        name="rwkv_scan",
    )(r, w, k, v, a, b, s0)


def _lambda_value(lam_ref):
    lv = lam_ref[...]
    s01 = jnp.sum(lv[0:1, :] * lv[1:2, :], axis=-1, keepdims=True)
    s23 = jnp.sum(lv[2:3, :] * lv[3:4, :], axis=-1, keepdims=True)
    return jnp.exp(s01) - jnp.exp(s23) + LAM_INIT


def _alibi_slope(h):
    slope = jnp.float32(2.0 ** (-8.0 * DA_HEADS / DA_HEADS))
    for i in range(DA_HEADS - 2, -1, -1):
        slope = jnp.where(h == i, jnp.float32(2.0 ** (-8.0 * (i + 1) / DA_HEADS)), slope)
    return slope


def _sub_norm(o, g):
    return o * lax.rsqrt(jnp.mean(o * o, axis=-1, keepdims=True) + NORM_EPS) * g * (1.0 - LAM_INIT)


ATT_ONES = 16


def _attn_prompt_kernel(q_ref, k_ref, v_ref, d0_ref, lam_ref, sg_ref, o_ref,
                        qc_ref, vt_ref, bias_ref, m_ref, acc_ref, *, blk):
    h = pl.program_id(1)
    qi = pl.program_id(2)
    slope = _alibi_slope(h)
    tp = k_ref.shape[0]
    sub = blk // LANES

    @pl.when(qi == 0)
    def _():
        for jb in range(tp // LANES):
            vt = v_ref[jb * LANES:(jb + 1) * LANES, :].T.astype(BF16)
            vt_ref[jb // sub, 0:DA_DV, (jb % sub) * LANES:(jb % sub + 1) * LANES] = vt
        vt_ref[:, DA_DV:, :] = jnp.ones((tp // blk, ATT_ONES, blk), BF16)

    lane = lax.broadcasted_iota(jnp.int32, (blk, LANES), 1)
    qb = q_ref[...].astype(BF16)
    for c in range(2):
        qc_ref[c] = jnp.where((lane // DA_DK) == c, qb, jnp.zeros_like(qb))
    bias_ref[...] = slope * d0_ref[...]
    m_ref[...] = jnp.full_like(m_ref, NEG_BIG)
    acc_ref[...] = jnp.zeros_like(acc_ref)

    def chunk(j, masked):
        kb = k_ref[pl.ds(pl.multiple_of(j * blk, blk), blk), :].astype(BF16)
        off = ((qi - j) * blk).astype(F32)
        c0 = slope * off
        if masked:
            kpos = j * blk + lax.broadcasted_iota(jnp.int32, (blk, blk), 0)
            valid = (d0_ref[...] + off >= 0.0) & (kpos >= PAD_LEAD)
        scores = [lax.dot_general(kb, qc_ref[c], (((1,), (1,)), ((), ())), preferred_element_type=F32)
                  for c in range(2)]
        for c in range(2):
            t = scores[c] - bias_ref[...]
            if masked:
                t = jnp.where(valid, t, NEG_BIG)
            m_old = m_ref[c]
            m_new = jnp.maximum(m_old, jnp.max(t, axis=0, keepdims=True) - c0)
            alpha = jnp.exp(m_old - m_new)
            p = jnp.exp(t - (m_new + c0)).astype(BF16)
            acc_ref[c] = alpha * acc_ref[c] + jnp.dot(vt_ref[j], p, preferred_element_type=F32)
            m_ref[c] = m_new

    chunk(0, True)
    lax.fori_loop(1, qi, lambda j, carry: (chunk(j, False), carry)[1], 0)

    @pl.when(qi > 0)
    def _():
        chunk(qi, True)

    lam = _lambda_value(lam_ref)
    a0 = acc_ref[0]
    a1 = acc_ref[1]
    ot = a0[:DA_DV] / a0[DA_DV:DA_DV + 1] - lam * (a1[:DA_DV] / a1[DA_DV:DA_DV + 1])
    o_ref[...] = _sub_norm(ot.T, sg_ref[...])


def _attn_prompt(q, k, v, da_lam, subln_g, blk):
    b, tp, _ = q.shape
    assert tp % blk == 0 and blk % LANES == 0 and PAD_LEAD <= blk
    d0t = (jnp.arange(blk, dtype=F32)[None, :] - jnp.arange(blk, dtype=F32)[:, None])
    qspec = pl.BlockSpec((None, blk, DA_DV), lambda bi, h, i: (bi, i, h))
    kspec = pl.BlockSpec((None, tp, DA_DV), lambda bi, h, i: (bi, 0, h))
    full = lambda a: pl.BlockSpec(a.shape, lambda bi, h, i: (0,) * a.ndim)
    return pl.pallas_call(
        functools.partial(_attn_prompt_kernel, blk=blk),
        grid=(b, DA_HEADS, tp // blk),
        in_specs=[qspec, kspec, kspec, full(d0t), full(da_lam), full(subln_g)],
        out_specs=qspec,
        scratch_shapes=[pltpu.VMEM((2, blk, DA_DV), BF16),
                        pltpu.VMEM((tp // blk, DA_DV + ATT_ONES, blk), BF16),
                        pltpu.VMEM((blk, blk), F32), pltpu.VMEM((2, 1, blk), F32),
                        pltpu.VMEM((2, DA_DV + ATT_ONES, blk), F32)],
        out_shape=jax.ShapeDtypeStruct((b, tp, DA_V), F32),
        compiler_params=_cparams(("parallel", "parallel", "arbitrary")),
        name="attn_prompt",
    )(q, k, v, d0t, da_lam, subln_g)


def _attn_sample_kernel(pt_ref, q_ref, kn_ref, vn_ref, lam_ref, sg_ref, *rest, ppg, page, past):
    k_refs = rest[:ppg]
    v_refs = rest[ppg:2 * ppg]
    o_ref, m_ref, l_ref, acc_ref = rest[2 * ppg:]
    g = pl.program_id(1)
    nrow = 2 * DA_HEADS
    rowi = lax.broadcasted_iota(jnp.int32, (nrow, DA_QK), 0)
    lanei = lax.broadcasted_iota(jnp.int32, (nrow, DA_QK), 1)
    qsel = (lanei // DA_DK) == rowi
    qmat = jnp.where(qsel, q_ref[...], 0.0)
    rh = lax.broadcasted_iota(jnp.int32, (nrow, 1), 0) // 2
    slope = jnp.full((nrow, 1), 2.0 ** (-8.0), F32)
    for i in range(DA_HEADS - 2, -1, -1):
        slope = jnp.where(rh == i, jnp.float32(2.0 ** (-8.0 * (i + 1) / DA_HEADS)), slope)

    @pl.when(g == 0)
    def _():
        m_ref[...] = jnp.full_like(m_ref, NEG_BIG)
        l_ref[...] = jnp.zeros_like(l_ref)
        acc_ref[...] = jnp.zeros_like(acc_ref)

    qb = qmat.astype(BF16)
    kpos = g * (ppg * page) + lax.broadcasted_iota(jnp.int32, (nrow, ppg * page), 1)
    s = jnp.concatenate([jnp.dot(qb, k_refs[i][...].astype(BF16), preferred_element_type=F32)
                         for i in range(ppg)], axis=1)
    s = s - slope * (past - kpos).astype(F32)
    m_old = m_ref[...]
    m_new = jnp.maximum(m_old, jnp.max(s, axis=-1, keepdims=True))
    alpha = jnp.exp(m_old - m_new)
    p = jnp.exp(s - m_new)
    l_ref[...] = alpha * l_ref[...] + jnp.sum(p, axis=-1, keepdims=True)
    pb = p.astype(BF16)
    pv = [jnp.zeros((nrow, DA_DV), F32)] * DA_HEADS
    for i in range(ppg):
        for hh in range(DA_HEADS):
            vh = v_refs[i][pl.ds(hh, page, stride=DA_HEADS), :].astype(BF16)
            pv[hh] = pv[hh] + jnp.dot(pb[:, i * page:(i + 1) * page], vh, preferred_element_type=F32)
    tot = pv[DA_HEADS - 1]
    for hh in range(DA_HEADS - 1):
        tot = jnp.where(rh == hh, pv[hh], tot)
    acc_ref[...] = alpha * acc_ref[...] + tot
    m_ref[...] = m_new

    @pl.when(g == pl.num_programs(1) - 1)
    def _():
        kn = kn_ref[...].astype(BF16).astype(F32)
        s_self = jnp.sum(qb.astype(F32) * kn, axis=-1, keepdims=True)
        m_old = m_ref[...]
        m_new = jnp.maximum(m_old, s_self)
        alpha = jnp.exp(m_old - m_new)
        p_self = jnp.exp(s_self - m_new)
        l_fin = alpha * l_ref[...] + p_self
        vn = vn_ref[...].astype(BF16).astype(F32)
        vrow = jnp.zeros((nrow, DA_DV), F32)
        for hh in range(DA_HEADS):
            vrow = jnp.where(rh == hh, vn[:, hh * DA_DV:(hh + 1) * DA_DV], vrow)
        acc = alpha * acc_ref[...] + p_self.astype(BF16).astype(F32) * vrow
        lam = _lambda_value(lam_ref)
        ri = lax.broadcasted_iota(jnp.int32, (nrow, 1), 0)
        t = acc / l_fin * jnp.where(ri % 2 == 0, 1.0, -lam)
        outs = []
        for hh in range(DA_HEADS):
            o = t[2 * hh:2 * hh + 1, :] + t[2 * hh + 1:2 * hh + 2, :]
            outs.append(_sub_norm(o, sg_ref[...]))
        o_ref[...] = jnp.concatenate(outs, axis=1)


def _attn_sample(q, kn, vn, cache_kt, cache_v, page_table, da_lam, subln_g, ppg):
    bd = q.shape[0]
    n_pages = page_table.shape[1]
    page = cache_kt.shape[2]
    assert n_pages % ppg == 0 and cache_v.shape[1:] == (page * DA_HEADS, DA_DV)
    pt = page_table.reshape(-1)
    one = pl.BlockSpec((None, 1, DA_QK), lambda b, g, pt: (b, 0, 0))
    full = lambda a: pl.BlockSpec(a.shape, lambda b, g, pt: (0,) * a.ndim)

    def pspec(i):
        return pl.BlockSpec((None, DA_QK, page),
                            lambda b, g, pt, i=i: (pt[b * n_pages + g * ppg + i], 0, 0))

    out = pl.pallas_call(
        functools.partial(_attn_sample_kernel, ppg=ppg, page=page, past=n_pages * page),
        grid_spec=pltpu.PrefetchScalarGridSpec(
            num_scalar_prefetch=1,
            grid=(bd, n_pages // ppg),
            in_specs=[one, one, one, full(da_lam), full(subln_g)]
                     + [pspec(i) for i in range(ppg)] * 2,
            out_specs=one,
            scratch_shapes=[pltpu.VMEM((2 * DA_HEADS, 1), F32), pltpu.VMEM((2 * DA_HEADS, 1), F32),
                            pltpu.VMEM((2 * DA_HEADS, DA_DV), F32)],
        ),
        out_shape=jax.ShapeDtypeStruct((bd, 1, DA_V), F32),
        compiler_params=_cparams(("parallel", "arbitrary")),
        name="attn_sample",
    )(pt, q.reshape(bd, 1, DA_QK), kn.reshape(bd, 1, DA_QK), vn.reshape(bd, 1, DA_V), da_lam, subln_g,
      *([cache_kt] * ppg), *([cache_v] * ppg))
    return out.reshape(bd, DA_V)


def _merge_kernel(y_ref, bonus_ref, g_ref, lng_ref, lnb_ref, o_ref, gt_ref, h_ref, wrw_ref, wda_ref,
                  wout_ref, n2_ref, wq_ref, sk_ref, h2_ref, xn_ref, st_ref):
    j = _seg_ones()
    y = y_ref[...]
    mu = _seg64(y, j) * (1.0 / RW_HEAD)
    d = y - mu
    var = _seg64(d * d, j) * (1.0 / RW_HEAD)
    yln = d * lax.rsqrt(var + RW_LN_EPS) * lng_ref[...] + lnb_ref[...]
    yr = ((yln + bonus_ref[...]) * g_ref[...]).astype(BF16)
    y_rw = jnp.dot(yr, wrw_ref[...], preferred_element_type=F32)
    y_da = jnp.dot(o_ref[...].astype(BF16), wda_ref[...], preferred_element_type=F32)
    gt = gt_ref[...]
    mix = gt[:, :D_MODEL] * y_rw + gt[:, D_MODEL:] * y_da
    h2 = h_ref[...] + jnp.dot(mix.astype(BF16), wout_ref[...], preferred_element_type=F32)
    h2_ref[...] = h2
    ms = jnp.mean(h2 * h2, axis=-1, keepdims=True)
    xn = (h2 * lax.rsqrt(ms + NORM_EPS) * n2_ref[...]).astype(BF16)
    xn_ref[...] = xn
    q = jnp.dot(xn, wq_ref[...], preferred_element_type=F32)
    for hc in range(2 * PEER_HEADS):
        qc = q[:, hc * PEER_DHALF:(hc + 1) * PEER_DHALF].astype(BF16)
        st_ref[hc * N_KEYS:(hc + 1) * N_KEYS, :] = lax.dot_general(
            sk_ref[hc % 2], qc, (((1,), (1,)), ((), ())), preferred_element_type=F32)


def _merge(y, bonus, g, o, gates, h, consts, tm, tile_off):
    b, rows, _ = y.shape
    assert rows % tm == 0
    nt = rows // tm - tile_off
    rin = lambda c: pl.BlockSpec((None, tm, c), lambda i, t: (i, t + tile_off, 0))
    full = lambda a: pl.BlockSpec(a.shape, lambda i, t: (0,) * a.ndim)
    ntok = b * nt * tm
    return pl.pallas_call(
        _merge_kernel,
        grid=(b, nt),
        in_specs=[rin(RW_DIM), rin(RW_DIM), rin(RW_DIM), full(consts[0]), full(consts[1]), rin(DA_V),
                  rin(2 * D_MODEL), rin(D_MODEL)] + [full(a) for a in consts[2:]],
        out_specs=[pl.BlockSpec((tm, D_MODEL), lambda i, t: (i * nt + t, 0)),
                   pl.BlockSpec((tm, D_MODEL), lambda i, t: (i * nt + t, 0)),
                   pl.BlockSpec((2 * PEER_HEADS * N_KEYS, tm), lambda i, t: (0, i * nt + t))],
        out_shape=[jax.ShapeDtypeStruct((ntok, D_MODEL), F32),
                   jax.ShapeDtypeStruct((ntok, D_MODEL), BF16),
                   jax.ShapeDtypeStruct((2 * PEER_HEADS * N_KEYS, ntok), F32)],
        compiler_params=_cparams(("parallel", "parallel")),
        name="merge",
    )(y, bonus, g, consts[0], consts[1], o, gates, h, *consts[2:])


def _count(mask):
    return jnp.sum(jnp.where(mask, 1.0, 0.0), axis=0, keepdims=True)


def _top16(x, exact):
    nk = x.shape[0]
    iota = lax.broadcasted_iota(jnp.int32, x.shape, 0).astype(F32)
    rank = jnp.full(x.shape, float(nk), F32)
    vals = []
    for r in range(PEER_TOPK):
        m = jnp.max(x, axis=0, keepdims=True)
        sel = x == m
        if exact:
            cand = jnp.where(sel, iota, float(nk))
            sel = cand == jnp.min(cand, axis=0, keepdims=True)
        x = jnp.where(sel, -jnp.inf, x)
        rank = jnp.where(sel, float(r), rank)
        vals.append(m)
    return vals, rank


def _peer_topk_kernel(st_ref, rank2_ref, c2_ref, cnt1_ref, c1_ref):
    n = st_ref.shape[1]
    rid = lax.broadcasted_iota(jnp.int32, (_CAND_ROWS, n), 0)
    flat = jnp.zeros((_CAND_ROWS, n), jnp.int32) + PEER_TOPK * PEER_TOPK
    irow = jnp.zeros((_CAND_ROWS, n), jnp.int32) + PEER_TOPK
    for ci, (i, j) in enumerate(_CAND):
        flat = jnp.where(rid == ci, i * PEER_TOPK + j, flat)
        irow = jnp.where(rid == ci, i, irow)
    outs = (rank2_ref, c2_ref, cnt1_ref, c1_ref)
    ties = _peer_topk_body(st_ref, outs, flat, irow, exact=False)

    @pl.when(jnp.max(ties) > 0.0)
    def _():
        _peer_topk_body(st_ref, outs, flat, irow, exact=True)


def _peer_topk_body(st_ref, outs, flat, irow, exact):
    rank2_ref, c2_ref, cnt1_ref, c1_ref = outs
    n = st_ref.shape[1]
    ties = jnp.zeros((1, n), F32)
    for h in range(PEER_HEADS):
        s1 = st_ref[(2 * h) * N_KEYS:(2 * h + 1) * N_KEYS, :]
        s2 = st_ref[(2 * h + 1) * N_KEYS:(2 * h + 2) * N_KEYS, :]
        v1, rank1 = _top16(s1, exact)
        v2, rank2 = _top16(s2, exact)
        rows = [v1[i] + v2[j] for (i, j) in _CAND]
        rows += [jnp.full((1, n), -jnp.inf, F32)] * (_CAND_ROWS - len(_CAND))
        cand = jnp.concatenate(rows, axis=0)
        top = v1[0] + v2[0]
        chosen = jnp.zeros(cand.shape, jnp.bool_)
        work = cand
        for _ in range(PEER_TOPK):
            m = jnp.max(work, axis=0, keepdims=True)
            sel = work == m
            if exact:
                fl = jnp.where(sel, flat, PEER_TOPK * PEER_TOPK + 1)
                sel = fl == jnp.min(fl, axis=0, keepdims=True)
            work = jnp.where(sel, -jnp.inf, work)
            chosen = chosen | sel
        if not exact:
            k = float(PEER_TOPK)
            bad = (_count(rank1 < k) != k) | (_count(rank2 < k) != k) | (_count(chosen) != k)
            ties = jnp.maximum(ties, jnp.where(bad, 1.0, 0.0))
        z = jnp.sum(jnp.where(chosen, jnp.exp(cand - top), 0.0), axis=0, keepdims=True)
        cnt1 = jnp.zeros(s1.shape, F32)
        for i in range(PEER_TOPK):
            m_i = jnp.sum(jnp.where(chosen & (irow == i), 1.0, 0.0), axis=0, keepdims=True)
            cnt1 = jnp.where(rank1 == float(i), m_i, cnt1)
        cnt1_ref[h] = cnt1
        c1_ref[h] = jnp.where(rank1 < float(PEER_TOPK), jnp.exp(s1 - v1[0]) / z, 0.0)
        c2_ref[h] = jnp.where(rank2 < float(PEER_TOPK), jnp.exp(s2 - v2[0]), 0.0).astype(BF16)
        rank2_ref[h] = rank2.astype(BF16)
    return ties


def _peer_topk(st, tn):
    ntok = st.shape[1]
    assert ntok % tn == 0
    spec = pl.BlockSpec((PEER_HEADS, N_KEYS, tn), lambda i: (0, 0, i))
    return pl.pallas_call(
        _peer_topk_kernel,
        grid=(ntok // tn,),
        in_specs=[pl.BlockSpec((st.shape[0], tn), lambda i: (0, i))],
        out_specs=[spec] * 4,
        out_shape=[jax.ShapeDtypeStruct((PEER_HEADS, N_KEYS, ntok), dt) for dt in (BF16, BF16, F32, F32)],
        compiler_params=_cparams(("parallel",)),
        name="peer_topk",
    )(st)


def _peer_dense_kernel(x_ref, u_ref, vt_ref, rank2_ref, c2_ref, cnt1_ref, c1_ref, h2_ref, y_ref,
                       acc_ref, hid0_ref, hid1_ref, p_ref, *, e1b):
    j = pl.program_id(1)

    @pl.when(j == 0)
    def _():
        acc_ref[...] = jnp.zeros_like(acc_ref)
        hid1_ref[...] = jnp.zeros_like(hid1_ref)

    for parity, (hid_wr, hid_rd) in enumerate(((hid0_ref, hid1_ref), (hid1_ref, hid0_ref))):
        @pl.when(j % 2 == parity)
        def _():
            _peer_dense_step(jnp.maximum(j - 1, 0), e1b, x_ref, u_ref, vt_ref, rank2_ref, c2_ref, cnt1_ref,
                             c1_ref, acc_ref, hid_wr, hid_rd, p_ref)

    @pl.when(j == pl.num_programs(1) - 1)
    def _():
        y_ref[...] = h2_ref[...] + acc_ref[...].T


def _peer_dense_step(jg, e1b, x_ref, u_ref, vt_ref, rank2_ref, c2_ref, cnt1_ref, c1_ref,
                     acc_ref, hid_wr, hid_ref, p_ref):
    tn = x_ref.shape[0]
    ngrp = N_KEYS // BF16_ROWS
    for e in range(e1b):
        e1 = jg * e1b + e
        gate = [None] * ngrp
        for h in range(PEER_HEADS):
            cnt = jnp.broadcast_to(cnt1_ref[h, pl.ds(e1, 1), :], (BF16_ROWS, tn)).astype(BF16)
            c1 = jnp.broadcast_to(c1_ref[h, pl.ds(e1, 1), :], (BF16_ROWS, tn)).astype(BF16)
            for g in range(ngrp):
                rows = slice(g * BF16_ROWS, (g + 1) * BF16_ROWS)
                c2 = c2_ref[h, rows, :]
                term = jnp.where(rank2_ref[h, rows, :] < cnt, c2, jnp.zeros_like(c2)) * c1
                gate[g] = term if gate[g] is None else gate[g] + term
        hh = hid_ref[e * N_KEYS:(e + 1) * N_KEYS, :]
        gelu = (0.5 * hh * (1.0 + lax.erf(hh * (2.0 ** -0.5)))).astype(BF16)
        for g in range(ngrp):
            r0 = e * N_KEYS + g * BF16_ROWS
            p_ref[r0:r0 + BF16_ROWS, :] = gelu[g * BF16_ROWS:(g + 1) * BF16_ROWS] * gate[g]
    hid_wr[...] = lax.dot_general(u_ref[...], x_ref[...], (((1,), (1,)), ((), ())),
                                  preferred_element_type=F32)
    acc_ref[...] += jnp.dot(vt_ref[...], p_ref[...], preferred_element_type=F32)


def _peer_dense(xn, u_bf, vt_bf, rank2, c2, cnt1, c1, h2, tn, e1b):
    ntok = xn.shape[0]
    assert ntok % tn == 0 and N_KEYS % e1b == 0
    tok = pl.BlockSpec((PEER_HEADS, N_KEYS, tn), lambda i, j: (0, 0, i))
    row = pl.BlockSpec((tn, D_MODEL), lambda i, j: (i, 0))
    nt = N_KEYS // e1b
    return pl.pallas_call(
        functools.partial(_peer_dense_kernel, e1b=e1b),
        grid=(ntok // tn, nt + 1),
        in_specs=[row,
                  pl.BlockSpec((e1b * N_KEYS, D_MODEL), lambda i, j: (jnp.minimum(j, nt - 1), 0)),
                  pl.BlockSpec((D_MODEL, e1b * N_KEYS), lambda i, j: (0, jnp.maximum(j - 1, 0))),
                  tok, tok, tok, tok, row],
        out_specs=row,
        out_shape=jax.ShapeDtypeStruct((ntok, D_MODEL), F32),
        scratch_shapes=[pltpu.VMEM((D_MODEL, tn), F32), pltpu.VMEM((e1b * N_KEYS, tn), F32),
                        pltpu.VMEM((e1b * N_KEYS, tn), F32), pltpu.VMEM((e1b * N_KEYS, tn), BF16)],
        compiler_params=_cparams(("parallel", "arbitrary")),
        name="peer_dense",
    )(xn, u_bf, vt_bf, rank2, c2, cnt1, c1, h2)


def _state_in(s):
    b = s.shape[0]
    return jnp.transpose(s, (0, 2, 1, 3)).reshape(b, RW_HEAD, RW_DIM)


def _state_out(s):
    b = s.shape[0]
    return jnp.transpose(s.reshape(b, RW_HEAD, RW_HEADS, RW_HEAD), (0, 2, 1, 3))


def _pad_tokens(a, n, axis):
    pad = [(0, 0)] * a.ndim
    pad[axis] = (0, n - a.shape[axis])
    return jnp.pad(a, pad)


def kernel(x_prompt, x_sample, cache_k, cache_v, state_wkv, state_shift, page_table, meta, norm1_g, w_in,
           rw_mu, rw_w0, rw_w2, rw_a0, rw_a2, rw_g2, rw_kk, rw_ka, rw_rk, rw_ln_g, rw_ln_b, w_rw_br,
           da_qn_g, da_kn_g, da_lam, da_subln_g, w_da_br, w_out, norm2_g, peer_wq, peer_subkeys,
           peer_u, peer_v):
    b, seq, _ = x_prompt.shape
    bd = x_sample.shape[0]
    t_real = N_META + seq
    tp = PAD_LEAD + t_real
    assert tp % ROW_TILE == 0 and tp % SCAN_CHUNK == 0 and x_sample.shape[1] == 1
    row2 = lambda a: a.reshape(1, -1)

    w_in_bf = w_in[0].astype(BF16)
    g1 = row2(norm1_g[0])
    qg = row2(jnp.tile(da_qn_g[0], DA_QK // DA_DK))
    kg = row2(jnp.tile(da_kn_g[0], DA_QK // DA_DK))
    zlo = jnp.zeros((W_LORA, RW_DIM), F32)
    prep_params = (row2(rw_mu[0]), row2(rw_w0[0]),
                   jnp.concatenate([rw_w2[0], zlo], axis=0).astype(BF16), row2(rw_a0[0]),
                   jnp.concatenate([zlo, rw_a2[0]], axis=0).astype(BF16), rw_g2[0].astype(BF16),
                   row2(rw_kk[0]), row2(rw_ka[0]), row2(rw_rk[0]))
    merge_consts = (row2(rw_ln_g[0]), row2(rw_ln_b[0]), w_rw_br[0].astype(BF16), w_da_br[0].astype(BF16),
                    w_out[0].astype(BF16), row2(norm2_g[0]), peer_wq[0].astype(BF16),
                    peer_subkeys[0].astype(BF16))
    subln = row2(da_subln_g[0])
    u_bf = peer_u[0].astype(BF16)
    vt_bf = peer_v[0].astype(BF16).T

    hp = jnp.concatenate([jnp.zeros((b, PAD_LEAD, D_MODEL), F32),
                          jnp.broadcast_to(meta[None], (b, N_META, D_MODEL)), x_prompt], axis=1)
    prw, q, k, v, gates = _in_proj(hp.reshape(b * tp, D_MODEL), g1, w_in_bf, qg, kg, 256)
    r3 = lambda a: a.reshape(b, tp, -1)
    pr, pw, pk, pv, pa, pb, pg, pbonus = _rwkv_prep_prompt(r3(prw), prep_params, 384)
    y_scan, s_fin = _rwkv_scan(pr, pw, pk, pv, pa, pb, jnp.zeros((b, RW_HEAD, RW_DIM), F32))
    o_attn = _attn_prompt(r3(q), r3(k), r3(v), da_lam[0], subln, 384)
    h2, xn2, st = _merge(y_scan, pbonus, pg, o_attn, r3(gates), hp, merge_consts, ROW_TILE, 1)
    y_prompt = _peer_dense(xn2, u_bf, vt_bf, *_peer_topk(st, 128), h2, 512, 8)

    xs = x_sample.reshape(bd, D_MODEL)
    prw_s, q_s, k_s, v_s, gates_s = _in_proj(xs, g1, w_in_bf, qg, kg, bd)
    sr, sw, sk, sv, sa, sb, sg, sbonus = _rwkv_prep_sample(prw_s, state_shift[0], prep_params)
    ys_scan, ss_fin = _rwkv_step(sr, sw, sk, sv, sa, sb, _state_in(state_wkv[0]))
    n_phys = cache_k.shape[1]
    ckt = jnp.transpose(cache_k[0].reshape(n_phys, -1, DA_QK), (0, 2, 1))
    cv = cache_v[0].reshape(n_phys, -1, DA_DV)
    o_s = _attn_sample(q_s, k_s, v_s, ckt, cv, page_table, da_lam[0], subln, 8)
    f3 = lambda a: a.reshape(1, bd, -1)
    h2s, xn2s, sts = _merge(ys_scan.reshape(1, bd, RW_DIM), f3(sbonus), f3(sg), f3(o_s), f3(gates_s),
                            f3(xs), merge_consts, bd, 0)
    npad = LANES
    tk_s = _peer_topk(_pad_tokens(sts, npad, 1), npad)
    y_s = _peer_dense(_pad_tokens(xn2s, npad, 0), u_bf, vt_bf, *tk_s, _pad_tokens(h2s, npad, 0), npad, 8)

    y_prompt = y_prompt.reshape(b, seq, D_MODEL)
    y_sample = y_s[:bd].reshape(bd, 1, D_MODEL)
    k_p = r3(k)[:, PAD_LEAD:].reshape(1, b, t_real, DA_HEADS, 2, DA_DK)
    v_p = r3(v)[:, PAD_LEAD:].reshape(1, b, t_real, DA_HEADS, DA_DV)
    return (y_prompt, y_sample, k_p, v_p,
            k_s.reshape(1, bd, 1, DA_HEADS, 2, DA_DK), v_s.reshape(1, bd, 1, DA_HEADS, DA_DV),
            _state_out(s_fin)[None], _state_out(ss_fin)[None],
            r3(prw)[:, -1][None], prw_s[None])
```

```python
import functools
import math

import jax
import jax.numpy as jnp
from jax import lax
from jax.experimental import pallas as pl
from jax.experimental.pallas import tpu as pltpu

F32 = jnp.float32
BF16 = jnp.bfloat16

D_MODEL = 1024
N_META = 16
NORM_EPS = 1e-6
RW_HEAD = 64
RW_DIM = D_MODEL // 2
RW_HEADS = RW_DIM // RW_HEAD
W_LORA = 64
A_LORA = 64
G_LORA = 128
RW_PROJ = 3 * RW_DIM + W_LORA + A_LORA + G_LORA
RW_LN_EPS = 64e-5
DA_DK = 64
DA_DV = 2 * DA_DK
DA_DIM = D_MODEL // 2
DA_HEADS = DA_DIM // DA_DV
DA_QK = DA_HEADS * 2 * DA_DK
DA_V = DA_HEADS * DA_DV
N_KEYS = 128
PEER_HEADS = 8
PEER_TOPK = 16
PEER_DQ = 256
PEER_DHALF = PEER_DQ // 2
LAM_INIT = 0.8 - 0.6 * math.exp(-0.3 * 0)

LANES = 128
BF16_ROWS = 16
ROW_TILE = 128
PAD_LEAD = ROW_TILE - N_META
SCAN_CHUNK = 64
NEG_BIG = -1e30
VMEM_LIMIT = 56 * 1024 * 1024

O1 = RW_PROJ
O2 = O1 + DA_QK
O3 = O2 + DA_QK
O4 = O3 + DA_V
O5 = O4 + D_MODEL
O6 = O5 + D_MODEL

_CAND = [(i, j) for i in range(PEER_TOPK) for j in range(PEER_TOPK) if (i + 1) * (j + 1) <= PEER_TOPK]
_CAND_ROWS = -(-len(_CAND) // 8) * 8


def _cparams(sem):
    return pltpu.CompilerParams(dimension_semantics=sem, vmem_limit_bytes=VMEM_LIMIT)


def _split_dot(x, j):
    hi = x.astype(BF16)
    lo = (x - hi.astype(F32)).astype(BF16)
    return (jnp.dot(hi, j, preferred_element_type=F32)
            + jnp.dot(lo, j, preferred_element_type=F32))


def _seg64(x, j):
    outs = [_split_dot(x[:, c * LANES:(c + 1) * LANES], j) for c in range(x.shape[1] // LANES)]
    return outs[0] if len(outs) == 1 else jnp.concatenate(outs, axis=1)


def _seg_ones():
    r = lax.broadcasted_iota(jnp.int32, (LANES, LANES), 0) // 64
    c = lax.broadcasted_iota(jnp.int32, (LANES, LANES), 1) // 64
    return (r == c).astype(BF16)


def _in_proj_kernel(x_ref, g1_ref, w_ref, qg_ref, kg_ref, prw_ref, q_ref, k_ref, v_ref, gt_ref):
    x = x_ref[...]
    ms = jnp.mean(x * x, axis=-1, keepdims=True)
    xn = (x * lax.rsqrt(ms + NORM_EPS) * g1_ref[...]).astype(BF16)

    def mm(a, b):
        return jnp.dot(xn, w_ref[:, a:b], preferred_element_type=F32)

    j = _seg_ones()

    def head_norm(t, g):
        msq = _seg64(t * t, j) * (1.0 / DA_DK)
        return t * lax.rsqrt(msq + NORM_EPS) * g

    prw_ref[...] = mm(0, O1)
    q_ref[...] = head_norm(mm(O1, O2), qg_ref[...]) * (DA_DK ** -0.5)
    k_ref[...] = head_norm(mm(O2, O3), kg_ref[...])
    v_ref[...] = mm(O3, O4)
    gt_ref[...] = jax.nn.sigmoid(mm(O4, O6))


def _in_proj(x, g1, w_bf, qg, kg, tm):
    n = x.shape[0]
    assert n % tm == 0
    row = lambda c: pl.BlockSpec((tm, c), lambda i: (i, 0))
    full = lambda a: pl.BlockSpec(a.shape, lambda i: (0,) * a.ndim)
    widths = (RW_PROJ, DA_QK, DA_QK, DA_V, 2 * D_MODEL)
    return pl.pallas_call(
        _in_proj_kernel,
        grid=(n // tm,),
        in_specs=[row(D_MODEL), full(g1), full(w_bf), full(qg), full(kg)],
        out_specs=[row(c) for c in widths],
        out_shape=[jax.ShapeDtypeStruct((n, c), F32) for c in widths],
        compiler_params=_cparams(("parallel",)),
        name="in_proj",
    )(x, g1, w_bf, qg, kg)


def _rwkv_prep_math(p, p_prev, mu, w0, w2p, a0, a2p, g2, kkp, kap, rk):
    ps = p + (p_prev - p) * mu
    r = ps[:, 0:RW_DIM]
    k = ps[:, RW_DIM:2 * RW_DIM]
    v = ps[:, 2 * RW_DIM:3 * RW_DIM]
    wa = ps[:, 3 * RW_DIM:3 * RW_DIM + W_LORA + A_LORA]
    gl = ps[:, 3 * RW_DIM + W_LORA + A_LORA:RW_PROJ]
    w = -jax.nn.softplus(-(w0 + jnp.dot(jnp.tanh(wa).astype(BF16), w2p, preferred_element_type=F32))) - 0.5
    decay = jnp.exp(-jnp.exp(w))
    a = jax.nn.sigmoid(a0 + jnp.dot(wa.astype(BF16), a2p, preferred_element_type=F32))
    g = jnp.dot(jax.nn.sigmoid(gl).astype(BF16), g2, preferred_element_type=F32)
    j = _seg_ones()
    kk = k * kkp
    kk = kk / jnp.maximum(jnp.sqrt(_seg64(kk * kk, j)), 1e-12)
    k2 = k * (1.0 + (a - 1.0) * kap)
    bonus = _seg64(r * k2 * rk, j) * v
    return r, decay, k2, v, -kk, kk * a, g, bonus


def _rwkv_prep_carry_kernel(p_ref, mu, w0, w2p, a0, a2p, g2, kkp, kap, rk, *rest):
    outs, carry = rest[:-1], rest[-1]

    @pl.when(pl.program_id(1) == 0)
    def _():
        carry[...] = jnp.zeros_like(carry)

    p = p_ref[...]
    rows = lax.broadcasted_iota(jnp.int32, p.shape, 0)
    p_prev = jnp.where(rows == 0, carry[...], pltpu.roll(p, 1, 0))
    carry[...] = p[p.shape[0] - 1:, :]
    res = _rwkv_prep_math(p, p_prev, mu[...], w0[...], w2p[...], a0[...], a2p[...], g2[...],
                          kkp[...], kap[...], rk[...])
    for o, val in zip(outs, res):
        o[...] = val


def _rwkv_prep_given_kernel(p_ref, pp_ref, mu, w0, w2p, a0, a2p, g2, kkp, kap, rk, *outs):
    res = _rwkv_prep_math(p_ref[...], pp_ref[...], mu[...], w0[...], w2p[...], a0[...], a2p[...],
                          g2[...], kkp[...], kap[...], rk[...])
    for o, val in zip(outs, res):
        o[...] = val


def _rwkv_prep_prompt(p3, params, tm):
    b, tp, _ = p3.shape
    assert tp % tm == 0
    full = lambda a: pl.BlockSpec(a.shape, lambda i, j: (0,) * a.ndim)
    return pl.pallas_call(
        _rwkv_prep_carry_kernel,
        grid=(b, tp // tm),
        in_specs=[pl.BlockSpec((None, tm, RW_PROJ), lambda i, j: (i, j, 0))] + [full(a) for a in params],
        out_specs=[pl.BlockSpec((None, tm, RW_DIM), lambda i, j: (i, j, 0))] * 8,
        out_shape=[jax.ShapeDtypeStruct((b, tp, RW_DIM), F32)] * 8,
        scratch_shapes=[pltpu.VMEM((1, RW_PROJ), F32)],
        compiler_params=_cparams(("arbitrary", "arbitrary")),
        name="rwkv_prep_prompt",
    )(p3, *params)


def _rwkv_prep_sample(p, p_prev, params):
    n = p.shape[0]
    full = lambda a: pl.BlockSpec(a.shape, lambda i: (0,) * a.ndim)
    return pl.pallas_call(
        _rwkv_prep_given_kernel,
        grid=(1,),
        in_specs=[full(p), full(p_prev)] + [full(a) for a in params],
        out_specs=[pl.BlockSpec((n, RW_DIM), lambda i: (0, 0))] * 8,
        out_shape=[jax.ShapeDtypeStruct((n, RW_DIM), F32)] * 8,
        compiler_params=_cparams(("arbitrary",)),
        name="rwkv_prep_sample",
    )(p, p_prev, *params)


def _scan_consts():
    lane = lax.broadcasted_iota(jnp.int32, (RW_HEAD, LANES), 1)
    sub = lax.broadcasted_iota(jnp.int32, (RW_HEAD, LANES), 0)
    first = lane < RW_HEAD
    eye_lo = (lane == sub).astype(F32)
    eye_hi = (lane - RW_HEAD == sub).astype(F32)
    return lane, first, eye_lo, eye_hi


def _seg(first, x_lo, x_hi):
    lo = jnp.sum(x_lo, axis=-1, keepdims=True)
    hi = jnp.sum(x_hi, axis=-1, keepdims=True)
    return jnp.where(first, lo, hi)


def _split_row(first, row):
    lo = jnp.where(first[0:1, :], row, 0.0)
    return lo, row - lo


def _scan_step_kernel(r_ref, w_ref, k_ref, v_ref, a_ref, b_ref, s0_ref, y_ref, s_ref, *, nb):
    _, first, eye_lo, eye_hi = _scan_consts()
    for b in range(nb):
        for p in range(RW_DIM // LANES):
            cols = pl.ds(p * LANES, LANES)
            r_t, w_t, k_t, v_t, a_t, b_t = [ref[pl.ds(b, 1), cols]
                                            for ref in (r_ref, w_ref, k_ref, v_ref, a_ref, b_ref)]
            s = s0_ref[b, :, cols]
            a_lo, a_hi = _split_row(first, a_t)
            r_lo, r_hi = _split_row(first, r_t)
            sa = _seg(first, s * a_lo, s * a_hi)
            vcol = _seg(first, eye_lo * v_t, eye_hi * v_t)
            s = s * w_t + sa * b_t + vcol * k_t
            s_ref[b, :, cols] = s
            ycol = _seg(first, s * r_lo, s * r_hi)
            y_ref[pl.ds(b, 1), cols] = jnp.sum((eye_lo + eye_hi) * ycol, axis=0, keepdims=True)


def _rwkv_step(r, w, k, v, a, b, s0):
    nb = r.shape[0]
    full = lambda x: pl.BlockSpec(x.shape, lambda i: (0,) * x.ndim)
    return pl.pallas_call(
        functools.partial(_scan_step_kernel, nb=nb),
        grid=(1,),
        in_specs=[full(r)] * 6 + [full(s0)],
        out_specs=[full(r), full(s0)],
        out_shape=[jax.ShapeDtypeStruct(r.shape, F32), jax.ShapeDtypeStruct(s0.shape, F32)],
        compiler_params=_cparams(("arbitrary",)),
        name="rwkv_step",
    )(r, w, k, v, a, b, s0)


def _scan_kernel(r_ref, w_ref, k_ref, v_ref, a_ref, b_ref, s0_ref, y_ref, sfin_ref,
                 s_scr, sa_scr, vc_scr, yacc_scr, vstage, zstage, yres, *, nb):
    npair = RW_DIM // LANES
    chains = [(b, p) for b in range(nb) for p in range(npair)]
    lane, first, eye_lo, eye_hi = _scan_consts()
    sub_t = 8

    @pl.when(pl.program_id(0) == 0)
    def _():
        for ci, (b, p) in enumerate(chains):
            s_scr[ci] = s0_ref[b, :, p * LANES:(p + 1) * LANES]

    yacc_scr[...] = jnp.zeros_like(yacc_scr)
    nch = len(chains)
    eye2 = eye_lo + eye_hi
    seg_ones = _seg_ones()
    zero_b = jnp.zeros((LANES, LANES), BF16)
    j_stack = jnp.concatenate([seg_ones, seg_ones], axis=0)
    j_pair = jnp.concatenate([jnp.concatenate([seg_ones, zero_b], axis=1),
                              jnp.concatenate([zero_b, seg_ones], axis=1)], axis=0)

    def prepare(ci, s, a_t):
        a_lo, a_hi = _split_row(first, a_t)
        sa_scr[ci] = _seg(first, s * a_lo, s * a_hi)

    def group(tg, carry):
        rows = pl.ds(pl.multiple_of(tg * sub_t, sub_t), sub_t)

        def row(ref, ci, i):
            b, p = chains[ci]
            return ref[b, rows, pl.ds(p * LANES, LANES)][i:i + 1, :]

        for i in range(sub_t):
            for ci in range(nch):
                v_t = row(v_ref, ci, i)
                v_hi = v_t.astype(BF16).astype(F32)
                lhs = jnp.concatenate([eye2 * v_hi, eye2 * (v_t - v_hi)], axis=1)
                vstage[pl.ds((i * nch + ci) * RW_HEAD, RW_HEAD), :] = lhs.astype(BF16)
        vc_scr[...] = jnp.dot(vstage[...], j_stack, preferred_element_type=F32).reshape(vc_scr.shape)

        for ci in range(nch):
            prepare(ci, s_scr[ci], row(a_ref, ci, 0))
        for i in range(sub_t):
            for ci in range(nch):
                s = s_scr[ci] * row(w_ref, ci, i) + sa_scr[ci] * row(b_ref, ci, i) \
                    + vc_scr[i * nch + ci] * row(k_ref, ci, i)
                s_scr[ci] = s
                if i + 1 < sub_t:
                    prepare(ci, s, row(a_ref, ci, i + 1))
                z = (s * row(r_ref, ci, i)).astype(BF16)
                zstage[pl.ds((i * (nch // 2) + ci // 2) * RW_HEAD, RW_HEAD),
                       pl.ds((ci % 2) * LANES, LANES)] = z
        yres[...] = jnp.dot(zstage[...], j_pair, preferred_element_type=F32)
        for i in range(sub_t):
            place = (lane % RW_HEAD) == tg * sub_t + i
            for ci in range(nch):
                ycol = yres[pl.ds((i * (nch // 2) + ci // 2) * RW_HEAD, RW_HEAD),
                            pl.ds((ci % 2) * LANES, LANES)]
                yacc_scr[ci] = jnp.where(place, ycol, yacc_scr[ci])
        return carry

    lax.fori_loop(0, SCAN_CHUNK // sub_t, group, 0)

    zeros = jnp.zeros((LANES - RW_HEAD, LANES), F32)
    for ci, (b, p) in enumerate(chains):
        cols = pl.ds(p * LANES, LANES)
        tr = jnp.concatenate([yacc_scr[ci], zeros], axis=0).T
        y_ref[b, :, cols] = jnp.where(first, tr[0:RW_HEAD], pltpu.roll(tr[RW_HEAD:], RW_HEAD, 1))
        sfin_ref[b, :, cols] = s_scr[ci]


def _rwkv_scan(r, w, k, v, a, b, s0):
    nb, tp, _ = r.shape
    assert tp % SCAN_CHUNK == 0 and SCAN_CHUNK == RW_HEAD
    nch = nb * (RW_DIM // LANES)
    seq = pl.BlockSpec((nb, SCAN_CHUNK, RW_DIM), lambda i: (0, i, 0))
    st = pl.BlockSpec((nb, RW_HEAD, RW_DIM), lambda i: (0, 0, 0))
    return pl.pallas_call(
        functools.partial(_scan_kernel, nb=nb),
        grid=(tp // SCAN_CHUNK,),
        in_specs=[seq] * 6 + [st],
        out_specs=[seq, st],
        out_shape=[jax.ShapeDtypeStruct((nb, tp, RW_DIM), F32),
                   jax.ShapeDtypeStruct((nb, RW_HEAD, RW_DIM), F32)],
        scratch_shapes=[pltpu.VMEM((nch, RW_HEAD, LANES), F32),
                        pltpu.VMEM((nch, RW_HEAD, LANES), F32),
                        pltpu.VMEM((8 * nch, RW_HEAD, LANES), F32),
                        pltpu.VMEM((nch, RW_HEAD, LANES), F32),
                        pltpu.VMEM((8 * nch * RW_HEAD, 2 * LANES), BF16),
                        pltpu.VMEM((4 * nch * RW_HEAD, 2 * LANES), BF16),
                        pltpu.VMEM((4 * nch * RW_HEAD, 2 * LANES), F32)],
        compiler_params=_cparams(("arbitrary",)),
        name="rwkv_scan",
    )(r, w, k, v, a, b, s0)


def _lambda_value(lam_ref):
    lv = lam_ref[...]
    s01 = jnp.sum(lv[0:1, :] * lv[1:2, :], axis=-1, keepdims=True)
    s23 = jnp.sum(lv[2:3, :] * lv[3:4, :], axis=-1, keepdims=True)
    return jnp.exp(s01) - jnp.exp(s23) + LAM_INIT


def _alibi_slope(h):
    slope = jnp.float32(2.0 ** (-8.0 * DA_HEADS / DA_HEADS))
    for i in range(DA_HEADS - 2, -1, -1):
        slope = jnp.where(h == i, jnp.float32(2.0 ** (-8.0 * (i + 1) / DA_HEADS)), slope)
    return slope


def _sub_norm(o, g):
    return o * lax.rsqrt(jnp.mean(o * o, axis=-1, keepdims=True) + NORM_EPS) * g * (1.0 - LAM_INIT)


ATT_ONES = 16


def _attn_prompt_kernel(q_ref, k_ref, v_ref, d0_ref, lam_ref, sg_ref, o_ref,
                        qc_ref, vt_ref, bias_ref, m_ref, acc_ref, s_ref, *, blk):
    h = pl.program_id(1)
    qi = pl.program_id(2)
    slope = _alibi_slope(h)
    tp = k_ref.shape[0]
    sub = blk // LANES

    @pl.when(qi == 0)
    def _():
        for jb in range(tp // LANES):
            vt = v_ref[jb * LANES:(jb + 1) * LANES, :].T.astype(BF16)
            vt_ref[jb // sub, 0:DA_DV, (jb % sub) * LANES:(jb % sub + 1) * LANES] = vt
        vt_ref[:, DA_DV:, :] = jnp.ones((tp // blk, ATT_ONES, blk), BF16)

    lane = lax.broadcasted_iota(jnp.int32, (blk, LANES), 1)
    qb = q_ref[...].astype(BF16)
    for c in range(2):
        qc_ref[c] = jnp.where((lane // DA_DK) == c, qb, jnp.zeros_like(qb))
    bias_ref[...] = slope * d0_ref[...]
    m_ref[...] = jnp.full_like(m_ref, NEG_BIG)
    acc_ref[...] = jnp.zeros_like(acc_ref)

    def scores(j):
        kb = k_ref[pl.ds(pl.multiple_of(j * blk, blk), blk), :].astype(BF16)
        for c in range(2):
            s_ref[(j % 2) * 2 + c] = lax.dot_general(kb, qc_ref[c], (((1,), (1,)), ((), ())),
                                                     preferred_element_type=F32)

    def consume(j, masked):
        off = ((qi - j) * blk).astype(F32)
        c0 = slope * off
        if masked:
            kpos = j * blk + lax.broadcasted_iota(jnp.int32, (blk, blk), 0)
            valid = (d0_ref[...] + off >= 0.0) & (kpos >= PAD_LEAD)
        for c in range(2):
            t = s_ref[(j % 2) * 2 + c] - bias_ref[...]
            if masked:
                t = jnp.where(valid, t, NEG_BIG)
            m_old = m_ref[c]
            m_new = jnp.maximum(m_old, jnp.max(t, axis=0, keepdims=True) - c0)
            alpha = jnp.exp(m_old - m_new)
            p = jnp.exp(t - (m_new + c0)).astype(BF16)
            acc_ref[c] = alpha * acc_ref[c] + jnp.dot(vt_ref[j], p, preferred_element_type=F32)
            m_ref[c] = m_new

    def middle(j, carry):
        consume(j, False)
        scores(j + 1)
        return carry

    scores(0)
    consume(0, True)
    scores(jnp.minimum(1, tp // blk - 1))
    lax.fori_loop(1, qi, middle, 0)

    @pl.when(qi > 0)
    def _():
        consume(qi, True)

    lam = _lambda_value(lam_ref)
    a0 = acc_ref[0]
    a1 = acc_ref[1]
    ot = a0[:DA_DV] / a0[DA_DV:DA_DV + 1] - lam * (a1[:DA_DV] / a1[DA_DV:DA_DV + 1])
    o_ref[...] = _sub_norm(ot.T, sg_ref[...])


def _attn_prompt(q, k, v, da_lam, subln_g, blk):
    b, tp, _ = q.shape
    assert tp % blk == 0 and blk % LANES == 0 and PAD_LEAD <= blk
    d0t = (jnp.arange(blk, dtype=F32)[None, :] - jnp.arange(blk, dtype=F32)[:, None])
    qspec = pl.BlockSpec((None, blk, DA_DV), lambda bi, h, i: (bi, i, h))
    kspec = pl.BlockSpec((None, tp, DA_DV), lambda bi, h, i: (bi, 0, h))
    full = lambda a: pl.BlockSpec(a.shape, lambda bi, h, i: (0,) * a.ndim)
    return pl.pallas_call(
        functools.partial(_attn_prompt_kernel, blk=blk),
        grid=(b, DA_HEADS, tp // blk),
        in_specs=[qspec, kspec, kspec, full(d0t), full(da_lam), full(subln_g)],
        out_specs=qspec,
        scratch_shapes=[pltpu.VMEM((2, blk, DA_DV), BF16),
                        pltpu.VMEM((tp // blk, DA_DV + ATT_ONES, blk), BF16),
                        pltpu.VMEM((blk, blk), F32), pltpu.VMEM((2, 1, blk), F32),
                        pltpu.VMEM((2, DA_DV + ATT_ONES, blk), F32),
                        pltpu.VMEM((4, blk, blk), F32)],
        out_shape=jax.ShapeDtypeStruct((b, tp, DA_V), F32),
        compiler_params=_cparams(("parallel", "parallel", "arbitrary")),
        name="attn_prompt",
    )(q, k, v, d0t, da_lam, subln_g)


def _attn_sample_kernel(pt_ref, q_ref, kn_ref, vn_ref, lam_ref, sg_ref, *rest, ppg, page, past):
    k_refs = rest[:ppg]
    v_refs = rest[ppg:2 * ppg]
    o_ref, m_ref, l_ref, acc_ref = rest[2 * ppg:]
    g = pl.program_id(1)
    nrow = 2 * DA_HEADS
    rowi = lax.broadcasted_iota(jnp.int32, (nrow, DA_QK), 0)
    lanei = lax.broadcasted_iota(jnp.int32, (nrow, DA_QK), 1)
    qsel = (lanei // DA_DK) == rowi
    qmat = jnp.where(qsel, q_ref[...], 0.0)
    rh = lax.broadcasted_iota(jnp.int32, (nrow, 1), 0) // 2
    slope = jnp.full((nrow, 1), 2.0 ** (-8.0), F32)
    for i in range(DA_HEADS - 2, -1, -1):
        slope = jnp.where(rh == i, jnp.float32(2.0 ** (-8.0 * (i + 1) / DA_HEADS)), slope)

    @pl.when(g == 0)
    def _():
        m_ref[...] = jnp.full_like(m_ref, NEG_BIG)
        l_ref[...] = jnp.zeros_like(l_ref)
        acc_ref[...] = jnp.zeros_like(acc_ref)

    qb = qmat.astype(BF16)
    kpos = g * (ppg * page) + lax.broadcasted_iota(jnp.int32, (nrow, ppg * page), 1)
    s = jnp.concatenate([jnp.dot(qb, k_refs[i][...].astype(BF16), preferred_element_type=F32)
                         for i in range(ppg)], axis=1)
    s = s - slope * (past - kpos).astype(F32)
    m_old = m_ref[...]
    m_new = jnp.maximum(m_old, jnp.max(s, axis=-1, keepdims=True))
    alpha = jnp.exp(m_old - m_new)
    p = jnp.exp(s - m_new)
    l_ref[...] = alpha * l_ref[...] + jnp.sum(p, axis=-1, keepdims=True)
    pb = p.astype(BF16)
    pv = [jnp.zeros((nrow, DA_DV), F32)] * DA_HEADS
    for i in range(ppg):
        for hh in range(DA_HEADS):
            vh = v_refs[i][pl.ds(hh, page, stride=DA_HEADS), :].astype(BF16)
            pv[hh] = pv[hh] + jnp.dot(pb[:, i * page:(i + 1) * page], vh, preferred_element_type=F32)
    tot = pv[DA_HEADS - 1]
    for hh in range(DA_HEADS - 1):
        tot = jnp.where(rh == hh, pv[hh], tot)
    acc_ref[...] = alpha * acc_ref[...] + tot
    m_ref[...] = m_new

    @pl.when(g == pl.num_programs(1) - 1)
    def _():
        kn = kn_ref[...].astype(BF16).astype(F32)
        s_self = jnp.sum(qb.astype(F32) * kn, axis=-1, keepdims=True)
        m_old = m_ref[...]
        m_new = jnp.maximum(m_old, s_self)
        alpha = jnp.exp(m_old - m_new)
        p_self = jnp.exp(s_self - m_new)
        l_fin = alpha * l_ref[...] + p_self
        vn = vn_ref[...].astype(BF16).astype(F32)
        vrow = jnp.zeros((nrow, DA_DV), F32)
        for hh in range(DA_HEADS):
            vrow = jnp.where(rh == hh, vn[:, hh * DA_DV:(hh + 1) * DA_DV], vrow)
        acc = alpha * acc_ref[...] + p_self.astype(BF16).astype(F32) * vrow
        lam = _lambda_value(lam_ref)
        ri = lax.broadcasted_iota(jnp.int32, (nrow, 1), 0)
        t = acc / l_fin * jnp.where(ri % 2 == 0, 1.0, -lam)
        outs = []
        for hh in range(DA_HEADS):
            o = t[2 * hh:2 * hh + 1, :] + t[2 * hh + 1:2 * hh + 2, :]
            outs.append(_sub_norm(o, sg_ref[...]))
        o_ref[...] = jnp.concatenate(outs, axis=1)


def _attn_sample(q, kn, vn, cache_kt, cache_v, page_table, da_lam, subln_g, ppg):
    bd = q.shape[0]
    n_pages = page_table.shape[1]
    page = cache_kt.shape[2]
    assert n_pages % ppg == 0 and cache_v.shape[1:] == (page * DA_HEADS, DA_DV)
    pt = page_table.reshape(-1)
    one = pl.BlockSpec((None, 1, DA_QK), lambda b, g, pt: (b, 0, 0))
    full = lambda a: pl.BlockSpec(a.shape, lambda b, g, pt: (0,) * a.ndim)

    def pspec(i):
        return pl.BlockSpec((None, DA_QK, page),
                            lambda b, g, pt, i=i: (pt[b * n_pages + g * ppg + i], 0, 0))

    out = pl.pallas_call(
        functools.partial(_attn_sample_kernel, ppg=ppg, page=page, past=n_pages * page),
        grid_spec=pltpu.PrefetchScalarGridSpec(
            num_scalar_prefetch=1,
            grid=(bd, n_pages // ppg),
            in_specs=[one, one, one, full(da_lam), full(subln_g)]
                     + [pspec(i) for i in range(ppg)] * 2,
            out_specs=one,
            scratch_shapes=[pltpu.VMEM((2 * DA_HEADS, 1), F32), pltpu.VMEM((2 * DA_HEADS, 1), F32),
                            pltpu.VMEM((2 * DA_HEADS, DA_DV), F32)],
        ),
        out_shape=jax.ShapeDtypeStruct((bd, 1, DA_V), F32),
        compiler_params=_cparams(("parallel", "arbitrary")),
        name="attn_sample",
    )(pt, q.reshape(bd, 1, DA_QK), kn.reshape(bd, 1, DA_QK), vn.reshape(bd, 1, DA_V), da_lam, subln_g,
      *([cache_kt] * ppg), *([cache_v] * ppg))
    return out.reshape(bd, DA_V)


def _merge_kernel(y_ref, bonus_ref, g_ref, lng_ref, lnb_ref, o_ref, gt_ref, h_ref, wrw_ref, wda_ref,
                  wout_ref, n2_ref, wq_ref, sk_ref, h2_ref, xn_ref, st_ref):
    j = _seg_ones()
    y = y_ref[...]
    mu = _seg64(y, j) * (1.0 / RW_HEAD)
    d = y - mu
    var = _seg64(d * d, j) * (1.0 / RW_HEAD)
    yln = d * lax.rsqrt(var + RW_LN_EPS) * lng_ref[...] + lnb_ref[...]
    yr = ((yln + bonus_ref[...]) * g_ref[...]).astype(BF16)
    y_rw = jnp.dot(yr, wrw_ref[...], preferred_element_type=F32)
    y_da = jnp.dot(o_ref[...].astype(BF16), wda_ref[...], preferred_element_type=F32)
    gt = gt_ref[...]
    mix = gt[:, :D_MODEL] * y_rw + gt[:, D_MODEL:] * y_da
    h2 = h_ref[...] + jnp.dot(mix.astype(BF16), wout_ref[...], preferred_element_type=F32)
    h2_ref[...] = h2
    ms = jnp.mean(h2 * h2, axis=-1, keepdims=True)
    xn = (h2 * lax.rsqrt(ms + NORM_EPS) * n2_ref[...]).astype(BF16)
    xn_ref[...] = xn
    q = jnp.dot(xn, wq_ref[...], preferred_element_type=F32)
    for hc in range(2 * PEER_HEADS):
        qc = q[:, hc * PEER_DHALF:(hc + 1) * PEER_DHALF].astype(BF16)
        st_ref[hc * N_KEYS:(hc + 1) * N_KEYS, :] = lax.dot_general(
            sk_ref[hc % 2], qc, (((1,), (1,)), ((), ())), preferred_element_type=F32)


def _merge(y, bonus, g, o, gates, h, consts, tm, tile_off):
    b, rows, _ = y.shape
    assert rows % tm == 0
    nt = rows // tm - tile_off
    rin = lambda c: pl.BlockSpec((None, tm, c), lambda i, t: (i, t + tile_off, 0))
    full = lambda a: pl.BlockSpec(a.shape, lambda i, t: (0,) * a.ndim)
    ntok = b * nt * tm
    return pl.pallas_call(
        _merge_kernel,
        grid=(b, nt),
        in_specs=[rin(RW_DIM), rin(RW_DIM), rin(RW_DIM), full(consts[0]), full(consts[1]), rin(DA_V),
                  rin(2 * D_MODEL), rin(D_MODEL)] + [full(a) for a in consts[2:]],
        out_specs=[pl.BlockSpec((tm, D_MODEL), lambda i, t: (i * nt + t, 0)),
                   pl.BlockSpec((tm, D_MODEL), lambda i, t: (i * nt + t, 0)),
                   pl.BlockSpec((2 * PEER_HEADS * N_KEYS, tm), lambda i, t: (0, i * nt + t))],
        out_shape=[jax.ShapeDtypeStruct((ntok, D_MODEL), F32),
                   jax.ShapeDtypeStruct((ntok, D_MODEL), BF16),
                   jax.ShapeDtypeStruct((2 * PEER_HEADS * N_KEYS, ntok), F32)],
        compiler_params=_cparams(("parallel", "parallel")),
        name="merge",
    )(y, bonus, g, consts[0], consts[1], o, gates, h, *consts[2:])


def _count(mask):
    return jnp.sum(jnp.where(mask, 1.0, 0.0), axis=0, keepdims=True)


def _top16(x, exact):
    nk = x.shape[0]
    iota = lax.broadcasted_iota(jnp.int32, x.shape, 0).astype(F32)
    rank = jnp.full(x.shape, float(nk), F32)
    vals = []
    for r in range(PEER_TOPK):
        m = jnp.max(x, axis=0, keepdims=True)
        sel = x == m
        if exact:
            cand = jnp.where(sel, iota, float(nk))
            sel = cand == jnp.min(cand, axis=0, keepdims=True)
        x = jnp.where(sel, -jnp.inf, x)
        rank = jnp.where(sel, float(r), rank)
        vals.append(m)
    return vals, rank


def _peer_topk_kernel(st_ref, rank2_ref, c2_ref, cnt1_ref, c1_ref):
    n = st_ref.shape[1]
    rid = lax.broadcasted_iota(jnp.int32, (_CAND_ROWS, n), 0)
    flat = jnp.zeros((_CAND_ROWS, n), jnp.int32) + PEER_TOPK * PEER_TOPK
    irow = jnp.zeros((_CAND_ROWS, n), jnp.int32) + PEER_TOPK
    for ci, (i, j) in enumerate(_CAND):
        flat = jnp.where(rid == ci, i * PEER_TOPK + j, flat)
        irow = jnp.where(rid == ci, i, irow)
    outs = (rank2_ref, c2_ref, cnt1_ref, c1_ref)
    ties = _peer_topk_body(st_ref, outs, flat, irow, exact=False)

    @pl.when(jnp.max(ties) > 0.0)
    def _():
        _peer_topk_body(st_ref, outs, flat, irow, exact=True)


def _peer_topk_body(st_ref, outs, flat, irow, exact):
    rank2_ref, c2_ref, cnt1_ref, c1_ref = outs
    n = st_ref.shape[1]
    ties = jnp.zeros((1, n), F32)
    for h in range(PEER_HEADS):
        s1 = st_ref[(2 * h) * N_KEYS:(2 * h + 1) * N_KEYS, :]
        s2 = st_ref[(2 * h + 1) * N_KEYS:(2 * h + 2) * N_KEYS, :]
        v1, rank1 = _top16(s1, exact)
        v2, rank2 = _top16(s2, exact)
        rows = [v1[i] + v2[j] for (i, j) in _CAND]
        rows += [jnp.full((1, n), -jnp.inf, F32)] * (_CAND_ROWS - len(_CAND))
        cand = jnp.concatenate(rows, axis=0)
        top = v1[0] + v2[0]
        chosen = jnp.zeros(cand.shape, jnp.bool_)
        work = cand
        for _ in range(PEER_TOPK):
            m = jnp.max(work, axis=0, keepdims=True)
            sel = work == m
            if exact:
                fl = jnp.where(sel, flat, PEER_TOPK * PEER_TOPK + 1)
                sel = fl == jnp.min(fl, axis=0, keepdims=True)
            work = jnp.where(sel, -jnp.inf, work)
            chosen = chosen | sel
        if not exact:
            k = float(PEER_TOPK)
            bad = (_count(rank1 < k) != k) | (_count(rank2 < k) != k) | (_count(chosen) != k)
            ties = jnp.maximum(ties, jnp.where(bad, 1.0, 0.0))
        z = jnp.sum(jnp.where(chosen, jnp.exp(cand - top), 0.0), axis=0, keepdims=True)
        cnt1 = jnp.zeros(s1.shape, F32)
        for i in range(PEER_TOPK):
            m_i = jnp.sum(jnp.where(chosen & (irow == i), 1.0, 0.0), axis=0, keepdims=True)
            cnt1 = jnp.where(rank1 == float(i), m_i, cnt1)
        cnt1_ref[h] = cnt1
        c1_ref[h] = jnp.where(rank1 < float(PEER_TOPK), jnp.exp(s1 - v1[0]) / z, 0.0)
        c2_ref[h] = jnp.where(rank2 < float(PEER_TOPK), jnp.exp(s2 - v2[0]), 0.0).astype(BF16)
        rank2_ref[h] = rank2.astype(BF16)
    return ties


def _peer_topk(st, tn):
    ntok = st.shape[1]
    assert ntok % tn == 0
    spec = pl.BlockSpec((PEER_HEADS, N_KEYS, tn), lambda i: (0, 0, i))
    return pl.pallas_call(
        _peer_topk_kernel,
        grid=(ntok // tn,),
        in_specs=[pl.BlockSpec((st.shape[0], tn), lambda i: (0, i))],
        out_specs=[spec] * 4,
        out_shape=[jax.ShapeDtypeStruct((PEER_HEADS, N_KEYS, ntok), dt) for dt in (BF16, BF16, F32, F32)],
        compiler_params=_cparams(("parallel",)),
        name="peer_topk",
    )(st)


def _peer_dense_kernel(x_ref, u_ref, vt_ref, rank2_ref, c2_ref, cnt1_ref, c1_ref, h2_ref, y_ref,
                       acc_ref, hid_ref, p_ref, *, e1b):
    j = pl.program_id(1)
    tn = x_ref.shape[0]
    ngrp = N_KEYS // BF16_ROWS

    @pl.when(j == 0)
    def _():
        acc_ref[...] = jnp.zeros_like(acc_ref)

    hid_ref[...] = lax.dot_general(u_ref[...], x_ref[...], (((1,), (1,)), ((), ())),
                                   preferred_element_type=F32)

    for e in range(e1b):
        e1 = j * e1b + e
        gate = [None] * ngrp
        for h in range(PEER_HEADS):
            cnt = jnp.broadcast_to(cnt1_ref[h, pl.ds(e1, 1), :], (BF16_ROWS, tn)).astype(BF16)
            c1 = jnp.broadcast_to(c1_ref[h, pl.ds(e1, 1), :], (BF16_ROWS, tn)).astype(BF16)
            for g in range(ngrp):
                rows = slice(g * BF16_ROWS, (g + 1) * BF16_ROWS)
                c2 = c2_ref[h, rows, :]
                term = jnp.where(rank2_ref[h, rows, :] < cnt, c2, jnp.zeros_like(c2)) * c1
                gate[g] = term if gate[g] is None else gate[g] + term
        hh = hid_ref[e * N_KEYS:(e + 1) * N_KEYS, :]
        gelu = (0.5 * hh * (1.0 + lax.erf(hh * (2.0 ** -0.5)))).astype(BF16)
        for g in range(ngrp):
            r0 = e * N_KEYS + g * BF16_ROWS
            p_ref[r0:r0 + BF16_ROWS, :] = gelu[g * BF16_ROWS:(g + 1) * BF16_ROWS] * gate[g]
    acc_ref[...] += jnp.dot(vt_ref[...], p_ref[...], preferred_element_type=F32)

    @pl.when(j == pl.num_programs(1) - 1)
    def _():
        y_ref[...] = h2_ref[...] + acc_ref[...].T


def _peer_dense(xn, u_bf, vt_bf, rank2, c2, cnt1, c1, h2, tn, e1b):
    ntok = xn.shape[0]
    assert ntok % tn == 0 and N_KEYS % e1b == 0
    tok = pl.BlockSpec((PEER_HEADS, N_KEYS, tn), lambda i, j: (0, 0, i))
    row = pl.BlockSpec((tn, D_MODEL), lambda i, j: (i, 0))
    return pl.pallas_call(
        functools.partial(_peer_dense_kernel, e1b=e1b),
        grid=(ntok // tn, N_KEYS // e1b),
        in_specs=[row,
                  pl.BlockSpec((e1b * N_KEYS, D_MODEL), lambda i, j: (j, 0)),
                  pl.BlockSpec((D_MODEL, e1b * N_KEYS), lambda i, j: (0, j)),
                  tok, tok, tok, tok, row],
        out_specs=row,
        out_shape=jax.ShapeDtypeStruct((ntok, D_MODEL), F32),
        scratch_shapes=[pltpu.VMEM((D_MODEL, tn), F32), pltpu.VMEM((e1b * N_KEYS, tn), F32),
                        pltpu.VMEM((e1b * N_KEYS, tn), BF16)],
        compiler_params=_cparams(("parallel", "arbitrary")),
        name="peer_dense",
    )(xn, u_bf, vt_bf, rank2, c2, cnt1, c1, h2)


def _state_in(s):
    b = s.shape[0]
    return jnp.transpose(s, (0, 2, 1, 3)).reshape(b, RW_HEAD, RW_DIM)


def _state_out(s):
    b = s.shape[0]
    return jnp.transpose(s.reshape(b, RW_HEAD, RW_HEADS, RW_HEAD), (0, 2, 1, 3))


def _pad_tokens(a, n, axis):
    pad = [(0, 0)] * a.ndim
    pad[axis] = (0, n - a.shape[axis])
    return jnp.pad(a, pad)


def kernel(x_prompt, x_sample, cache_k, cache_v, state_wkv, state_shift, page_table, meta, norm1_g, w_in,
           rw_mu, rw_w0, rw_w2, rw_a0, rw_a2, rw_g2, rw_kk, rw_ka, rw_rk, rw_ln_g, rw_ln_b, w_rw_br,
           da_qn_g, da_kn_g, da_lam, da_subln_g, w_da_br, w_out, norm2_g, peer_wq, peer_subkeys,
           peer_u, peer_v):
    b, seq, _ = x_prompt.shape
    bd = x_sample.shape[0]
    t_real = N_META + seq
    tp = PAD_LEAD + t_real
    assert tp % ROW_TILE == 0 and tp % SCAN_CHUNK == 0 and x_sample.shape[1] == 1
    row2 = lambda a: a.reshape(1, -1)

    w_in_bf = w_in[0].astype(BF16)
    g1 = row2(norm1_g[0])
    qg = row2(jnp.tile(da_qn_g[0], DA_QK // DA_DK))
    kg = row2(jnp.tile(da_kn_g[0], DA_QK // DA_DK))
    zlo = jnp.zeros((W_LORA, RW_DIM), F32)
    prep_params = (row2(rw_mu[0]), row2(rw_w0[0]),
                   jnp.concatenate([rw_w2[0], zlo], axis=0).astype(BF16), row2(rw_a0[0]),
                   jnp.concatenate([zlo, rw_a2[0]], axis=0).astype(BF16), rw_g2[0].astype(BF16),
                   row2(rw_kk[0]), row2(rw_ka[0]), row2(rw_rk[0]))
    merge_consts = (row2(rw_ln_g[0]), row2(rw_ln_b[0]), w_rw_br[0].astype(BF16), w_da_br[0].astype(BF16),
                    w_out[0].astype(BF16), row2(norm2_g[0]), peer_wq[0].astype(BF16),
                    peer_subkeys[0].astype(BF16))
    subln = row2(da_subln_g[0])
    u_bf = peer_u[0].astype(BF16)
    vt_bf = peer_v[0].astype(BF16).T

    hp = jnp.concatenate([jnp.zeros((b, PAD_LEAD, D_MODEL), F32),
                          jnp.broadcast_to(meta[None], (b, N_META, D_MODEL)), x_prompt], axis=1)
    prw, q, k, v, gates = _in_proj(hp.reshape(b * tp, D_MODEL), g1, w_in_bf, qg, kg, 256)
    r3 = lambda a: a.reshape(b, tp, -1)
    pr, pw, pk, pv, pa, pb, pg, pbonus = _rwkv_prep_prompt(r3(prw), prep_params, 384)
    y_scan, s_fin = _rwkv_scan(pr, pw, pk, pv, pa, pb, jnp.zeros((b, RW_HEAD, RW_DIM), F32))
    o_attn = _attn_prompt(r3(q), r3(k), r3(v), da_lam[0], subln, 384)
    h2, xn2, st = _merge(y_scan, pbonus, pg, o_attn, r3(gates), hp, merge_consts, ROW_TILE, 1)
    y_prompt = _peer_dense(xn2, u_bf, vt_bf, *_peer_topk(st, 128), h2, 512, 8)

    xs = x_sample.reshape(bd, D_MODEL)
    prw_s, q_s, k_s, v_s, gates_s = _in_proj(xs, g1, w_in_bf, qg, kg, bd)
    sr, sw, sk, sv, sa, sb, sg, sbonus = _rwkv_prep_sample(prw_s, state_shift[0], prep_params)
    ys_scan, ss_fin = _rwkv_step(sr, sw, sk, sv, sa, sb, _state_in(state_wkv[0]))
    n_phys = cache_k.shape[1]
    ckt = jnp.transpose(cache_k[0].reshape(n_phys, -1, DA_QK), (0, 2, 1))
    cv = cache_v[0].reshape(n_phys, -1, DA_DV)
    o_s = _attn_sample(q_s, k_s, v_s, ckt, cv, page_table, da_lam[0], subln, 8)
    f3 = lambda a: a.reshape(1, bd, -1)
    h2s, xn2s, sts = _merge(ys_scan.reshape(1, bd, RW_DIM), f3(sbonus), f3(sg), f3(o_s), f3(gates_s),
                            f3(xs), merge_consts, bd, 0)
    npad = LANES
    tk_s = _peer_topk(_pad_tokens(sts, npad, 1), npad)
    y_s = _peer_dense(_pad_tokens(xn2s, npad, 0), u_bf, vt_bf, *tk_s, _pad_tokens(h2s, npad, 0), npad, 8)

    y_prompt = y_prompt.reshape(b, seq, D_MODEL)
    y_sample = y_s[:bd].reshape(bd, 1, D_MODEL)
    k_p = r3(k)[:, PAD_LEAD:].reshape(1, b, t_real, DA_HEADS, 2, DA_DK)
    v_p = r3(v)[:, PAD_LEAD:].reshape(1, b, t_real, DA_HEADS, DA_DV)
    return (y_prompt, y_sample, k_p, v_p,
            k_s.reshape(1, bd, 1, DA_HEADS, 2, DA_DK), v_s.reshape(1, bd, 1, DA_HEADS, DA_DV),
            _state_out(s_fin)[None], _state_out(ss_fin)[None],
            r3(prw)[:, -1][None], prw_s[None])
```

```python
import functools
import math

import jax
import jax.numpy as jnp
from jax import lax
from jax.experimental import pallas as pl
from jax.experimental.pallas import tpu as pltpu

F32 = jnp.float32
BF16 = jnp.bfloat16

D_MODEL = 1024
N_META = 16
NORM_EPS = 1e-6
RW_HEAD = 64
RW_DIM = D_MODEL // 2
RW_HEADS = RW_DIM // RW_HEAD
W_LORA = 64
A_LORA = 64
G_LORA = 128
RW_PROJ = 3 * RW_DIM + W_LORA + A_LORA + G_LORA
RW_LN_EPS = 64e-5
DA_DK = 64
DA_DV = 2 * DA_DK
DA_DIM = D_MODEL // 2
DA_HEADS = DA_DIM // DA_DV
DA_QK = DA_HEADS * 2 * DA_DK
DA_V = DA_HEADS * DA_DV
N_KEYS = 128
PEER_HEADS = 8
PEER_TOPK = 16
PEER_DQ = 256
PEER_DHALF = PEER_DQ // 2
LAM_INIT = 0.8 - 0.6 * math.exp(-0.3 * 0)

LANES = 128
BF16_ROWS = 16
ROW_TILE = 128
PAD_LEAD = ROW_TILE - N_META
SCAN_CHUNK = 64
NEG_BIG = -1e30
VMEM_LIMIT = 56 * 1024 * 1024

O1 = RW_PROJ
O2 = O1 + DA_QK
O3 = O2 + DA_QK
O4 = O3 + DA_V
O5 = O4 + D_MODEL
O6 = O5 + D_MODEL

_CAND = [(i, j) for i in range(PEER_TOPK) for j in range(PEER_TOPK) if (i + 1) * (j + 1) <= PEER_TOPK]
_CAND_ROWS = -(-len(_CAND) // 8) * 8


def _cparams(sem):
    return pltpu.CompilerParams(dimension_semantics=sem, vmem_limit_bytes=VMEM_LIMIT)


def _split_dot(x, j):
    hi = x.astype(BF16)
    lo = (x - hi.astype(F32)).astype(BF16)
    return (jnp.dot(hi, j, preferred_element_type=F32)
            + jnp.dot(lo, j, preferred_element_type=F32))


def _seg64(x, j):
    outs = [_split_dot(x[:, c * LANES:(c + 1) * LANES], j) for c in range(x.shape[1] // LANES)]
    return outs[0] if len(outs) == 1 else jnp.concatenate(outs, axis=1)


def _seg_ones():
    r = lax.broadcasted_iota(jnp.int32, (LANES, LANES), 0) // 64
    c = lax.broadcasted_iota(jnp.int32, (LANES, LANES), 1) // 64
    return (r == c).astype(BF16)


def _in_proj_kernel(x_ref, g1_ref, w_ref, qg_ref, kg_ref, prw_ref, q_ref, k_ref, v_ref, gt_ref):
    x = x_ref[...]
    ms = jnp.mean(x * x, axis=-1, keepdims=True)
    xn = (x * lax.rsqrt(ms + NORM_EPS) * g1_ref[...]).astype(BF16)

    def mm(a, b):
        return jnp.dot(xn, w_ref[:, a:b], preferred_element_type=F32)

    j = _seg_ones()

    def head_norm(t, g):
        msq = _seg64(t * t, j) * (1.0 / DA_DK)
        return t * lax.rsqrt(msq + NORM_EPS) * g

    prw_ref[...] = mm(0, O1)
    q_ref[...] = head_norm(mm(O1, O2), qg_ref[...]) * (DA_DK ** -0.5)
    k_ref[...] = head_norm(mm(O2, O3), kg_ref[...])
    v_ref[...] = mm(O3, O4)
    gt_ref[...] = jax.nn.sigmoid(mm(O4, O6))


def _in_proj(x, g1, w_bf, qg, kg, tm):
    n = x.shape[0]
    assert n % tm == 0
    row = lambda c: pl.BlockSpec((tm, c), lambda i: (i, 0))
    full = lambda a: pl.BlockSpec(a.shape, lambda i: (0,) * a.ndim)
    widths = (RW_PROJ, DA_QK, DA_QK, DA_V, 2 * D_MODEL)
    return pl.pallas_call(
        _in_proj_kernel,
        grid=(n // tm,),
        in_specs=[row(D_MODEL), full(g1), full(w_bf), full(qg), full(kg)],
        out_specs=[row(c) for c in widths],
        out_shape=[jax.ShapeDtypeStruct((n, c), F32) for c in widths],
        compiler_params=_cparams(("parallel",)),
        name="in_proj",
    )(x, g1, w_bf, qg, kg)


def _rwkv_prep_math(p, p_prev, mu, w0, w2p, a0, a2p, g2, kkp, kap, rk):
    ps = p + (p_prev - p) * mu
    r = ps[:, 0:RW_DIM]
    k = ps[:, RW_DIM:2 * RW_DIM]
    v = ps[:, 2 * RW_DIM:3 * RW_DIM]
    wa = ps[:, 3 * RW_DIM:3 * RW_DIM + W_LORA + A_LORA]
    gl = ps[:, 3 * RW_DIM + W_LORA + A_LORA:RW_PROJ]
    w = -jax.nn.softplus(-(w0 + jnp.dot(jnp.tanh(wa).astype(BF16), w2p, preferred_element_type=F32))) - 0.5
    decay = jnp.exp(-jnp.exp(w))
    a = jax.nn.sigmoid(a0 + jnp.dot(wa.astype(BF16), a2p, preferred_element_type=F32))
    g = jnp.dot(jax.nn.sigmoid(gl).astype(BF16), g2, preferred_element_type=F32)
    j = _seg_ones()
    kk = k * kkp
    kk = kk / jnp.maximum(jnp.sqrt(_seg64(kk * kk, j)), 1e-12)
    k2 = k * (1.0 + (a - 1.0) * kap)
    bonus = _seg64(r * k2 * rk, j) * v
    return r, decay, k2, v, -kk, kk * a, g, bonus


def _rwkv_prep_carry_kernel(p_ref, mu, w0, w2p, a0, a2p, g2, kkp, kap, rk, *rest):
    outs, carry = rest[:-1], rest[-1]

    @pl.when(pl.program_id(1) == 0)
    def _():
        carry[...] = jnp.zeros_like(carry)

    p = p_ref[...]
    rows = lax.broadcasted_iota(jnp.int32, p.shape, 0)
    p_prev = jnp.where(rows == 0, carry[...], pltpu.roll(p, 1, 0))
    carry[...] = p[p.shape[0] - 1:, :]
    res = _rwkv_prep_math(p, p_prev, mu[...], w0[...], w2p[...], a0[...], a2p[...], g2[...],
                          kkp[...], kap[...], rk[...])
    for o, val in zip(outs, res):
        o[...] = val


def _rwkv_prep_given_kernel(p_ref, pp_ref, mu, w0, w2p, a0, a2p, g2, kkp, kap, rk, *outs):
    res = _rwkv_prep_math(p_ref[...], pp_ref[...], mu[...], w0[...], w2p[...], a0[...], a2p[...],
                          g2[...], kkp[...], kap[...], rk[...])
    for o, val in zip(outs, res):
        o[...] = val


def _rwkv_prep_prompt(p3, params, tm):
    b, tp, _ = p3.shape
    assert tp % tm == 0
    full = lambda a: pl.BlockSpec(a.shape, lambda i, j: (0,) * a.ndim)
    return pl.pallas_call(
        _rwkv_prep_carry_kernel,
        grid=(b, tp // tm),
        in_specs=[pl.BlockSpec((None, tm, RW_PROJ), lambda i, j: (i, j, 0))] + [full(a) for a in params],
        out_specs=[pl.BlockSpec((None, tm, RW_DIM), lambda i, j: (i, j, 0))] * 8,
        out_shape=[jax.ShapeDtypeStruct((b, tp, RW_DIM), F32)] * 8,
        scratch_shapes=[pltpu.VMEM((1, RW_PROJ), F32)],
        compiler_params=_cparams(("arbitrary", "arbitrary")),
        name="rwkv_prep_prompt",
    )(p3, *params)


def _rwkv_prep_sample(p, p_prev, params):
    n = p.shape[0]
    full = lambda a: pl.BlockSpec(a.shape, lambda i: (0,) * a.ndim)
    return pl.pallas_call(
        _rwkv_prep_given_kernel,
        grid=(1,),
        in_specs=[full(p), full(p_prev)] + [full(a) for a in params],
        out_specs=[pl.BlockSpec((n, RW_DIM), lambda i: (0, 0))] * 8,
        out_shape=[jax.ShapeDtypeStruct((n, RW_DIM), F32)] * 8,
        compiler_params=_cparams(("arbitrary",)),
        name="rwkv_prep_sample",
    )(p, p_prev, *params)


def _scan_consts():
    lane = lax.broadcasted_iota(jnp.int32, (RW_HEAD, LANES), 1)
    sub = lax.broadcasted_iota(jnp.int32, (RW_HEAD, LANES), 0)
    first = lane < RW_HEAD
    eye_lo = (lane == sub).astype(F32)
    eye_hi = (lane - RW_HEAD == sub).astype(F32)
    return lane, first, eye_lo, eye_hi


def _seg(first, x_lo, x_hi):
    lo = jnp.sum(x_lo, axis=-1, keepdims=True)
    hi = jnp.sum(x_hi, axis=-1, keepdims=True)
    return jnp.where(first, lo, hi)


def _split_row(first, row):
    lo = jnp.where(first[0:1, :], row, 0.0)
    return lo, row - lo


def _scan_step_kernel(r_ref, w_ref, k_ref, v_ref, a_ref, b_ref, s0_ref, y_ref, s_ref, *, nb):
    _, first, eye_lo, eye_hi = _scan_consts()
    for b in range(nb):
        for p in range(RW_DIM // LANES):
            cols = pl.ds(p * LANES, LANES)
            r_t, w_t, k_t, v_t, a_t, b_t = [ref[pl.ds(b, 1), cols]
                                            for ref in (r_ref, w_ref, k_ref, v_ref, a_ref, b_ref)]
            s = s0_ref[b, :, cols]
            a_lo, a_hi = _split_row(first, a_t)
            r_lo, r_hi = _split_row(first, r_t)
            sa = _seg(first, s * a_lo, s * a_hi)
            vcol = _seg(first, eye_lo * v_t, eye_hi * v_t)
            s = s * w_t + sa * b_t + vcol * k_t
            s_ref[b, :, cols] = s
            ycol = _seg(first, s * r_lo, s * r_hi)
            y_ref[pl.ds(b, 1), cols] = jnp.sum((eye_lo + eye_hi) * ycol, axis=0, keepdims=True)


def _rwkv_step(r, w, k, v, a, b, s0):
    nb = r.shape[0]
    full = lambda x: pl.BlockSpec(x.shape, lambda i: (0,) * x.ndim)
    return pl.pallas_call(
        functools.partial(_scan_step_kernel, nb=nb),
        grid=(1,),
        in_specs=[full(r)] * 6 + [full(s0)],
        out_specs=[full(r), full(s0)],
        out_shape=[jax.ShapeDtypeStruct(r.shape, F32), jax.ShapeDtypeStruct(s0.shape, F32)],
        compiler_params=_cparams(("arbitrary",)),
        name="rwkv_step",
    )(r, w, k, v, a, b, s0)


def _scan_kernel(r_ref, w_ref, k_ref, v_ref, a_ref, b_ref, s0_ref, y_ref, sfin_ref,
                 s_scr, sa_scr, vc_scr, yacc_scr, vstage, zstage, yres, *, nb):
    npair = RW_DIM // LANES
    chains = [(b, p) for b in range(nb) for p in range(npair)]
    lane, first, eye_lo, eye_hi = _scan_consts()
    sub_t = 8

    @pl.when(pl.program_id(0) == 0)
    def _():
        for ci, (b, p) in enumerate(chains):
            s_scr[ci] = s0_ref[b, :, p * LANES:(p + 1) * LANES]

    yacc_scr[...] = jnp.zeros_like(yacc_scr)
    nch = len(chains)
    eye2 = eye_lo + eye_hi
    seg_ones = _seg_ones()
    zero_b = jnp.zeros((LANES, LANES), BF16)
    j_stack = jnp.concatenate([seg_ones, seg_ones], axis=0)
    j_pair = jnp.concatenate([jnp.concatenate([seg_ones, zero_b], axis=1),
                              jnp.concatenate([zero_b, seg_ones], axis=1)], axis=0)

    def prepare(ci, s, a_t):
        a_lo, a_hi = _split_row(first, a_t)
        sa_scr[ci] = _seg(first, s * a_lo, s * a_hi)

    def group(tg, carry):
        rows = pl.ds(pl.multiple_of(tg * sub_t, sub_t), sub_t)

        def row(ref, ci, i):
            b, p = chains[ci]
            return ref[b, rows, pl.ds(p * LANES, LANES)][i:i + 1, :]

        for i in range(sub_t):
            for ci in range(nch):
                v_t = row(v_ref, ci, i)
                v_hi = v_t.astype(BF16).astype(F32)
                lhs = jnp.concatenate([eye2 * v_hi, eye2 * (v_t - v_hi)], axis=1)
                vstage[pl.ds((i * nch + ci) * RW_HEAD, RW_HEAD), :] = lhs.astype(BF16)
        vc_scr[...] = jnp.dot(vstage[...], j_stack, preferred_element_type=F32).reshape(vc_scr.shape)

        for ci in range(nch):
            prepare(ci, s_scr[ci], row(a_ref, ci, 0))
        for i in range(sub_t):
            for ci in range(nch):
                s = s_scr[ci] * row(w_ref, ci, i) + sa_scr[ci] * row(b_ref, ci, i) \
                    + vc_scr[i * nch + ci] * row(k_ref, ci, i)
                s_scr[ci] = s
                if i + 1 < sub_t:
                    prepare(ci, s, row(a_ref, ci, i + 1))
                z = (s * row(r_ref, ci, i)).astype(BF16)
                zstage[pl.ds((i * (nch // 2) + ci // 2) * RW_HEAD, RW_HEAD),
                       pl.ds((ci % 2) * LANES, LANES)] = z
        yres[...] = jnp.dot(zstage[...], j_pair, preferred_element_type=F32)
        for i in range(sub_t):
            place = (lane % RW_HEAD) == tg * sub_t + i
            for ci in range(nch):
                ycol = yres[pl.ds((i * (nch // 2) + ci // 2) * RW_HEAD, RW_HEAD),
                            pl.ds((ci % 2) * LANES, LANES)]
                yacc_scr[ci] = jnp.where(place, ycol, yacc_scr[ci])
        return carry

    lax.fori_loop(0, SCAN_CHUNK // sub_t, group, 0)

    zeros = jnp.zeros((LANES - RW_HEAD, LANES), F32)
    for ci, (b, p) in enumerate(chains):
        cols = pl.ds(p * LANES, LANES)
        tr = jnp.concatenate([yacc_scr[ci], zeros], axis=0).T
        y_ref[b, :, cols] = jnp.where(first, tr[0:RW_HEAD], pltpu.roll(tr[RW_HEAD:], RW_HEAD, 1))
        sfin_ref[b, :, cols] = s_scr[ci]


def _rwkv_scan(r, w, k, v, a, b, s0):
    nb, tp, _ = r.shape
    assert tp % SCAN_CHUNK == 0 and SCAN_CHUNK == RW_HEAD
    nch = nb * (RW_DIM // LANES)
    seq = pl.BlockSpec((nb, SCAN_CHUNK, RW_DIM), lambda i: (0, i, 0))
    st = pl.BlockSpec((nb, RW_HEAD, RW_DIM), lambda i: (0, 0, 0))
    return pl.pallas_call(
        functools.partial(_scan_kernel, nb=nb),
        grid=(tp // SCAN_CHUNK,),
        in_specs=[seq] * 6 + [st],
        out_specs=[seq, st],
        out_shape=[jax.ShapeDtypeStruct((nb, tp, RW_DIM), F32),
                   jax.ShapeDtypeStruct((nb, RW_HEAD, RW_DIM), F32)],
        scratch_shapes=[pltpu.VMEM((nch, RW_HEAD, LANES), F32),
                        pltpu.VMEM((nch, RW_HEAD, LANES), F32),
                        pltpu.VMEM((8 * nch, RW_HEAD, LANES), F32),
                        pltpu.VMEM((nch, RW_HEAD, LANES), F32),
                        pltpu.VMEM((8 * nch * RW_HEAD, 2 * LANES), BF16),
                        pltpu.VMEM((4 * nch * RW_HEAD, 2 * LANES), BF16),
                        pltpu.VMEM((4 * nch * RW_HEAD, 2 * LANES), F32)],
        compiler_params=_cparams(("arbitrary",)),
        name="rwkv_scan",
    )(r, w, k, v, a, b, s0)


def _lambda_value(lam_ref):
    lv = lam_ref[...]
    s01 = jnp.sum(lv[0:1, :] * lv[1:2, :], axis=-1, keepdims=True)
    s23 = jnp.sum(lv[2:3, :] * lv[3:4, :], axis=-1, keepdims=True)
    return jnp.exp(s01) - jnp.exp(s23) + LAM_INIT


def _alibi_slope(h):
    slope = jnp.float32(2.0 ** (-8.0 * DA_HEADS / DA_HEADS))
    for i in range(DA_HEADS - 2, -1, -1):
        slope = jnp.where(h == i, jnp.float32(2.0 ** (-8.0 * (i + 1) / DA_HEADS)), slope)
    return slope


def _sub_norm(o, g):
    return o * lax.rsqrt(jnp.mean(o * o, axis=-1, keepdims=True) + NORM_EPS) * g * (1.0 - LAM_INIT)


ATT_ONES = 16


def _attn_prompt_kernel(q_ref, k_ref, v_ref, d0_ref, lam_ref, sg_ref, o_ref,
                        qc_ref, vt_ref, bias_ref, m_ref, acc_ref, s_ref, *, blk):
    h = pl.program_id(1)
    qi = pl.program_id(2)
    slope = _alibi_slope(h)
    tp = k_ref.shape[0]
    sub = blk // LANES

    @pl.when(qi == 0)
    def _():
        for jb in range(tp // LANES):
            vt = v_ref[jb * LANES:(jb + 1) * LANES, :].T.astype(BF16)
            vt_ref[jb // sub, 0:DA_DV, (jb % sub) * LANES:(jb % sub + 1) * LANES] = vt
        vt_ref[:, DA_DV:, :] = jnp.ones((tp // blk, ATT_ONES, blk), BF16)

    lane = lax.broadcasted_iota(jnp.int32, (blk, LANES), 1)
    qb = q_ref[...].astype(BF16)
    for c in range(2):
        qc_ref[c] = jnp.where((lane // DA_DK) == c, qb, jnp.zeros_like(qb))
    bias_ref[...] = slope * d0_ref[...]
    m_ref[...] = jnp.full_like(m_ref, NEG_BIG)
    acc_ref[...] = jnp.zeros_like(acc_ref)

    def scores(j):
        kb = k_ref[pl.ds(pl.multiple_of(j * blk, blk), blk), :].astype(BF16)
        for c in range(2):
            s_ref[(j % 2) * 2 + c] = lax.dot_general(kb, qc_ref[c], (((1,), (1,)), ((), ())),
                                                     preferred_element_type=F32)

    def consume(j, masked):
        off = ((qi - j) * blk).astype(F32)
        c0 = slope * off
        if masked:
            kpos = j * blk + lax.broadcasted_iota(jnp.int32, (blk, blk), 0)
            valid = (d0_ref[...] + off >= 0.0) & (kpos >= PAD_LEAD)
        for c in range(2):
            t = s_ref[(j % 2) * 2 + c] - bias_ref[...]
            if masked:
                t = jnp.where(valid, t, NEG_BIG)
            m_old = m_ref[c]
            m_new = jnp.maximum(m_old, jnp.max(t, axis=0, keepdims=True) - c0)
            alpha = jnp.exp(m_old - m_new)
            p = jnp.exp(t - (m_new + c0)).astype(BF16)
            acc_ref[c] = alpha * acc_ref[c] + jnp.dot(vt_ref[j], p, preferred_element_type=F32)
            m_ref[c] = m_new

    def middle(j, carry):
        consume(j, False)
        scores(j + 1)
        return carry

    scores(0)
    consume(0, True)
    scores(jnp.minimum(1, tp // blk - 1))
    lax.fori_loop(1, qi, middle, 0)

    @pl.when(qi > 0)
    def _():
        consume(qi, True)

    lam = _lambda_value(lam_ref)
    a0 = acc_ref[0]
    a1 = acc_ref[1]
    ot = a0[:DA_DV] / a0[DA_DV:DA_DV + 1] - lam * (a1[:DA_DV] / a1[DA_DV:DA_DV + 1])
    o_ref[...] = _sub_norm(ot.T, sg_ref[...])


def _attn_prompt(q, k, v, da_lam, subln_g, blk):
    b, tp, _ = q.shape
    assert tp % blk == 0 and blk % LANES == 0 and PAD_LEAD <= blk
    d0t = (jnp.arange(blk, dtype=F32)[None, :] - jnp.arange(blk, dtype=F32)[:, None])
    qspec = pl.BlockSpec((None, blk, DA_DV), lambda bi, h, i: (bi, i, h))
    kspec = pl.BlockSpec((None, tp, DA_DV), lambda bi, h, i: (bi, 0, h))
    full = lambda a: pl.BlockSpec(a.shape, lambda bi, h, i: (0,) * a.ndim)
    return pl.pallas_call(
        functools.partial(_attn_prompt_kernel, blk=blk),
        grid=(b, DA_HEADS, tp // blk),
        in_specs=[qspec, kspec, kspec, full(d0t), full(da_lam), full(subln_g)],
        out_specs=qspec,
        scratch_shapes=[pltpu.VMEM((2, blk, DA_DV), BF16),
                        pltpu.VMEM((tp // blk, DA_DV + ATT_ONES, blk), BF16),
                        pltpu.VMEM((blk, blk), F32), pltpu.VMEM((2, 1, blk), F32),
                        pltpu.VMEM((2, DA_DV + ATT_ONES, blk), F32),
                        pltpu.VMEM((4, blk, blk), F32)],
        out_shape=jax.ShapeDtypeStruct((b, tp, DA_V), F32),
        compiler_params=_cparams(("parallel", "parallel", "arbitrary")),
        name="attn_prompt",
    )(q, k, v, d0t, da_lam, subln_g)


def _attn_sample_kernel(pt_ref, q_ref, kn_ref, vn_ref, lam_ref, sg_ref, *rest, ppg, page, past):
    k_refs = rest[:ppg]
    v_refs = rest[ppg:2 * ppg]
    o_ref, m_ref, l_ref, acc_ref = rest[2 * ppg:]
    g = pl.program_id(1)
    nrow = 2 * DA_HEADS
    rowi = lax.broadcasted_iota(jnp.int32, (nrow, DA_QK), 0)
    lanei = lax.broadcasted_iota(jnp.int32, (nrow, DA_QK), 1)
    qsel = (lanei // DA_DK) == rowi
    qmat = jnp.where(qsel, q_ref[...], 0.0)
    rh = lax.broadcasted_iota(jnp.int32, (nrow, 1), 0) // 2
    slope = jnp.full((nrow, 1), 2.0 ** (-8.0), F32)
    for i in range(DA_HEADS - 2, -1, -1):
        slope = jnp.where(rh == i, jnp.float32(2.0 ** (-8.0 * (i + 1) / DA_HEADS)), slope)

    @pl.when(g == 0)
    def _():
        m_ref[...] = jnp.full_like(m_ref, NEG_BIG)
        l_ref[...] = jnp.zeros_like(l_ref)
        acc_ref[...] = jnp.zeros_like(acc_ref)

    qb = qmat.astype(BF16)
    kpos = g * (ppg * page) + lax.broadcasted_iota(jnp.int32, (nrow, ppg * page), 1)
    s = jnp.concatenate([jnp.dot(qb, k_refs[i][...].astype(BF16), preferred_element_type=F32)
                         for i in range(ppg)], axis=1)
    s = s - slope * (past - kpos).astype(F32)
    m_old = m_ref[...]
    m_new = jnp.maximum(m_old, jnp.max(s, axis=-1, keepdims=True))
    alpha = jnp.exp(m_old - m_new)
    p = jnp.exp(s - m_new)
    l_ref[...] = alpha * l_ref[...] + jnp.sum(p, axis=-1, keepdims=True)
    pb = p.astype(BF16)
    pv = [jnp.zeros((nrow, DA_DV), F32)] * DA_HEADS
    for i in range(ppg):
        for hh in range(DA_HEADS):
            vh = v_refs[i][pl.ds(hh, page, stride=DA_HEADS), :].astype(BF16)
            pv[hh] = pv[hh] + jnp.dot(pb[:, i * page:(i + 1) * page], vh, preferred_element_type=F32)
    tot = pv[DA_HEADS - 1]
    for hh in range(DA_HEADS - 1):
        tot = jnp.where(rh == hh, pv[hh], tot)
    acc_ref[...] = alpha * acc_ref[...] + tot
    m_ref[...] = m_new

    @pl.when(g == pl.num_programs(1) - 1)
    def _():
        kn = kn_ref[...].astype(BF16).astype(F32)
        s_self = jnp.sum(qb.astype(F32) * kn, axis=-1, keepdims=True)
        m_old = m_ref[...]
        m_new = jnp.maximum(m_old, s_self)
        alpha = jnp.exp(m_old - m_new)
        p_self = jnp.exp(s_self - m_new)
        l_fin = alpha * l_ref[...] + p_self
        vn = vn_ref[...].astype(BF16).astype(F32)
        vrow = jnp.zeros((nrow, DA_DV), F32)
        for hh in range(DA_HEADS):
            vrow = jnp.where(rh == hh, vn[:, hh * DA_DV:(hh + 1) * DA_DV], vrow)
        acc = alpha * acc_ref[...] + p_self.astype(BF16).astype(F32) * vrow
        lam = _lambda_value(lam_ref)
        ri = lax.broadcasted_iota(jnp.int32, (nrow, 1), 0)
        t = acc / l_fin * jnp.where(ri % 2 == 0, 1.0, -lam)
        outs = []
        for hh in range(DA_HEADS):
            o = t[2 * hh:2 * hh + 1, :] + t[2 * hh + 1:2 * hh + 2, :]
            outs.append(_sub_norm(o, sg_ref[...]))
        o_ref[...] = jnp.concatenate(outs, axis=1)


def _attn_sample(q, kn, vn, cache_kt, cache_v, page_table, da_lam, subln_g, ppg):
    bd = q.shape[0]
    n_pages = page_table.shape[1]
    page = cache_kt.shape[2]
    assert n_pages % ppg == 0 and cache_v.shape[1:] == (page * DA_HEADS, DA_DV)
    pt = page_table.reshape(-1)
    one = pl.BlockSpec((None, 1, DA_QK), lambda b, g, pt: (b, 0, 0))
    full = lambda a: pl.BlockSpec(a.shape, lambda b, g, pt: (0,) * a.ndim)

    def pspec(i):
        return pl.BlockSpec((None, DA_QK, page),
                            lambda b, g, pt, i=i: (pt[b * n_pages + g * ppg + i], 0, 0))

    out = pl.pallas_call(
        functools.partial(_attn_sample_kernel, ppg=ppg, page=page, past=n_pages * page),
        grid_spec=pltpu.PrefetchScalarGridSpec(
            num_scalar_prefetch=1,
            grid=(bd, n_pages // ppg),
            in_specs=[one, one, one, full(da_lam), full(subln_g)]
                     + [pspec(i) for i in range(ppg)] * 2,
            out_specs=one,
            scratch_shapes=[pltpu.VMEM((2 * DA_HEADS, 1), F32), pltpu.VMEM((2 * DA_HEADS, 1), F32),
                            pltpu.VMEM((2 * DA_HEADS, DA_DV), F32)],
        ),
        out_shape=jax.ShapeDtypeStruct((bd, 1, DA_V), F32),
        compiler_params=_cparams(("parallel", "arbitrary")),
        name="attn_sample",
    )(pt, q.reshape(bd, 1, DA_QK), kn.reshape(bd, 1, DA_QK), vn.reshape(bd, 1, DA_V), da_lam, subln_g,
      *([cache_kt] * ppg), *([cache_v] * ppg))
    return out.reshape(bd, DA_V)


def _merge_kernel(y_ref, bonus_ref, g_ref, lng_ref, lnb_ref, o_ref, gt_ref, h_ref, wrw_ref, wda_ref,
                  wout_ref, n2_ref, wq_ref, sk_ref, h2_ref, xn_ref, st_ref):
    j = _seg_ones()
    y = y_ref[...]
    mu = _seg64(y, j) * (1.0 / RW_HEAD)
    d = y - mu
    var = _seg64(d * d, j) * (1.0 / RW_HEAD)
    yln = d * lax.rsqrt(var + RW_LN_EPS) * lng_ref[...] + lnb_ref[...]
    yr = ((yln + bonus_ref[...]) * g_ref[...]).astype(BF16)
    y_rw = jnp.dot(yr, wrw_ref[...], preferred_element_type=F32)
    y_da = jnp.dot(o_ref[...].astype(BF16), wda_ref[...], preferred_element_type=F32)
    gt = gt_ref[...]
    mix = gt[:, :D_MODEL] * y_rw + gt[:, D_MODEL:] * y_da
    h2 = h_ref[...] + jnp.dot(mix.astype(BF16), wout_ref[...], preferred_element_type=F32)
    h2_ref[...] = h2
    ms = jnp.mean(h2 * h2, axis=-1, keepdims=True)
    xn = (h2 * lax.rsqrt(ms + NORM_EPS) * n2_ref[...]).astype(BF16)
    xn_ref[...] = xn
    q = jnp.dot(xn, wq_ref[...], preferred_element_type=F32)
    for hc in range(2 * PEER_HEADS):
        qc = q[:, hc * PEER_DHALF:(hc + 1) * PEER_DHALF].astype(BF16)
        st_ref[hc * N_KEYS:(hc + 1) * N_KEYS, :] = lax.dot_general(
            sk_ref[hc % 2], qc, (((1,), (1,)), ((), ())), preferred_element_type=F32)


def _merge(y, bonus, g, o, gates, h, consts, tm, tile_off):
    b, rows, _ = y.shape
    assert rows % tm == 0
    nt = rows // tm - tile_off
    rin = lambda c: pl.BlockSpec((None, tm, c), lambda i, t: (i, t + tile_off, 0))
    full = lambda a: pl.BlockSpec(a.shape, lambda i, t: (0,) * a.ndim)
    ntok = b * nt * tm
    return pl.pallas_call(
        _merge_kernel,
        grid=(b, nt),
        in_specs=[rin(RW_DIM), rin(RW_DIM), rin(RW_DIM), full(consts[0]), full(consts[1]), rin(DA_V),
                  rin(2 * D_MODEL), rin(D_MODEL)] + [full(a) for a in consts[2:]],
        out_specs=[pl.BlockSpec((tm, D_MODEL), lambda i, t: (i * nt + t, 0)),
                   pl.BlockSpec((tm, D_MODEL), lambda i, t: (i * nt + t, 0)),
                   pl.BlockSpec((2 * PEER_HEADS * N_KEYS, tm), lambda i, t: (0, i * nt + t))],
        out_shape=[jax.ShapeDtypeStruct((ntok, D_MODEL), F32),
                   jax.ShapeDtypeStruct((ntok, D_MODEL), BF16),
                   jax.ShapeDtypeStruct((2 * PEER_HEADS * N_KEYS, ntok), F32)],
        compiler_params=_cparams(("parallel", "parallel")),
        name="merge",
    )(y, bonus, g, consts[0], consts[1], o, gates, h, *consts[2:])


def _count(mask):
    return jnp.sum(jnp.where(mask, 1.0, 0.0), axis=0, keepdims=True)


def _top16(x, exact):
    nk = x.shape[0]
    iota = lax.broadcasted_iota(jnp.int32, x.shape, 0).astype(F32)
    rank = jnp.full(x.shape, float(nk), F32)
    vals = []
    for r in range(PEER_TOPK):
        m = jnp.max(x, axis=0, keepdims=True)
        sel = x == m
        if exact:
            cand = jnp.where(sel, iota, float(nk))
            sel = cand == jnp.min(cand, axis=0, keepdims=True)
        x = jnp.where(sel, -jnp.inf, x)
        rank = jnp.where(sel, float(r), rank)
        vals.append(m)
    return vals, rank


def _peer_topk_kernel(st_ref, rank2_ref, c2_ref, cnt1_ref, c1_ref):
    n = st_ref.shape[1]
    rid = lax.broadcasted_iota(jnp.int32, (_CAND_ROWS, n), 0)
    flat = jnp.zeros((_CAND_ROWS, n), jnp.int32) + PEER_TOPK * PEER_TOPK
    irow = jnp.zeros((_CAND_ROWS, n), jnp.int32) + PEER_TOPK
    for ci, (i, j) in enumerate(_CAND):
        flat = jnp.where(rid == ci, i * PEER_TOPK + j, flat)
        irow = jnp.where(rid == ci, i, irow)
    outs = (rank2_ref, c2_ref, cnt1_ref, c1_ref)
    ties = _peer_topk_body(st_ref, outs, flat, irow, exact=False)

    @pl.when(jnp.max(ties) > 0.0)
    def _():
        _peer_topk_body(st_ref, outs, flat, irow, exact=True)


def _peer_topk_body(st_ref, outs, flat, irow, exact):
    rank2_ref, c2_ref, cnt1_ref, c1_ref = outs
    n = st_ref.shape[1]
    ties = jnp.zeros((1, n), F32)
    for h in range(PEER_HEADS):
        s1 = st_ref[(2 * h) * N_KEYS:(2 * h + 1) * N_KEYS, :]
        s2 = st_ref[(2 * h + 1) * N_KEYS:(2 * h + 2) * N_KEYS, :]
        v1, rank1 = _top16(s1, exact)
        v2, rank2 = _top16(s2, exact)
        rows = [v1[i] + v2[j] for (i, j) in _CAND]
        rows += [jnp.full((1, n), -jnp.inf, F32)] * (_CAND_ROWS - len(_CAND))
        cand = jnp.concatenate(rows, axis=0)
        top = v1[0] + v2[0]
        chosen = jnp.zeros(cand.shape, jnp.bool_)
        work = cand
        for _ in range(PEER_TOPK):
            m = jnp.max(work, axis=0, keepdims=True)
            sel = work == m
            if exact:
                fl = jnp.where(sel, flat, PEER_TOPK * PEER_TOPK + 1)
                sel = fl == jnp.min(fl, axis=0, keepdims=True)
            work = jnp.where(sel, -jnp.inf, work)
            chosen = chosen | sel
        if not exact:
            k = float(PEER_TOPK)
            bad = (_count(rank1 < k) != k) | (_count(rank2 < k) != k) | (_count(chosen) != k)
            ties = jnp.maximum(ties, jnp.where(bad, 1.0, 0.0))
        z = jnp.sum(jnp.where(chosen, jnp.exp(cand - top), 0.0), axis=0, keepdims=True)
        cnt1 = jnp.zeros(s1.shape, F32)
        for i in range(PEER_TOPK):
            m_i = jnp.sum(jnp.where(chosen & (irow == i), 1.0, 0.0), axis=0, keepdims=True)
            cnt1 = jnp.where(rank1 == float(i), m_i, cnt1)
        cnt1_ref[h] = cnt1
        c1_ref[h] = jnp.where(rank1 < float(PEER_TOPK), jnp.exp(s1 - v1[0]) / z, 0.0)
        c2_ref[h] = jnp.where(rank2 < float(PEER_TOPK), jnp.exp(s2 - v2[0]), 0.0).astype(BF16)
        rank2_ref[h] = rank2.astype(BF16)
    return ties


def _peer_topk(st, tn):
    ntok = st.shape[1]
    assert ntok % tn == 0
    spec = pl.BlockSpec((PEER_HEADS, N_KEYS, tn), lambda i: (0, 0, i))
    return pl.pallas_call(
        _peer_topk_kernel,
        grid=(ntok // tn,),
        in_specs=[pl.BlockSpec((st.shape[0], tn), lambda i: (0, i))],
        out_specs=[spec] * 4,
        out_shape=[jax.ShapeDtypeStruct((PEER_HEADS, N_KEYS, ntok), dt) for dt in (BF16, BF16, F32, F32)],
        compiler_params=_cparams(("parallel",)),
        name="peer_topk",
    )(st)


def _peer_dense_kernel(x_ref, u_ref, vt_ref, rank2_ref, c2_ref, cnt1_ref, c1_ref, h2_ref, y_ref,
                       acc_ref, hid_ref, p_ref, *, e1b):
    j = pl.program_id(1)
    tn = x_ref.shape[0]
    ngrp = N_KEYS // BF16_ROWS

    @pl.when(j == 0)
    def _():
        acc_ref[...] = jnp.zeros_like(acc_ref)

    hid_ref[...] = lax.dot_general(u_ref[...], x_ref[...], (((1,), (1,)), ((), ())),
                                   preferred_element_type=F32)

    for e in range(e1b):
        e1 = j * e1b + e
        gate = [None] * ngrp
        for h in range(PEER_HEADS):
            cnt = jnp.broadcast_to(cnt1_ref[h, pl.ds(e1, 1), :], (BF16_ROWS, tn)).astype(BF16)
            c1 = jnp.broadcast_to(c1_ref[h, pl.ds(e1, 1), :], (BF16_ROWS, tn)).astype(BF16)
            for g in range(ngrp):
                rows = slice(g * BF16_ROWS, (g + 1) * BF16_ROWS)
                c2 = c2_ref[h, rows, :]
                term = jnp.where(rank2_ref[h, rows, :] < cnt, c2, jnp.zeros_like(c2)) * c1
                gate[g] = term if gate[g] is None else gate[g] + term
        hh = hid_ref[e * N_KEYS:(e + 1) * N_KEYS, :]
        gelu = (0.5 * hh * (1.0 + lax.erf(hh * (2.0 ** -0.5)))).astype(BF16)
        for g in range(ngrp):
            r0 = e * N_KEYS + g * BF16_ROWS
            p_ref[r0:r0 + BF16_ROWS, :] = gelu[g * BF16_ROWS:(g + 1) * BF16_ROWS] * gate[g]
    acc_ref[...] += jnp.dot(vt_ref[...], p_ref[...], preferred_element_type=F32)

    @pl.when(j == pl.num_programs(1) - 1)
    def _():
        y_ref[...] = h2_ref[...] + acc_ref[...].T


def _peer_dense(xn, u_bf, vt_bf, rank2, c2, cnt1, c1, h2, tn, e1b):
    ntok = xn.shape[0]
    assert ntok % tn == 0 and N_KEYS % e1b == 0
    tok = pl.BlockSpec((PEER_HEADS, N_KEYS, tn), lambda i, j: (0, 0, i))
    row = pl.BlockSpec((tn, D_MODEL), lambda i, j: (i, 0))
    return pl.pallas_call(
        functools.partial(_peer_dense_kernel, e1b=e1b),
        grid=(ntok // tn, N_KEYS // e1b),
        in_specs=[row,
                  pl.BlockSpec((e1b * N_KEYS, D_MODEL), lambda i, j: (j, 0)),
                  pl.BlockSpec((D_MODEL, e1b * N_KEYS), lambda i, j: (0, j)),
                  tok, tok, tok, tok, row],
        out_specs=row,
        out_shape=jax.ShapeDtypeStruct((ntok, D_MODEL), F32),
        scratch_shapes=[pltpu.VMEM((D_MODEL, tn), F32), pltpu.VMEM((e1b * N_KEYS, tn), F32),
                        pltpu.VMEM((e1b * N_KEYS, tn), BF16)],
        compiler_params=_cparams(("parallel", "arbitrary")),
        name="peer_dense",
    )(xn, u_bf, vt_bf, rank2, c2, cnt1, c1, h2)


def _state_in(s):
    b = s.shape[0]
    return jnp.transpose(s, (0, 2, 1, 3)).reshape(b, RW_HEAD, RW_DIM)


def _state_out(s):
    b = s.shape[0]
    return jnp.transpose(s.reshape(b, RW_HEAD, RW_HEADS, RW_HEAD), (0, 2, 1, 3))


def _pad_tokens(a, n, axis):
    pad = [(0, 0)] * a.ndim
    pad[axis] = (0, n - a.shape[axis])
    return jnp.pad(a, pad)


def kernel(x_prompt, x_sample, cache_k, cache_v, state_wkv, state_shift, page_table, meta, norm1_g, w_in,
           rw_mu, rw_w0, rw_w2, rw_a0, rw_a2, rw_g2, rw_kk, rw_ka, rw_rk, rw_ln_g, rw_ln_b, w_rw_br,
           da_qn_g, da_kn_g, da_lam, da_subln_g, w_da_br, w_out, norm2_g, peer_wq, peer_subkeys,
           peer_u, peer_v):
    b, seq, _ = x_prompt.shape
    bd = x_sample.shape[0]
    t_real = N_META + seq
    tp = PAD_LEAD + t_real
    assert tp % ROW_TILE == 0 and tp % SCAN_CHUNK == 0 and x_sample.shape[1] == 1
    row2 = lambda a: a.reshape(1, -1)

    w_in_bf = w_in[0].astype(BF16)
    g1 = row2(norm1_g[0])
    qg = row2(jnp.tile(da_qn_g[0], DA_QK // DA_DK))
    kg = row2(jnp.tile(da_kn_g[0], DA_QK // DA_DK))
    zlo = jnp.zeros((W_LORA, RW_DIM), F32)
    prep_params = (row2(rw_mu[0]), row2(rw_w0[0]),
                   jnp.concatenate([rw_w2[0], zlo], axis=0).astype(BF16), row2(rw_a0[0]),
                   jnp.concatenate([zlo, rw_a2[0]], axis=0).astype(BF16), rw_g2[0].astype(BF16),
                   row2(rw_kk[0]), row2(rw_ka[0]), row2(rw_rk[0]))
    merge_consts = (row2(rw_ln_g[0]), row2(rw_ln_b[0]), w_rw_br[0].astype(BF16), w_da_br[0].astype(BF16),
                    w_out[0].astype(BF16), row2(norm2_g[0]), peer_wq[0].astype(BF16),
                    peer_subkeys[0].astype(BF16))
    subln = row2(da_subln_g[0])
    u_bf = peer_u[0].astype(BF16)
    vt_bf = peer_v[0].astype(BF16).T

    hp = jnp.concatenate([jnp.zeros((b, PAD_LEAD, D_MODEL), F32),
                          jnp.broadcast_to(meta[None], (b, N_META, D_MODEL)), x_prompt], axis=1)
    prw, q, k, v, gates = _in_proj(hp.reshape(b * tp, D_MODEL), g1, w_in_bf, qg, kg, 256)
    r3 = lambda a: a.reshape(b, tp, -1)
    pr, pw, pk, pv, pa, pb, pg, pbonus = _rwkv_prep_prompt(r3(prw), prep_params, 384)
    y_scan, s_fin = _rwkv_scan(pr, pw, pk, pv, pa, pb, jnp.zeros((b, RW_HEAD, RW_DIM), F32))
    o_attn = _attn_prompt(r3(q), r3(k), r3(v), da_lam[0], subln, 384)
    h2, xn2, st = _merge(y_scan, pbonus, pg, o_attn, r3(gates), hp, merge_consts, ROW_TILE, 1)
    y_prompt = _peer_dense(xn2, u_bf, vt_bf, *_peer_topk(st, 256), h2, 512, 16)

    xs = x_sample.reshape(bd, D_MODEL)
    prw_s, q_s, k_s, v_s, gates_s = _in_proj(xs, g1, w_in_bf, qg, kg, bd)
    sr, sw, sk, sv, sa, sb, sg, sbonus = _rwkv_prep_sample(prw_s, state_shift[0], prep_params)
    ys_scan, ss_fin = _rwkv_step(sr, sw, sk, sv, sa, sb, _state_in(state_wkv[0]))
    n_phys = cache_k.shape[1]
    ckt = jnp.transpose(cache_k[0].reshape(n_phys, -1, DA_QK), (0, 2, 1))
    cv = cache_v[0].reshape(n_phys, -1, DA_DV)
    o_s = _attn_sample(q_s, k_s, v_s, ckt, cv, page_table, da_lam[0], subln, 16)
    f3 = lambda a: a.reshape(1, bd, -1)
    h2s, xn2s, sts = _merge(ys_scan.reshape(1, bd, RW_DIM), f3(sbonus), f3(sg), f3(o_s), f3(gates_s),
                            f3(xs), merge_consts, bd, 0)
    npad = LANES
    tk_s = _peer_topk(_pad_tokens(sts, npad, 1), npad)
    y_s = _peer_dense(_pad_tokens(xn2s, npad, 0), u_bf, vt_bf, *tk_s, _pad_tokens(h2s, npad, 0), npad, 8)

    y_prompt = y_prompt.reshape(b, seq, D_MODEL)
    y_sample = y_s[:bd].reshape(bd, 1, D_MODEL)
    k_p = r3(k)[:, PAD_LEAD:].reshape(1, b, t_real, DA_HEADS, 2, DA_DK)
    v_p = r3(v)[:, PAD_LEAD:].reshape(1, b, t_real, DA_HEADS, DA_DV)
    return (y_prompt, y_sample, k_p, v_p,
            k_s.reshape(1, bd, 1, DA_HEADS, 2, DA_DK), v_s.reshape(1, bd, 1, DA_HEADS, DA_DV),
            _state_out(s_fin)[None], _state_out(ss_fin)[None],
            r3(prw)[:, -1][None], prw_s[None])
```

```python
import functools
import math

import jax
import jax.numpy as jnp
from jax import lax
from jax.experimental import pallas as pl
from jax.experimental.pallas import tpu as pltpu

F32 = jnp.float32
BF16 = jnp.bfloat16

D_MODEL = 1024
N_META = 16
NORM_EPS = 1e-6
RW_HEAD = 64
RW_DIM = D_MODEL // 2
RW_HEADS = RW_DIM // RW_HEAD
W_LORA = 64
A_LORA = 64
G_LORA = 128
RW_PROJ = 3 * RW_DIM + W_LORA + A_LORA + G_LORA
RW_LN_EPS = 64e-5
DA_DK = 64
DA_DV = 2 * DA_DK
DA_DIM = D_MODEL // 2
DA_HEADS = DA_DIM // DA_DV
DA_QK = DA_HEADS * 2 * DA_DK
DA_V = DA_HEADS * DA_DV
N_KEYS = 128
PEER_HEADS = 8
PEER_TOPK = 16
PEER_DQ = 256
PEER_DHALF = PEER_DQ // 2
LAM_INIT = 0.8 - 0.6 * math.exp(-0.3 * 0)

LANES = 128
BF16_ROWS = 16
ROW_TILE = 128
PAD_LEAD = ROW_TILE - N_META
SCAN_CHUNK = 64
NEG_BIG = -1e30
VMEM_LIMIT = 56 * 1024 * 1024

O1 = RW_PROJ
O2 = O1 + DA_QK
O3 = O2 + DA_QK
O4 = O3 + DA_V
O5 = O4 + D_MODEL
O6 = O5 + D_MODEL

_CAND = [(i, j) for i in range(PEER_TOPK) for j in range(PEER_TOPK) if (i + 1) * (j + 1) <= PEER_TOPK]
_CAND_ROWS = -(-len(_CAND) // 8) * 8


def _cparams(sem):
    return pltpu.CompilerParams(dimension_semantics=sem, vmem_limit_bytes=VMEM_LIMIT)


def _split_dot(x, j):
    hi = x.astype(BF16)
    lo = (x - hi.astype(F32)).astype(BF16)
    return (jnp.dot(hi, j, preferred_element_type=F32)
            + jnp.dot(lo, j, preferred_element_type=F32))


def _seg64(x, j):
    outs = [_split_dot(x[:, c * LANES:(c + 1) * LANES], j) for c in range(x.shape[1] // LANES)]
    return outs[0] if len(outs) == 1 else jnp.concatenate(outs, axis=1)


def _seg_ones():
    r = lax.broadcasted_iota(jnp.int32, (LANES, LANES), 0) // 64
    c = lax.broadcasted_iota(jnp.int32, (LANES, LANES), 1) // 64
    return (r == c).astype(BF16)


def _in_proj_kernel(x_ref, g1_ref, w_ref, qg_ref, kg_ref, prw_ref, q_ref, k_ref, v_ref, gt_ref):
    x = x_ref[...]
    ms = jnp.mean(x * x, axis=-1, keepdims=True)
    xn = (x * lax.rsqrt(ms + NORM_EPS) * g1_ref[...]).astype(BF16)

    def mm(a, b):
        return jnp.dot(xn, w_ref[:, a:b], preferred_element_type=F32)

    j = _seg_ones()

    def head_norm(t, g):
        msq = _seg64(t * t, j) * (1.0 / DA_DK)
        return t * lax.rsqrt(msq + NORM_EPS) * g

    prw_ref[...] = mm(0, O1)
    q_ref[...] = head_norm(mm(O1, O2), qg_ref[...]) * (DA_DK ** -0.5)
    k_ref[...] = head_norm(mm(O2, O3), kg_ref[...])
    v_ref[...] = mm(O3, O4)
    gt_ref[...] = jax.nn.sigmoid(mm(O4, O6))


def _in_proj(x, g1, w_bf, qg, kg, tm):
    n = x.shape[0]
    assert n % tm == 0
    row = lambda c: pl.BlockSpec((tm, c), lambda i: (i, 0))
    full = lambda a: pl.BlockSpec(a.shape, lambda i: (0,) * a.ndim)
    widths = (RW_PROJ, DA_QK, DA_QK, DA_V, 2 * D_MODEL)
    return pl.pallas_call(
        _in_proj_kernel,
        grid=(n // tm,),
        in_specs=[row(D_MODEL), full(g1), full(w_bf), full(qg), full(kg)],
        out_specs=[row(c) for c in widths],
        out_shape=[jax.ShapeDtypeStruct((n, c), F32) for c in widths],
        compiler_params=_cparams(("parallel",)),
        name="in_proj",
    )(x, g1, w_bf, qg, kg)


def _rwkv_prep_math(p, p_prev, mu, w0, w2p, a0, a2p, g2, kkp, kap, rk):
    ps = p + (p_prev - p) * mu
    r = ps[:, 0:RW_DIM]
    k = ps[:, RW_DIM:2 * RW_DIM]
    v = ps[:, 2 * RW_DIM:3 * RW_DIM]
    wa = ps[:, 3 * RW_DIM:3 * RW_DIM + W_LORA + A_LORA]
    gl = ps[:, 3 * RW_DIM + W_LORA + A_LORA:RW_PROJ]
    w = -jax.nn.softplus(-(w0 + jnp.dot(jnp.tanh(wa).astype(BF16), w2p, preferred_element_type=F32))) - 0.5
    decay = jnp.exp(-jnp.exp(w))
    a = jax.nn.sigmoid(a0 + jnp.dot(wa.astype(BF16), a2p, preferred_element_type=F32))
    g = jnp.dot(jax.nn.sigmoid(gl).astype(BF16), g2, preferred_element_type=F32)
    j = _seg_ones()
    kk = k * kkp
    kk = kk / jnp.maximum(jnp.sqrt(_seg64(kk * kk, j)), 1e-12)
    k2 = k * (1.0 + (a - 1.0) * kap)
    bonus = _seg64(r * k2 * rk, j) * v
    return r, decay, k2, v, -kk, kk * a, g, bonus


def _rwkv_prep_carry_kernel(p_ref, mu, w0, w2p, a0, a2p, g2, kkp, kap, rk, *rest):
    outs, carry = rest[:-1], rest[-1]

    @pl.when(pl.program_id(1) == 0)
    def _():
        carry[...] = jnp.zeros_like(carry)

    p = p_ref[...]
    rows = lax.broadcasted_iota(jnp.int32, p.shape, 0)
    p_prev = jnp.where(rows == 0, carry[...], pltpu.roll(p, 1, 0))
    carry[...] = p[p.shape[0] - 1:, :]
    res = _rwkv_prep_math(p, p_prev, mu[...], w0[...], w2p[...], a0[...], a2p[...], g2[...],
                          kkp[...], kap[...], rk[...])
    for o, val in zip(outs, res):
        o[...] = val


def _rwkv_prep_given_kernel(p_ref, pp_ref, mu, w0, w2p, a0, a2p, g2, kkp, kap, rk, *outs):
    res = _rwkv_prep_math(p_ref[...], pp_ref[...], mu[...], w0[...], w2p[...], a0[...], a2p[...],
                          g2[...], kkp[...], kap[...], rk[...])
    for o, val in zip(outs, res):
        o[...] = val


def _rwkv_prep_prompt(p3, params, tm):
    b, tp, _ = p3.shape
    assert tp % tm == 0
    full = lambda a: pl.BlockSpec(a.shape, lambda i, j: (0,) * a.ndim)
    return pl.pallas_call(
        _rwkv_prep_carry_kernel,
        grid=(b, tp // tm),
        in_specs=[pl.BlockSpec((None, tm, RW_PROJ), lambda i, j: (i, j, 0))] + [full(a) for a in params],
        out_specs=[pl.BlockSpec((None, tm, RW_DIM), lambda i, j: (i, j, 0))] * 8,
        out_shape=[jax.ShapeDtypeStruct((b, tp, RW_DIM), F32)] * 8,
        scratch_shapes=[pltpu.VMEM((1, RW_PROJ), F32)],
        compiler_params=_cparams(("arbitrary", "arbitrary")),
        name="rwkv_prep_prompt",
    )(p3, *params)


def _rwkv_prep_sample(p, p_prev, params):
    n = p.shape[0]
    full = lambda a: pl.BlockSpec(a.shape, lambda i: (0,) * a.ndim)
    return pl.pallas_call(
        _rwkv_prep_given_kernel,
        grid=(1,),
        in_specs=[full(p), full(p_prev)] + [full(a) for a in params],
        out_specs=[pl.BlockSpec((n, RW_DIM), lambda i: (0, 0))] * 8,
        out_shape=[jax.ShapeDtypeStruct((n, RW_DIM), F32)] * 8,
        compiler_params=_cparams(("arbitrary",)),
        name="rwkv_prep_sample",
    )(p, p_prev, *params)


def _scan_consts():
    lane = lax.broadcasted_iota(jnp.int32, (RW_HEAD, LANES), 1)
    sub = lax.broadcasted_iota(jnp.int32, (RW_HEAD, LANES), 0)
    first = lane < RW_HEAD
    eye_lo = (lane == sub).astype(F32)
    eye_hi = (lane - RW_HEAD == sub).astype(F32)
    return lane, first, eye_lo, eye_hi


def _seg(first, x_lo, x_hi):
    lo = jnp.sum(x_lo, axis=-1, keepdims=True)
    hi = jnp.sum(x_hi, axis=-1, keepdims=True)
    return jnp.where(first, lo, hi)


def _split_row(first, row):
    lo = jnp.where(first[0:1, :], row, 0.0)
    return lo, row - lo


def _scan_step_kernel(r_ref, w_ref, k_ref, v_ref, a_ref, b_ref, s0_ref, y_ref, s_ref, *, nb):
    _, first, eye_lo, eye_hi = _scan_consts()
    for b in range(nb):
        for p in range(RW_DIM // LANES):
            cols = pl.ds(p * LANES, LANES)
            r_t, w_t, k_t, v_t, a_t, b_t = [ref[pl.ds(b, 1), cols]
                                            for ref in (r_ref, w_ref, k_ref, v_ref, a_ref, b_ref)]
            s = s0_ref[b, :, cols]
            a_lo, a_hi = _split_row(first, a_t)
            r_lo, r_hi = _split_row(first, r_t)
            sa = _seg(first, s * a_lo, s * a_hi)
            vcol = _seg(first, eye_lo * v_t, eye_hi * v_t)
            s = s * w_t + sa * b_t + vcol * k_t
            s_ref[b, :, cols] = s
            ycol = _seg(first, s * r_lo, s * r_hi)
            y_ref[pl.ds(b, 1), cols] = jnp.sum((eye_lo + eye_hi) * ycol, axis=0, keepdims=True)


def _rwkv_step(r, w, k, v, a, b, s0):
    nb = r.shape[0]
    full = lambda x: pl.BlockSpec(x.shape, lambda i: (0,) * x.ndim)
    return pl.pallas_call(
        functools.partial(_scan_step_kernel, nb=nb),
        grid=(1,),
        in_specs=[full(r)] * 6 + [full(s0)],
        out_specs=[full(r), full(s0)],
        out_shape=[jax.ShapeDtypeStruct(r.shape, F32), jax.ShapeDtypeStruct(s0.shape, F32)],
        compiler_params=_cparams(("arbitrary",)),
        name="rwkv_step",
    )(r, w, k, v, a, b, s0)


def _scan_kernel(r_ref, w_ref, k_ref, v_ref, a_ref, b_ref, s0_ref, y_ref, sfin_ref,
                 s_scr, sa_scr, vc_scr, yacc_scr, vstage, zstage, yres, *, nb):
    npair = RW_DIM // LANES
    chains = [(b, p) for b in range(nb) for p in range(npair)]
    lane, first, eye_lo, eye_hi = _scan_consts()
    sub_t = 8

    @pl.when(pl.program_id(0) == 0)
    def _():
        for ci, (b, p) in enumerate(chains):
            s_scr[ci] = s0_ref[b, :, p * LANES:(p + 1) * LANES]

    yacc_scr[...] = jnp.zeros_like(yacc_scr)
    nch = len(chains)
    eye2 = eye_lo + eye_hi
    seg_ones = _seg_ones()
    zero_b = jnp.zeros((LANES, LANES), BF16)
    j_stack = jnp.concatenate([seg_ones, seg_ones], axis=0)
    j_pair = jnp.concatenate([jnp.concatenate([seg_ones, zero_b], axis=1),
                              jnp.concatenate([zero_b, seg_ones], axis=1)], axis=0)

    def prepare(ci, s, a_t):
        a_lo, a_hi = _split_row(first, a_t)
        sa_scr[ci] = _seg(first, s * a_lo, s * a_hi)

    def group(tg, carry):
        rows = pl.ds(pl.multiple_of(tg * sub_t, sub_t), sub_t)

        def row(ref, ci, i):
            b, p = chains[ci]
            return ref[b, rows, pl.ds(p * LANES, LANES)][i:i + 1, :]

        for i in range(sub_t):
            for ci in range(nch):
                v_t = row(v_ref, ci, i)
                v_hi = v_t.astype(BF16).astype(F32)
                lhs = jnp.concatenate([eye2 * v_hi, eye2 * (v_t - v_hi)], axis=1)
                vstage[pl.ds((i * nch + ci) * RW_HEAD, RW_HEAD), :] = lhs.astype(BF16)
        vc_scr[...] = jnp.dot(vstage[...], j_stack, preferred_element_type=F32).reshape(vc_scr.shape)

        for ci in range(nch):
            prepare(ci, s_scr[ci], row(a_ref, ci, 0))
        for i in range(sub_t):
            for ci in range(nch):
                s = s_scr[ci] * row(w_ref, ci, i) + sa_scr[ci] * row(b_ref, ci, i) \
                    + vc_scr[i * nch + ci] * row(k_ref, ci, i)
                s_scr[ci] = s
                if i + 1 < sub_t:
                    prepare(ci, s, row(a_ref, ci, i + 1))
                z = (s * row(r_ref, ci, i)).astype(BF16)
                zstage[pl.ds((i * (nch // 2) + ci // 2) * RW_HEAD, RW_HEAD),
                       pl.ds((ci % 2) * LANES, LANES)] = z
        yres[...] = jnp.dot(zstage[...], j_pair, preferred_element_type=F32)
        for i in range(sub_t):
            place = (lane % RW_HEAD) == tg * sub_t + i
            for ci in range(nch):
                ycol = yres[pl.ds((i * (nch // 2) + ci // 2) * RW_HEAD, RW_HEAD),
                            pl.ds((ci % 2) * LANES, LANES)]
                yacc_scr[ci] = jnp.where(place, ycol, yacc_scr[ci])
        return carry

    lax.fori_loop(0, SCAN_CHUNK // sub_t, group, 0)

    zeros = jnp.zeros((LANES - RW_HEAD, LANES), F32)
    for ci, (b, p) in enumerate(chains):
        cols = pl.ds(p * LANES, LANES)
        tr = jnp.concatenate([yacc_scr[ci], zeros], axis=0).T
        y_ref[b, :, cols] = jnp.where(first, tr[0:RW_HEAD], pltpu.roll(tr[RW_HEAD:], RW_HEAD, 1))
        sfin_ref[b, :, cols] = s_scr[ci]


def _rwkv_scan(r, w, k, v, a, b, s0):
    nb, tp, _ = r.shape
    assert tp % SCAN_CHUNK == 0 and SCAN_CHUNK == RW_HEAD
    nch = nb * (RW_DIM // LANES)
    seq = pl.BlockSpec((nb, SCAN_CHUNK, RW_DIM), lambda i: (0, i, 0))
    st = pl.BlockSpec((nb, RW_HEAD, RW_DIM), lambda i: (0, 0, 0))
    return pl.pallas_call(
        functools.partial(_scan_kernel, nb=nb),
        grid=(tp // SCAN_CHUNK,),
        in_specs=[seq] * 6 + [st],
        out_specs=[seq, st],
        out_shape=[jax.ShapeDtypeStruct((nb, tp, RW_DIM), F32),
                   jax.ShapeDtypeStruct((nb, RW_HEAD, RW_DIM), F32)],
        scratch_shapes=[pltpu.VMEM((nch, RW_HEAD, LANES), F32),
                        pltpu.VMEM((nch, RW_HEAD, LANES), F32),
                        pltpu.VMEM((8 * nch, RW_HEAD, LANES), F32),
                        pltpu.VMEM((nch, RW_HEAD, LANES), F32),
                        pltpu.VMEM((8 * nch * RW_HEAD, 2 * LANES), BF16),
                        pltpu.VMEM((4 * nch * RW_HEAD, 2 * LANES), BF16),
                        pltpu.VMEM((4 * nch * RW_HEAD, 2 * LANES), F32)],
        compiler_params=_cparams(("arbitrary",)),
        name="rwkv_scan",
    )(r, w, k, v, a, b, s0)


def _lambda_value(lam_ref):
    lv = lam_ref[...]
    s01 = jnp.sum(lv[0:1, :] * lv[1:2, :], axis=-1, keepdims=True)
    s23 = jnp.sum(lv[2:3, :] * lv[3:4, :], axis=-1, keepdims=True)
    return jnp.exp(s01) - jnp.exp(s23) + LAM_INIT


def _alibi_slope(h):
    slope = jnp.float32(2.0 ** (-8.0 * DA_HEADS / DA_HEADS))
    for i in range(DA_HEADS - 2, -1, -1):
        slope = jnp.where(h == i, jnp.float32(2.0 ** (-8.0 * (i + 1) / DA_HEADS)), slope)
    return slope


def _sub_norm(o, g):
    return o * lax.rsqrt(jnp.mean(o * o, axis=-1, keepdims=True) + NORM_EPS) * g * (1.0 - LAM_INIT)


ATT_ONES = 16


def _attn_prompt_kernel(q_ref, k_ref, v_ref, d0_ref, lam_ref, sg_ref, o_ref,
                        qc_ref, vt_ref, bias_ref, m_ref, acc_ref, s_ref, *, blk):
    h = pl.program_id(1)
    qi = pl.program_id(2)
    slope = _alibi_slope(h)
    tp = k_ref.shape[0]
    sub = blk // LANES

    @pl.when(qi == 0)
    def _():
        for jb in range(tp // LANES):
            vt = v_ref[jb * LANES:(jb + 1) * LANES, :].T.astype(BF16)
            vt_ref[jb // sub, 0:DA_DV, (jb % sub) * LANES:(jb % sub + 1) * LANES] = vt
        vt_ref[:, DA_DV:, :] = jnp.ones((tp // blk, ATT_ONES, blk), BF16)

    lane = lax.broadcasted_iota(jnp.int32, (blk, LANES), 1)
    qb = q_ref[...].astype(BF16)
    for c in range(2):
        qc_ref[c] = jnp.where((lane // DA_DK) == c, qb, jnp.zeros_like(qb))
    bias_ref[...] = slope * d0_ref[...]
    m_ref[...] = jnp.full_like(m_ref, NEG_BIG)
    acc_ref[...] = jnp.zeros_like(acc_ref)

    def scores(j):
        kb = k_ref[pl.ds(pl.multiple_of(j * blk, blk), blk), :].astype(BF16)
        for c in range(2):
            s_ref[(j % 2) * 2 + c] = lax.dot_general(kb, qc_ref[c], (((1,), (1,)), ((), ())),
                                                     preferred_element_type=F32)

    def consume(j, masked):
        off = ((qi - j) * blk).astype(F32)
        c0 = slope * off
        if masked:
            kpos = j * blk + lax.broadcasted_iota(jnp.int32, (blk, blk), 0)
            valid = (d0_ref[...] + off >= 0.0) & (kpos >= PAD_LEAD)
        for c in range(2):
            t = s_ref[(j % 2) * 2 + c] - bias_ref[...]
            if masked:
                t = jnp.where(valid, t, NEG_BIG)
            m_old = m_ref[c]
            m_new = jnp.maximum(m_old, jnp.max(t, axis=0, keepdims=True) - c0)
            alpha = jnp.exp(m_old - m_new)
            p = jnp.exp(t - (m_new + c0)).astype(BF16)
            acc_ref[c] = alpha * acc_ref[c] + jnp.dot(vt_ref[j], p, preferred_element_type=F32)
            m_ref[c] = m_new

    def middle(j, carry):
        consume(j, False)
        scores(j + 1)
        return carry

    scores(0)
    consume(0, True)
    scores(jnp.minimum(1, tp // blk - 1))
    lax.fori_loop(1, qi, middle, 0)

    @pl.when(qi > 0)
    def _():
        consume(qi, True)

    lam = _lambda_value(lam_ref)
    a0 = acc_ref[0]
    a1 = acc_ref[1]
    ot = a0[:DA_DV] / a0[DA_DV:DA_DV + 1] - lam * (a1[:DA_DV] / a1[DA_DV:DA_DV + 1])
    o_ref[...] = _sub_norm(ot.T, sg_ref[...])


def _attn_prompt(q, k, v, da_lam, subln_g, blk):
    b, tp, _ = q.shape
    assert tp % blk == 0 and blk % LANES == 0 and PAD_LEAD <= blk
    d0t = (jnp.arange(blk, dtype=F32)[None, :] - jnp.arange(blk, dtype=F32)[:, None])
    qspec = pl.BlockSpec((None, blk, DA_DV), lambda bi, h, i: (bi, i, h))
    kspec = pl.BlockSpec((None, tp, DA_DV), lambda bi, h, i: (bi, 0, h))
    full = lambda a: pl.BlockSpec(a.shape, lambda bi, h, i: (0,) * a.ndim)
    return pl.pallas_call(
        functools.partial(_attn_prompt_kernel, blk=blk),
        grid=(b, DA_HEADS, tp // blk),
        in_specs=[qspec, kspec, kspec, full(d0t), full(da_lam), full(subln_g)],
        out_specs=qspec,
        scratch_shapes=[pltpu.VMEM((2, blk, DA_DV), BF16),
                        pltpu.VMEM((tp // blk, DA_DV + ATT_ONES, blk), BF16),
                        pltpu.VMEM((blk, blk), F32), pltpu.VMEM((2, 1, blk), F32),
                        pltpu.VMEM((2, DA_DV + ATT_ONES, blk), F32),
                        pltpu.VMEM((4, blk, blk), F32)],
        out_shape=jax.ShapeDtypeStruct((b, tp, DA_V), F32),
        compiler_params=_cparams(("parallel", "parallel", "arbitrary")),
        name="attn_prompt",
    )(q, k, v, d0t, da_lam, subln_g)


def _attn_sample_kernel(pt_ref, q_ref, kn_ref, vn_ref, lam_ref, sg_ref, *rest, ppg, page, past):
    k_refs = rest[:ppg]
    v_refs = rest[ppg:2 * ppg]
    o_ref, m_ref, l_ref, acc_ref = rest[2 * ppg:]
    g = pl.program_id(1)
    nrow = 2 * DA_HEADS
    rowi = lax.broadcasted_iota(jnp.int32, (nrow, DA_QK), 0)
    lanei = lax.broadcasted_iota(jnp.int32, (nrow, DA_QK), 1)
    qsel = (lanei // DA_DK) == rowi
    qmat = jnp.where(qsel, q_ref[...], 0.0)
    rh = lax.broadcasted_iota(jnp.int32, (nrow, 1), 0) // 2
    slope = jnp.full((nrow, 1), 2.0 ** (-8.0), F32)
    for i in range(DA_HEADS - 2, -1, -1):
        slope = jnp.where(rh == i, jnp.float32(2.0 ** (-8.0 * (i + 1) / DA_HEADS)), slope)

    @pl.when(g == 0)
    def _():
        m_ref[...] = jnp.full_like(m_ref, NEG_BIG)
        l_ref[...] = jnp.zeros_like(l_ref)
        acc_ref[...] = jnp.zeros_like(acc_ref)

    qb = qmat.astype(BF16)
    kpos = g * (ppg * page) + lax.broadcasted_iota(jnp.int32, (nrow, ppg * page), 1)
    s = jnp.concatenate([jnp.dot(qb, k_refs[i][...].astype(BF16), preferred_element_type=F32)
                         for i in range(ppg)], axis=1)
    s = s - slope * (past - kpos).astype(F32)
    m_old = m_ref[...]
    m_new = jnp.maximum(m_old, jnp.max(s, axis=-1, keepdims=True))
    alpha = jnp.exp(m_old - m_new)
    p = jnp.exp(s - m_new)
    l_ref[...] = alpha * l_ref[...] + jnp.sum(p, axis=-1, keepdims=True)
    pb = p.astype(BF16)
    pv = [jnp.zeros((nrow, DA_DV), F32)] * DA_HEADS
    for i in range(ppg):
        for hh in range(DA_HEADS):
            vh = v_refs[i][pl.ds(hh, page, stride=DA_HEADS), :].astype(BF16)
            pv[hh] = pv[hh] + jnp.dot(pb[:, i * page:(i + 1) * page], vh, preferred_element_type=F32)
    tot = pv[DA_HEADS - 1]
    for hh in range(DA_HEADS - 1):
        tot = jnp.where(rh == hh, pv[hh], tot)
    acc_ref[...] = alpha * acc_ref[...] + tot
    m_ref[...] = m_new

    @pl.when(g == pl.num_programs(1) - 1)
    def _():
        kn = kn_ref[...].astype(BF16).astype(F32)
        s_self = jnp.sum(qb.astype(F32) * kn, axis=-1, keepdims=True)
        m_old = m_ref[...]
        m_new = jnp.maximum(m_old, s_self)
        alpha = jnp.exp(m_old - m_new)
        p_self = jnp.exp(s_self - m_new)
        l_fin = alpha * l_ref[...] + p_self
        vn = vn_ref[...].astype(BF16).astype(F32)
        vrow = jnp.zeros((nrow, DA_DV), F32)
        for hh in range(DA_HEADS):
            vrow = jnp.where(rh == hh, vn[:, hh * DA_DV:(hh + 1) * DA_DV], vrow)
        acc = alpha * acc_ref[...] + p_self.astype(BF16).astype(F32) * vrow
        lam = _lambda_value(lam_ref)
        ri = lax.broadcasted_iota(jnp.int32, (nrow, 1), 0)
        t = acc / l_fin * jnp.where(ri % 2 == 0, 1.0, -lam)
        outs = []
        for hh in range(DA_HEADS):
            o = t[2 * hh:2 * hh + 1, :] + t[2 * hh + 1:2 * hh + 2, :]
            outs.append(_sub_norm(o, sg_ref[...]))
        o_ref[...] = jnp.concatenate(outs, axis=1)


def _attn_sample(q, kn, vn, cache_kt, cache_v, page_table, da_lam, subln_g, ppg):
    bd = q.shape[0]
    n_pages = page_table.shape[1]
    page = cache_kt.shape[2]
    assert n_pages % ppg == 0 and cache_v.shape[1:] == (page * DA_HEADS, DA_DV)
    pt = page_table.reshape(-1)
    one = pl.BlockSpec((None, 1, DA_QK), lambda b, g, pt: (b, 0, 0))
    full = lambda a: pl.BlockSpec(a.shape, lambda b, g, pt: (0,) * a.ndim)

    def pspec(i):
        return pl.BlockSpec((None, DA_QK, page),
                            lambda b, g, pt, i=i: (pt[b * n_pages + g * ppg + i], 0, 0))

    out = pl.pallas_call(
        functools.partial(_attn_sample_kernel, ppg=ppg, page=page, past=n_pages * page),
        grid_spec=pltpu.PrefetchScalarGridSpec(
            num_scalar_prefetch=1,
            grid=(bd, n_pages // ppg),
            in_specs=[one, one, one, full(da_lam), full(subln_g)]
                     + [pspec(i) for i in range(ppg)] * 2,
            out_specs=one,
            scratch_shapes=[pltpu.VMEM((2 * DA_HEADS, 1), F32), pltpu.VMEM((2 * DA_HEADS, 1), F32),
                            pltpu.VMEM((2 * DA_HEADS, DA_DV), F32)],
        ),
        out_shape=jax.ShapeDtypeStruct((bd, 1, DA_V), F32),
        compiler_params=_cparams(("parallel", "arbitrary")),
        name="attn_sample",
    )(pt, q.reshape(bd, 1, DA_QK), kn.reshape(bd, 1, DA_QK), vn.reshape(bd, 1, DA_V), da_lam, subln_g,
      *([cache_kt] * ppg), *([cache_v] * ppg))
    return out.reshape(bd, DA_V)


def _merge_kernel(*refs, nsub):
    rows = [refs[k * nsub:(k + 1) * nsub] for k in range(6)]
    (lng_ref, lnb_ref, wrw_ref, wda_ref, wout_ref, n2_ref, wq_ref, sk_ref,
     h2_ref, xn_ref, st_ref) = refs[6 * nsub:]
    stack = lambda rs: rs[0][...] if nsub == 1 else jnp.concatenate([r[...] for r in rs], axis=0)
    y, bonus, g, o, gt, h = [stack(rs) for rs in rows]
    j = _seg_ones()
    mu = _seg64(y, j) * (1.0 / RW_HEAD)
    d = y - mu
    var = _seg64(d * d, j) * (1.0 / RW_HEAD)
    yln = d * lax.rsqrt(var + RW_LN_EPS) * lng_ref[...] + lnb_ref[...]
    yr = ((yln + bonus) * g).astype(BF16)
    y_rw = jnp.dot(yr, wrw_ref[...], preferred_element_type=F32)
    y_da = jnp.dot(o.astype(BF16), wda_ref[...], preferred_element_type=F32)
    mix = gt[:, :D_MODEL] * y_rw + gt[:, D_MODEL:] * y_da
    h2 = h + jnp.dot(mix.astype(BF16), wout_ref[...], preferred_element_type=F32)
    h2_ref[...] = h2
    ms = jnp.mean(h2 * h2, axis=-1, keepdims=True)
    xn = (h2 * lax.rsqrt(ms + NORM_EPS) * n2_ref[...]).astype(BF16)
    xn_ref[...] = xn
    q = jnp.dot(xn, wq_ref[...], preferred_element_type=F32)
    for hc in range(2 * PEER_HEADS):
        qc = q[:, hc * PEER_DHALF:(hc + 1) * PEER_DHALF].astype(BF16)
        st_ref[hc * N_KEYS:(hc + 1) * N_KEYS, :] = lax.dot_general(
            sk_ref[hc % 2], qc, (((1,), (1,)), ((), ())), preferred_element_type=F32)


def _merge(y, bonus, g, o, gates, h, consts, tm, tile_off, nsub):
    b, rows, _ = y.shape
    assert rows % tm == 0 and (rows // tm - tile_off) % nsub == 0
    nt = (rows // tm - tile_off) // nsub
    full = lambda a: pl.BlockSpec(a.shape, lambda i, t: (0,) * a.ndim)

    def rin(c):
        return [pl.BlockSpec((None, tm, c), lambda i, t, k=k: (i, t * nsub + k + tile_off, 0))
                for k in range(nsub)]

    row_arrays = (y, bonus, g, o, gates, h)
    tmo = tm * nsub
    ntok = b * nt * tmo
    return pl.pallas_call(
        functools.partial(_merge_kernel, nsub=nsub),
        grid=(b, nt),
        in_specs=[s for a in row_arrays for s in rin(a.shape[-1])] + [full(a) for a in consts],
        out_specs=[pl.BlockSpec((tmo, D_MODEL), lambda i, t: (i * nt + t, 0)),
                   pl.BlockSpec((tmo, D_MODEL), lambda i, t: (i * nt + t, 0)),
                   pl.BlockSpec((2 * PEER_HEADS * N_KEYS, tmo), lambda i, t: (0, i * nt + t))],
        out_shape=[jax.ShapeDtypeStruct((ntok, D_MODEL), F32),
                   jax.ShapeDtypeStruct((ntok, D_MODEL), BF16),
                   jax.ShapeDtypeStruct((2 * PEER_HEADS * N_KEYS, ntok), F32)],
        compiler_params=_cparams(("parallel", "parallel")),
        name="merge",
    )(*[a for a in row_arrays for _ in range(nsub)], *consts)


def _count(mask):
    return jnp.sum(jnp.where(mask, 1.0, 0.0), axis=0, keepdims=True)


def _top16(x, exact):
    nk = x.shape[0]
    iota = lax.broadcasted_iota(jnp.int32, x.shape, 0).astype(F32)
    rank = jnp.full(x.shape, float(nk), F32)
    vals = []
    for r in range(PEER_TOPK):
        m = jnp.max(x, axis=0, keepdims=True)
        sel = x == m
        if exact:
            cand = jnp.where(sel, iota, float(nk))
            sel = cand == jnp.min(cand, axis=0, keepdims=True)
        x = jnp.where(sel, -jnp.inf, x)
        rank = jnp.where(sel, float(r), rank)
        vals.append(m)
    return vals, rank


def _peer_topk_kernel(st_ref, rank2_ref, c2_ref, cnt1_ref, c1_ref):
    n = st_ref.shape[1]
    rid = lax.broadcasted_iota(jnp.int32, (_CAND_ROWS, n), 0)
    flat = jnp.zeros((_CAND_ROWS, n), jnp.int32) + PEER_TOPK * PEER_TOPK
    irow = jnp.zeros((_CAND_ROWS, n), jnp.int32) + PEER_TOPK
    for ci, (i, j) in enumerate(_CAND):
        flat = jnp.where(rid == ci, i * PEER_TOPK + j, flat)
        irow = jnp.where(rid == ci, i, irow)
    outs = (rank2_ref, c2_ref, cnt1_ref, c1_ref)
    ties = _peer_topk_body(st_ref, outs, flat, irow, exact=False)

    @pl.when(jnp.max(ties) > 0.0)
    def _():
        _peer_topk_body(st_ref, outs, flat, irow, exact=True)


def _peer_topk_body(st_ref, outs, flat, irow, exact):
    rank2_ref, c2_ref, cnt1_ref, c1_ref = outs
    n = st_ref.shape[1]
    ties = jnp.zeros((1, n), F32)
    for h in range(PEER_HEADS):
        s1 = st_ref[(2 * h) * N_KEYS:(2 * h + 1) * N_KEYS, :]
        s2 = st_ref[(2 * h + 1) * N_KEYS:(2 * h + 2) * N_KEYS, :]
        v1, rank1 = _top16(s1, exact)
        v2, rank2 = _top16(s2, exact)
        rows = [v1[i] + v2[j] for (i, j) in _CAND]
        rows += [jnp.full((1, n), -jnp.inf, F32)] * (_CAND_ROWS - len(_CAND))
        cand = jnp.concatenate(rows, axis=0)
        top = v1[0] + v2[0]
        chosen = jnp.zeros(cand.shape, jnp.bool_)
        work = cand
        for _ in range(PEER_TOPK):
            m = jnp.max(work, axis=0, keepdims=True)
            sel = work == m
            if exact:
                fl = jnp.where(sel, flat, PEER_TOPK * PEER_TOPK + 1)
                sel = fl == jnp.min(fl, axis=0, keepdims=True)
            work = jnp.where(sel, -jnp.inf, work)
            chosen = chosen | sel
        if not exact:
            k = float(PEER_TOPK)
            bad = (_count(rank1 < k) != k) | (_count(rank2 < k) != k) | (_count(chosen) != k)
            ties = jnp.maximum(ties, jnp.where(bad, 1.0, 0.0))
        z = jnp.sum(jnp.where(chosen, jnp.exp(cand - top), 0.0), axis=0, keepdims=True)
        cnt1 = jnp.zeros(s1.shape, F32)
        for i in range(PEER_TOPK):
            m_i = jnp.sum(jnp.where(chosen & (irow == i), 1.0, 0.0), axis=0, keepdims=True)
            cnt1 = jnp.where(rank1 == float(i), m_i, cnt1)
        cnt1_ref[h] = cnt1
        c1_ref[h] = jnp.where(rank1 < float(PEER_TOPK), jnp.exp(s1 - v1[0]) / z, 0.0)
        c2_ref[h] = jnp.where(rank2 < float(PEER_TOPK), jnp.exp(s2 - v2[0]), 0.0).astype(BF16)
        rank2_ref[h] = rank2.astype(BF16)
    return ties


def _peer_topk(st, tn):
    ntok = st.shape[1]
    assert ntok % tn == 0
    spec = pl.BlockSpec((PEER_HEADS, N_KEYS, tn), lambda i: (0, 0, i))
    return pl.pallas_call(
        _peer_topk_kernel,
        grid=(ntok // tn,),
        in_specs=[pl.BlockSpec((st.shape[0], tn), lambda i: (0, i))],
        out_specs=[spec] * 4,
        out_shape=[jax.ShapeDtypeStruct((PEER_HEADS, N_KEYS, ntok), dt) for dt in (BF16, BF16, F32, F32)],
        compiler_params=_cparams(("parallel",)),
        name="peer_topk",
    )(st)


def _peer_dense_kernel(x_ref, u_ref, vt_ref, rank2_ref, c2_ref, cnt1_ref, c1_ref, h2_ref, y_ref,
                       acc_ref, hid_ref, p_ref, *, e1b):
    j = pl.program_id(1)
    tn = x_ref.shape[0]
    ngrp = N_KEYS // BF16_ROWS

    @pl.when(j == 0)
    def _():
        acc_ref[...] = jnp.zeros_like(acc_ref)

    hid_ref[...] = lax.dot_general(u_ref[...], x_ref[...], (((1,), (1,)), ((), ())),
                                   preferred_element_type=F32)

    for e in range(e1b):
        e1 = j * e1b + e
        gate = [None] * ngrp
        for h in range(PEER_HEADS):
            cnt = jnp.broadcast_to(cnt1_ref[h, pl.ds(e1, 1), :], (BF16_ROWS, tn)).astype(BF16)
            c1 = jnp.broadcast_to(c1_ref[h, pl.ds(e1, 1), :], (BF16_ROWS, tn)).astype(BF16)
            for g in range(ngrp):
                rows = slice(g * BF16_ROWS, (g + 1) * BF16_ROWS)
                c2 = c2_ref[h, rows, :]
                term = jnp.where(rank2_ref[h, rows, :] < cnt, c2, jnp.zeros_like(c2)) * c1
                gate[g] = term if gate[g] is None else gate[g] + term
        hh = hid_ref[e * N_KEYS:(e + 1) * N_KEYS, :]
        gelu = (0.5 * hh * (1.0 + lax.erf(hh * (2.0 ** -0.5)))).astype(BF16)
        for g in range(ngrp):
            r0 = e * N_KEYS + g * BF16_ROWS
            p_ref[r0:r0 + BF16_ROWS, :] = gelu[g * BF16_ROWS:(g + 1) * BF16_ROWS] * gate[g]
    acc_ref[...] += jnp.dot(vt_ref[...], p_ref[...], preferred_element_type=F32)

    @pl.when(j == pl.num_programs(1) - 1)
    def _():
        y_ref[...] = h2_ref[...] + acc_ref[...].T


def _peer_dense(xn, u_bf, vt_bf, rank2, c2, cnt1, c1, h2, tn, e1b):
    ntok = xn.shape[0]
    assert ntok % tn == 0 and N_KEYS % e1b == 0
    tok = pl.BlockSpec((PEER_HEADS, N_KEYS, tn), lambda i, j: (0, 0, i))
    row = pl.BlockSpec((tn, D_MODEL), lambda i, j: (i, 0))
    return pl.pallas_call(
        functools.partial(_peer_dense_kernel, e1b=e1b),
        grid=(ntok // tn, N_KEYS // e1b),
        in_specs=[row,
                  pl.BlockSpec((e1b * N_KEYS, D_MODEL), lambda i, j: (j, 0)),
                  pl.BlockSpec((D_MODEL, e1b * N_KEYS), lambda i, j: (0, j)),
                  tok, tok, tok, tok, row],
        out_specs=row,
        out_shape=jax.ShapeDtypeStruct((ntok, D_MODEL), F32),
        scratch_shapes=[pltpu.VMEM((D_MODEL, tn), F32), pltpu.VMEM((e1b * N_KEYS, tn), F32),
                        pltpu.VMEM((e1b * N_KEYS, tn), BF16)],
        compiler_params=_cparams(("parallel", "arbitrary")),
        name="peer_dense",
    )(xn, u_bf, vt_bf, rank2, c2, cnt1, c1, h2)


def _state_in(s):
    b = s.shape[0]
    return jnp.transpose(s, (0, 2, 1, 3)).reshape(b, RW_HEAD, RW_DIM)


def _state_out(s):
    b = s.shape[0]
    return jnp.transpose(s.reshape(b, RW_HEAD, RW_HEADS, RW_HEAD), (0, 2, 1, 3))


def _pad_tokens(a, n, axis):
    pad = [(0, 0)] * a.ndim
    pad[axis] = (0, n - a.shape[axis])
    return jnp.pad(a, pad)


def kernel(x_prompt, x_sample, cache_k, cache_v, state_wkv, state_shift, page_table, meta, norm1_g, w_in,
           rw_mu, rw_w0, rw_w2, rw_a0, rw_a2, rw_g2, rw_kk, rw_ka, rw_rk, rw_ln_g, rw_ln_b, w_rw_br,
           da_qn_g, da_kn_g, da_lam, da_subln_g, w_da_br, w_out, norm2_g, peer_wq, peer_subkeys,
           peer_u, peer_v):
    b, seq, _ = x_prompt.shape
    bd = x_sample.shape[0]
    t_real = N_META + seq
    tp = PAD_LEAD + t_real
    assert tp % ROW_TILE == 0 and tp % SCAN_CHUNK == 0 and x_sample.shape[1] == 1
    row2 = lambda a: a.reshape(1, -1)

    w_in_bf = w_in[0].astype(BF16)
    g1 = row2(norm1_g[0])
    qg = row2(jnp.tile(da_qn_g[0], DA_QK // DA_DK))
    kg = row2(jnp.tile(da_kn_g[0], DA_QK // DA_DK))
    zlo = jnp.zeros((W_LORA, RW_DIM), F32)
    prep_params = (row2(rw_mu[0]), row2(rw_w0[0]),
                   jnp.concatenate([rw_w2[0], zlo], axis=0).astype(BF16), row2(rw_a0[0]),
                   jnp.concatenate([zlo, rw_a2[0]], axis=0).astype(BF16), rw_g2[0].astype(BF16),
                   row2(rw_kk[0]), row2(rw_ka[0]), row2(rw_rk[0]))
    merge_consts = (row2(rw_ln_g[0]), row2(rw_ln_b[0]), w_rw_br[0].astype(BF16), w_da_br[0].astype(BF16),
                    w_out[0].astype(BF16), row2(norm2_g[0]), peer_wq[0].astype(BF16),
                    peer_subkeys[0].astype(BF16))
    subln = row2(da_subln_g[0])
    u_bf = peer_u[0].astype(BF16)
    vt_bf = peer_v[0].astype(BF16).T

    hp = jnp.concatenate([jnp.zeros((b, PAD_LEAD, D_MODEL), F32),
                          jnp.broadcast_to(meta[None], (b, N_META, D_MODEL)), x_prompt], axis=1)
    prw, q, k, v, gates = _in_proj(hp.reshape(b * tp, D_MODEL), g1, w_in_bf, qg, kg, 256)
    r3 = lambda a: a.reshape(b, tp, -1)
    pr, pw, pk, pv, pa, pb, pg, pbonus = _rwkv_prep_prompt(r3(prw), prep_params, 384)
    y_scan, s_fin = _rwkv_scan(pr, pw, pk, pv, pa, pb, jnp.zeros((b, RW_HEAD, RW_DIM), F32))
    o_attn = _attn_prompt(r3(q), r3(k), r3(v), da_lam[0], subln, 384)
    h2, xn2, st = _merge(y_scan, pbonus, pg, o_attn, r3(gates), hp, merge_consts, ROW_TILE, 1, 2)
    y_prompt = _peer_dense(xn2, u_bf, vt_bf, *_peer_topk(st, 128), h2, 512, 16)

    xs = x_sample.reshape(bd, D_MODEL)
    prw_s, q_s, k_s, v_s, gates_s = _in_proj(xs, g1, w_in_bf, qg, kg, bd)
    sr, sw, sk, sv, sa, sb, sg, sbonus = _rwkv_prep_sample(prw_s, state_shift[0], prep_params)
    ys_scan, ss_fin = _rwkv_step(sr, sw, sk, sv, sa, sb, _state_in(state_wkv[0]))
    n_phys = cache_k.shape[1]
    ckt = jnp.transpose(cache_k[0].reshape(n_phys, -1, DA_QK), (0, 2, 1))
    cv = cache_v[0].reshape(n_phys, -1, DA_DV)
    o_s = _attn_sample(q_s, k_s, v_s, ckt, cv, page_table, da_lam[0], subln, 16)
    f3 = lambda a: a.reshape(1, bd, -1)
    h2s, xn2s, sts = _merge(ys_scan.reshape(1, bd, RW_DIM), f3(sbonus), f3(sg), f3(o_s), f3(gates_s),
                            f3(xs), merge_consts, bd, 0, 1)
    npad = LANES
    tk_s = _peer_topk(_pad_tokens(sts, npad, 1), npad)
    y_s = _peer_dense(_pad_tokens(xn2s, npad, 0), u_bf, vt_bf, *tk_s, _pad_tokens(h2s, npad, 0), npad, 8)

    y_prompt = y_prompt.reshape(b, seq, D_MODEL)
    y_sample = y_s[:bd].reshape(bd, 1, D_MODEL)
    k_p = r3(k)[:, PAD_LEAD:].reshape(1, b, t_real, DA_HEADS, 2, DA_DK)
    v_p = r3(v)[:, PAD_LEAD:].reshape(1, b, t_real, DA_HEADS, DA_DV)
    return (y_prompt, y_sample, k_p, v_p,
            k_s.reshape(1, bd, 1, DA_HEADS, 2, DA_DK), v_s.reshape(1, bd, 1, DA_HEADS, DA_DV),
            _state_out(s_fin)[None], _state_out(ss_fin)[None],
            r3(prw)[:, -1][None], prw_s[None])
```

```python
import functools
import math

import jax
import jax.numpy as jnp
from jax import lax
from jax.experimental import pallas as pl
from jax.experimental.pallas import tpu as pltpu

F32 = jnp.float32
BF16 = jnp.bfloat16

D_MODEL = 1024
N_META = 16
NORM_EPS = 1e-6
RW_HEAD = 64
RW_DIM = D_MODEL // 2
RW_HEADS = RW_DIM // RW_HEAD
W_LORA = 64
A_LORA = 64
G_LORA = 128
RW_PROJ = 3 * RW_DIM + W_LORA + A_LORA + G_LORA
RW_LN_EPS = 64e-5
DA_DK = 64
DA_DV = 2 * DA_DK
DA_DIM = D_MODEL // 2
DA_HEADS = DA_DIM // DA_DV
DA_QK = DA_HEADS * 2 * DA_DK
DA_V = DA_HEADS * DA_DV
N_KEYS = 128
PEER_HEADS = 8
PEER_TOPK = 16
PEER_DQ = 256
PEER_DHALF = PEER_DQ // 2
LAM_INIT = 0.8 - 0.6 * math.exp(-0.3 * 0)

LANES = 128
BF16_ROWS = 16
ROW_TILE = 128
PAD_LEAD = ROW_TILE - N_META
SCAN_CHUNK = 64
NEG_BIG = -1e30
VMEM_LIMIT = 56 * 1024 * 1024

O1 = RW_PROJ
O2 = O1 + DA_QK
O3 = O2 + DA_QK
O4 = O3 + DA_V
O5 = O4 + D_MODEL
O6 = O5 + D_MODEL

_CAND = [(i, j) for i in range(PEER_TOPK) for j in range(PEER_TOPK) if (i + 1) * (j + 1) <= PEER_TOPK]
_CAND_ROWS = -(-len(_CAND) // 8) * 8


def _cparams(sem):
    return pltpu.CompilerParams(dimension_semantics=sem, vmem_limit_bytes=VMEM_LIMIT)


def _split_dot(x, j):
    hi = x.astype(BF16)
    lo = (x - hi.astype(F32)).astype(BF16)
    return (jnp.dot(hi, j, preferred_element_type=F32)
            + jnp.dot(lo, j, preferred_element_type=F32))


def _seg64(x, j):
    outs = [_split_dot(x[:, c * LANES:(c + 1) * LANES], j) for c in range(x.shape[1] // LANES)]
    return outs[0] if len(outs) == 1 else jnp.concatenate(outs, axis=1)


def _seg_ones():
    r = lax.broadcasted_iota(jnp.int32, (LANES, LANES), 0) // 64
    c = lax.broadcasted_iota(jnp.int32, (LANES, LANES), 1) // 64
    return (r == c).astype(BF16)


def _in_proj_kernel(x_ref, g1_ref, w_ref, qg_ref, kg_ref, prw_ref, q_ref, k_ref, v_ref, gt_ref):
    x = x_ref[...]
    ms = jnp.mean(x * x, axis=-1, keepdims=True)
    xn = (x * lax.rsqrt(ms + NORM_EPS) * g1_ref[...]).astype(BF16)

    def mm(a, b):
        return jnp.dot(xn, w_ref[:, a:b], preferred_element_type=F32)

    j = _seg_ones()

    def head_norm(t, g):
        msq = _seg64(t * t, j) * (1.0 / DA_DK)
        return t * lax.rsqrt(msq + NORM_EPS) * g

    prw_ref[...] = mm(0, O1)
    q_ref[...] = head_norm(mm(O1, O2), qg_ref[...]) * (DA_DK ** -0.5)
    k_ref[...] = head_norm(mm(O2, O3), kg_ref[...])
    v_ref[...] = mm(O3, O4)
    gt_ref[...] = jax.nn.sigmoid(mm(O4, O6))


def _in_proj(x, g1, w_bf, qg, kg, tm):
    n = x.shape[0]
    assert n % tm == 0
    row = lambda c: pl.BlockSpec((tm, c), lambda i: (i, 0))
    full = lambda a: pl.BlockSpec(a.shape, lambda i: (0,) * a.ndim)
    widths = (RW_PROJ, DA_QK, DA_QK, DA_V, 2 * D_MODEL)
    return pl.pallas_call(
        _in_proj_kernel,
        grid=(n // tm,),
        in_specs=[row(D_MODEL), full(g1), full(w_bf), full(qg), full(kg)],
        out_specs=[row(c) for c in widths],
        out_shape=[jax.ShapeDtypeStruct((n, c), F32) for c in widths],
        compiler_params=_cparams(("parallel",)),
        name="in_proj",
    )(x, g1, w_bf, qg, kg)


def _rwkv_prep_math(p, p_prev, mu, w0, w2p, a0, a2p, g2, kkp, kap, rk):
    ps = p + (p_prev - p) * mu
    r = ps[:, 0:RW_DIM]
    k = ps[:, RW_DIM:2 * RW_DIM]
    v = ps[:, 2 * RW_DIM:3 * RW_DIM]
    wa = ps[:, 3 * RW_DIM:3 * RW_DIM + W_LORA + A_LORA]
    gl = ps[:, 3 * RW_DIM + W_LORA + A_LORA:RW_PROJ]
    w = -jax.nn.softplus(-(w0 + jnp.dot(jnp.tanh(wa).astype(BF16), w2p, preferred_element_type=F32))) - 0.5
    decay = jnp.exp(-jnp.exp(w))
    a = jax.nn.sigmoid(a0 + jnp.dot(wa.astype(BF16), a2p, preferred_element_type=F32))
    g = jnp.dot(jax.nn.sigmoid(gl).astype(BF16), g2, preferred_element_type=F32)
    j = _seg_ones()
    kk = k * kkp
    kk = kk / jnp.maximum(jnp.sqrt(_seg64(kk * kk, j)), 1e-12)
    k2 = k * (1.0 + (a - 1.0) * kap)
    bonus = _seg64(r * k2 * rk, j) * v
    return r, decay, k2, v, -kk, kk * a, g, bonus


def _rwkv_prep_carry_kernel(p_ref, mu, w0, w2p, a0, a2p, g2, kkp, kap, rk, *rest):
    outs, carry = rest[:-1], rest[-1]

    @pl.when(pl.program_id(1) == 0)
    def _():
        carry[...] = jnp.zeros_like(carry)

    p = p_ref[...]
    rows = lax.broadcasted_iota(jnp.int32, p.shape, 0)
    p_prev = jnp.where(rows == 0, carry[...], pltpu.roll(p, 1, 0))
    carry[...] = p[p.shape[0] - 1:, :]
    res = _rwkv_prep_math(p, p_prev, mu[...], w0[...], w2p[...], a0[...], a2p[...], g2[...],
                          kkp[...], kap[...], rk[...])
    for o, val in zip(outs, res):
        o[...] = val


def _rwkv_prep_given_kernel(p_ref, pp_ref, mu, w0, w2p, a0, a2p, g2, kkp, kap, rk, *outs):
    res = _rwkv_prep_math(p_ref[...], pp_ref[...], mu[...], w0[...], w2p[...], a0[...], a2p[...],
                          g2[...], kkp[...], kap[...], rk[...])
    for o, val in zip(outs, res):
        o[...] = val


def _rwkv_prep_prompt(p3, params, tm):
    b, tp, _ = p3.shape
    assert tp % tm == 0
    full = lambda a: pl.BlockSpec(a.shape, lambda i, j: (0,) * a.ndim)
    return pl.pallas_call(
        _rwkv_prep_carry_kernel,
        grid=(b, tp // tm),
        in_specs=[pl.BlockSpec((None, tm, RW_PROJ), lambda i, j: (i, j, 0))] + [full(a) for a in params],
        out_specs=[pl.BlockSpec((None, tm, RW_DIM), lambda i, j: (i, j, 0))] * 8,
        out_shape=[jax.ShapeDtypeStruct((b, tp, RW_DIM), F32)] * 8,
        scratch_shapes=[pltpu.VMEM((1, RW_PROJ), F32)],
        compiler_params=_cparams(("arbitrary", "arbitrary")),
        name="rwkv_prep_prompt",
    )(p3, *params)


def _rwkv_prep_sample(p, p_prev, params):
    n = p.shape[0]
    full = lambda a: pl.BlockSpec(a.shape, lambda i: (0,) * a.ndim)
    return pl.pallas_call(
        _rwkv_prep_given_kernel,
        grid=(1,),
        in_specs=[full(p), full(p_prev)] + [full(a) for a in params],
        out_specs=[pl.BlockSpec((n, RW_DIM), lambda i: (0, 0))] * 8,
        out_shape=[jax.ShapeDtypeStruct((n, RW_DIM), F32)] * 8,
        compiler_params=_cparams(("arbitrary",)),
        name="rwkv_prep_sample",
    )(p, p_prev, *params)


def _scan_consts():
    lane = lax.broadcasted_iota(jnp.int32, (RW_HEAD, LANES), 1)
    sub = lax.broadcasted_iota(jnp.int32, (RW_HEAD, LANES), 0)
    first = lane < RW_HEAD
    eye_lo = (lane == sub).astype(F32)
    eye_hi = (lane - RW_HEAD == sub).astype(F32)
    return lane, first, eye_lo, eye_hi


def _seg(first, x_lo, x_hi):
    lo = jnp.sum(x_lo, axis=-1, keepdims=True)
    hi = jnp.sum(x_hi, axis=-1, keepdims=True)
    return jnp.where(first, lo, hi)


def _split_row(first, row):
    lo = jnp.where(first[0:1, :], row, 0.0)
    return lo, row - lo


def _scan_step_kernel(r_ref, w_ref, k_ref, v_ref, a_ref, b_ref, s0_ref, y_ref, s_ref, *, nb):
    _, first, eye_lo, eye_hi = _scan_consts()
    for b in range(nb):
        for p in range(RW_DIM // LANES):
            cols = pl.ds(p * LANES, LANES)
            r_t, w_t, k_t, v_t, a_t, b_t = [ref[pl.ds(b, 1), cols]
                                            for ref in (r_ref, w_ref, k_ref, v_ref, a_ref, b_ref)]
            s = s0_ref[b, :, cols]
            a_lo, a_hi = _split_row(first, a_t)
            r_lo, r_hi = _split_row(first, r_t)
            sa = _seg(first, s * a_lo, s * a_hi)
            vcol = _seg(first, eye_lo * v_t, eye_hi * v_t)
            s = s * w_t + sa * b_t + vcol * k_t
            s_ref[b, :, cols] = s
            ycol = _seg(first, s * r_lo, s * r_hi)
            y_ref[pl.ds(b, 1), cols] = jnp.sum((eye_lo + eye_hi) * ycol, axis=0, keepdims=True)


def _rwkv_step(r, w, k, v, a, b, s0):
    nb = r.shape[0]
    full = lambda x: pl.BlockSpec(x.shape, lambda i: (0,) * x.ndim)
    return pl.pallas_call(
        functools.partial(_scan_step_kernel, nb=nb),
        grid=(1,),
        in_specs=[full(r)] * 6 + [full(s0)],
        out_specs=[full(r), full(s0)],
        out_shape=[jax.ShapeDtypeStruct(r.shape, F32), jax.ShapeDtypeStruct(s0.shape, F32)],
        compiler_params=_cparams(("arbitrary",)),
        name="rwkv_step",
    )(r, w, k, v, a, b, s0)


def _scan_kernel(r_ref, w_ref, k_ref, v_ref, a_ref, b_ref, s0_ref, y_ref, sfin_ref,
                 s_scr, sa_scr, vc_scr, yacc_scr, vstage, zstage, yres, *, nb, zero_chunks):
    npair = RW_DIM // LANES
    chains = [(b, p) for b in range(nb) for p in range(npair)]

    @pl.when(pl.program_id(0) == 0)
    def _():
        for ci, (b, p) in enumerate(chains):
            s_scr[ci] = s0_ref[b, :, p * LANES:(p + 1) * LANES]

    @pl.when(pl.program_id(0) < zero_chunks)
    def _():
        y_ref[...] = jnp.zeros_like(y_ref)
        sfin_ref[...] = s0_ref[...]

    @pl.when(pl.program_id(0) >= zero_chunks)
    def _():
        _scan_chunk(r_ref, w_ref, k_ref, v_ref, a_ref, b_ref, y_ref, sfin_ref,
                    s_scr, sa_scr, vc_scr, yacc_scr, vstage, zstage, yres, chains)


def _scan_chunk(r_ref, w_ref, k_ref, v_ref, a_ref, b_ref, y_ref, sfin_ref,
                s_scr, sa_scr, vc_scr, yacc_scr, vstage, zstage, yres, chains):
    lane, first, eye_lo, eye_hi = _scan_consts()
    sub_t = 8
    yacc_scr[...] = jnp.zeros_like(yacc_scr)
    nch = len(chains)
    eye2 = eye_lo + eye_hi
    seg_ones = _seg_ones()
    zero_b = jnp.zeros((LANES, LANES), BF16)
    j_stack = jnp.concatenate([seg_ones, seg_ones], axis=0)
    j_pair = jnp.concatenate([jnp.concatenate([seg_ones, zero_b], axis=1),
                              jnp.concatenate([zero_b, seg_ones], axis=1)], axis=0)

    def prepare(ci, s, a_t):
        a_lo, a_hi = _split_row(first, a_t)
        sa_scr[ci] = _seg(first, s * a_lo, s * a_hi)

    def group(tg, carry):
        rows = pl.ds(pl.multiple_of(tg * sub_t, sub_t), sub_t)

        def row(ref, ci, i):
            b, p = chains[ci]
            return ref[b, rows, pl.ds(p * LANES, LANES)][i:i + 1, :]

        for i in range(sub_t):
            for ci in range(nch):
                v_t = row(v_ref, ci, i)
                v_hi = v_t.astype(BF16).astype(F32)
                lhs = jnp.concatenate([eye2 * v_hi, eye2 * (v_t - v_hi)], axis=1)
                vstage[pl.ds((i * nch + ci) * RW_HEAD, RW_HEAD), :] = lhs.astype(BF16)
        vc_scr[...] = jnp.dot(vstage[...], j_stack, preferred_element_type=F32).reshape(vc_scr.shape)

        for ci in range(nch):
            prepare(ci, s_scr[ci], row(a_ref, ci, 0))
        for i in range(sub_t):
            for ci in range(nch):
                s = s_scr[ci] * row(w_ref, ci, i) + sa_scr[ci] * row(b_ref, ci, i) \
                    + vc_scr[i * nch + ci] * row(k_ref, ci, i)
                s_scr[ci] = s
                if i + 1 < sub_t:
                    prepare(ci, s, row(a_ref, ci, i + 1))
                z = (s * row(r_ref, ci, i)).astype(BF16)
                zstage[pl.ds((i * (nch // 2) + ci // 2) * RW_HEAD, RW_HEAD),
                       pl.ds((ci % 2) * LANES, LANES)] = z
        yres[...] = jnp.dot(zstage[...], j_pair, preferred_element_type=F32)
        for i in range(sub_t):
            place = (lane % RW_HEAD) == tg * sub_t + i
            for ci in range(nch):
                ycol = yres[pl.ds((i * (nch // 2) + ci // 2) * RW_HEAD, RW_HEAD),
                            pl.ds((ci % 2) * LANES, LANES)]
                yacc_scr[ci] = jnp.where(place, ycol, yacc_scr[ci])
        return carry

    lax.fori_loop(0, SCAN_CHUNK // sub_t, group, 0)

    zeros = jnp.zeros((LANES - RW_HEAD, LANES), F32)
    for ci, (b, p) in enumerate(chains):
        cols = pl.ds(p * LANES, LANES)
        tr = jnp.concatenate([yacc_scr[ci], zeros], axis=0).T
        y_ref[b, :, cols] = jnp.where(first, tr[0:RW_HEAD], pltpu.roll(tr[RW_HEAD:], RW_HEAD, 1))
        sfin_ref[b, :, cols] = s_scr[ci]


def _rwkv_scan(r, w, k, v, a, b, s0, zero_chunks):
    nb, tp, _ = r.shape
    assert tp % SCAN_CHUNK == 0 and SCAN_CHUNK == RW_HEAD
    nch = nb * (RW_DIM // LANES)
    seq = pl.BlockSpec((nb, SCAN_CHUNK, RW_DIM), lambda i: (0, i, 0))
    st = pl.BlockSpec((nb, RW_HEAD, RW_DIM), lambda i: (0, 0, 0))
    return pl.pallas_call(
        functools.partial(_scan_kernel, nb=nb, zero_chunks=zero_chunks),
        grid=(tp // SCAN_CHUNK,),
        in_specs=[seq] * 6 + [st],
        out_specs=[seq, st],
        out_shape=[jax.ShapeDtypeStruct((nb, tp, RW_DIM), F32),
                   jax.ShapeDtypeStruct((nb, RW_HEAD, RW_DIM), F32)],
        scratch_shapes=[pltpu.VMEM((nch, RW_HEAD, LANES), F32),
                        pltpu.VMEM((nch, RW_HEAD, LANES), F32),
                        pltpu.VMEM((8 * nch, RW_HEAD, LANES), F32),
                        pltpu.VMEM((nch, RW_HEAD, LANES), F32),
                        pltpu.VMEM((8 * nch * RW_HEAD, 2 * LANES), BF16),
                        pltpu.VMEM((4 * nch * RW_HEAD, 2 * LANES), BF16),
                        pltpu.VMEM((4 * nch * RW_HEAD, 2 * LANES), F32)],
        compiler_params=_cparams(("arbitrary",)),
        name="rwkv_scan",
    )(r, w, k, v, a, b, s0)


def _lambda_value(lam_ref):
    lv = lam_ref[...]
    s01 = jnp.sum(lv[0:1, :] * lv[1:2, :], axis=-1, keepdims=True)
    s23 = jnp.sum(lv[2:3, :] * lv[3:4, :], axis=-1, keepdims=True)
    return jnp.exp(s01) - jnp.exp(s23) + LAM_INIT


def _alibi_slope(h):
    slope = jnp.float32(2.0 ** (-8.0 * DA_HEADS / DA_HEADS))
    for i in range(DA_HEADS - 2, -1, -1):
        slope = jnp.where(h == i, jnp.float32(2.0 ** (-8.0 * (i + 1) / DA_HEADS)), slope)
    return slope


def _sub_norm(o, g):
    return o * lax.rsqrt(jnp.mean(o * o, axis=-1, keepdims=True) + NORM_EPS) * g * (1.0 - LAM_INIT)


ATT_ONES = 16


def _attn_prompt_kernel(q_ref, k_ref, v_ref, d0_ref, lam_ref, sg_ref, o_ref,
                        qc_ref, vt_ref, bias_ref, m_ref, acc_ref, s_ref, *, blk):
    h = pl.program_id(1)
    qi = pl.program_id(2)
    slope = _alibi_slope(h)
    tp = k_ref.shape[0]
    sub = blk // LANES

    @pl.when(qi == 0)
    def _():
        for jb in range(tp // LANES):
            vt = v_ref[jb * LANES:(jb + 1) * LANES, :].T.astype(BF16)
            vt_ref[jb // sub, 0:DA_DV, (jb % sub) * LANES:(jb % sub + 1) * LANES] = vt
        vt_ref[:, DA_DV:, :] = jnp.ones((tp // blk, ATT_ONES, blk), BF16)

    lane = lax.broadcasted_iota(jnp.int32, (blk, LANES), 1)
    qb = q_ref[...].astype(BF16)
    for c in range(2):
        qc_ref[c] = jnp.where((lane // DA_DK) == c, qb, jnp.zeros_like(qb))
    bias_ref[...] = slope * d0_ref[...]
    m_ref[...] = jnp.full_like(m_ref, NEG_BIG)
    acc_ref[...] = jnp.zeros_like(acc_ref)

    def scores(j):
        kb = k_ref[pl.ds(pl.multiple_of(j * blk, blk), blk), :].astype(BF16)
        for c in range(2):
            s_ref[(j % 2) * 2 + c] = lax.dot_general(kb, qc_ref[c], (((1,), (1,)), ((), ())),
                                                     preferred_element_type=F32)

    def consume(j, masked):
        off = ((qi - j) * blk).astype(F32)
        c0 = slope * off
        if masked:
            kpos = j * blk + lax.broadcasted_iota(jnp.int32, (blk, blk), 0)
            valid = (d0_ref[...] + off >= 0.0) & (kpos >= PAD_LEAD)
        for c in range(2):
            t = s_ref[(j % 2) * 2 + c] - bias_ref[...]
            if masked:
                t = jnp.where(valid, t, NEG_BIG)
            m_old = m_ref[c]
            m_new = jnp.maximum(m_old, jnp.max(t, axis=0, keepdims=True) - c0)
            alpha = jnp.exp(m_old - m_new)
            p = jnp.exp(t - (m_new + c0)).astype(BF16)
            acc_ref[c] = alpha * acc_ref[c] + jnp.dot(vt_ref[j], p, preferred_element_type=F32)
            m_ref[c] = m_new

    def middle(j, carry):
        consume(j, False)
        scores(j + 1)
        return carry

    scores(0)
    consume(0, True)
    scores(jnp.minimum(1, tp // blk - 1))
    lax.fori_loop(1, qi, middle, 0)

    @pl.when(qi > 0)
    def _():
        consume(qi, True)

    lam = _lambda_value(lam_ref)
    a0 = acc_ref[0]
    a1 = acc_ref[1]
    ot = a0[:DA_DV] / a0[DA_DV:DA_DV + 1] - lam * (a1[:DA_DV] / a1[DA_DV:DA_DV + 1])
    o_ref[...] = _sub_norm(ot.T, sg_ref[...])


def _attn_prompt(q, k, v, da_lam, subln_g, blk):
    b, tp, _ = q.shape
    assert tp % blk == 0 and blk % LANES == 0 and PAD_LEAD <= blk
    d0t = (jnp.arange(blk, dtype=F32)[None, :] - jnp.arange(blk, dtype=F32)[:, None])
    qspec = pl.BlockSpec((None, blk, DA_DV), lambda bi, h, i: (bi, i, h))
    kspec = pl.BlockSpec((None, tp, DA_DV), lambda bi, h, i: (bi, 0, h))
    full = lambda a: pl.BlockSpec(a.shape, lambda bi, h, i: (0,) * a.ndim)
    return pl.pallas_call(
        functools.partial(_attn_prompt_kernel, blk=blk),
        grid=(b, DA_HEADS, tp // blk),
        in_specs=[qspec, kspec, kspec, full(d0t), full(da_lam), full(subln_g)],
        out_specs=qspec,
        scratch_shapes=[pltpu.VMEM((2, blk, DA_DV), BF16),
                        pltpu.VMEM((tp // blk, DA_DV + ATT_ONES, blk), BF16),
                        pltpu.VMEM((blk, blk), F32), pltpu.VMEM((2, 1, blk), F32),
                        pltpu.VMEM((2, DA_DV + ATT_ONES, blk), F32),
                        pltpu.VMEM((4, blk, blk), F32)],
        out_shape=jax.ShapeDtypeStruct((b, tp, DA_V), F32),
        compiler_params=_cparams(("parallel", "parallel", "arbitrary")),
        name="attn_prompt",
    )(q, k, v, d0t, da_lam, subln_g)


def _attn_sample_kernel(pt_ref, q_ref, kn_ref, vn_ref, lam_ref, sg_ref, *rest, ppg, page, past):
    k_refs = rest[:ppg]
    v_refs = rest[ppg:2 * ppg]
    o_ref, m_ref, l_ref, acc_ref = rest[2 * ppg:]
    g = pl.program_id(1)
    nrow = 2 * DA_HEADS
    rowi = lax.broadcasted_iota(jnp.int32, (nrow, DA_QK), 0)
    lanei = lax.broadcasted_iota(jnp.int32, (nrow, DA_QK), 1)
    qsel = (lanei // DA_DK) == rowi
    qmat = jnp.where(qsel, q_ref[...], 0.0)
    rh = lax.broadcasted_iota(jnp.int32, (nrow, 1), 0) // 2
    slope = jnp.full((nrow, 1), 2.0 ** (-8.0), F32)
    for i in range(DA_HEADS - 2, -1, -1):
        slope = jnp.where(rh == i, jnp.float32(2.0 ** (-8.0 * (i + 1) / DA_HEADS)), slope)

    @pl.when(g == 0)
    def _():
        m_ref[...] = jnp.full_like(m_ref, NEG_BIG)
        l_ref[...] = jnp.zeros_like(l_ref)
        acc_ref[...] = jnp.zeros_like(acc_ref)

    qb = qmat.astype(BF16)
    kpos = g * (ppg * page) + lax.broadcasted_iota(jnp.int32, (nrow, ppg * page), 1)
    s = jnp.concatenate([jnp.dot(qb, k_refs[i][...].astype(BF16), preferred_element_type=F32)
                         for i in range(ppg)], axis=1)
    s = s - slope * (past - kpos).astype(F32)
    m_old = m_ref[...]
    m_new = jnp.maximum(m_old, jnp.max(s, axis=-1, keepdims=True))
    alpha = jnp.exp(m_old - m_new)
    p = jnp.exp(s - m_new)
    l_ref[...] = alpha * l_ref[...] + jnp.sum(p, axis=-1, keepdims=True)
    pb = p.astype(BF16)
    pv = [jnp.zeros((nrow, DA_DV), F32)] * DA_HEADS
    for i in range(ppg):
        for hh in range(DA_HEADS):
            vh = v_refs[i][pl.ds(hh, page, stride=DA_HEADS), :].astype(BF16)
            pv[hh] = pv[hh] + jnp.dot(pb[:, i * page:(i + 1) * page], vh, preferred_element_type=F32)
    tot = pv[DA_HEADS - 1]
    for hh in range(DA_HEADS - 1):
        tot = jnp.where(rh == hh, pv[hh], tot)
    acc_ref[...] = alpha * acc_ref[...] + tot
    m_ref[...] = m_new

    @pl.when(g == pl.num_programs(1) - 1)
    def _():
        kn = kn_ref[...].astype(BF16).astype(F32)
        s_self = jnp.sum(qb.astype(F32) * kn, axis=-1, keepdims=True)
        m_old = m_ref[...]
        m_new = jnp.maximum(m_old, s_self)
        alpha = jnp.exp(m_old - m_new)
        p_self = jnp.exp(s_self - m_new)
        l_fin = alpha * l_ref[...] + p_self
        vn = vn_ref[...].astype(BF16).astype(F32)
        vrow = jnp.zeros((nrow, DA_DV), F32)
        for hh in range(DA_HEADS):
            vrow = jnp.where(rh == hh, vn[:, hh * DA_DV:(hh + 1) * DA_DV], vrow)
        acc = alpha * acc_ref[...] + p_self.astype(BF16).astype(F32) * vrow
        lam = _lambda_value(lam_ref)
        ri = lax.broadcasted_iota(jnp.int32, (nrow, 1), 0)
        t = acc / l_fin * jnp.where(ri % 2 == 0, 1.0, -lam)
        outs = []
        for hh in range(DA_HEADS):
            o = t[2 * hh:2 * hh + 1, :] + t[2 * hh + 1:2 * hh + 2, :]
            outs.append(_sub_norm(o, sg_ref[...]))
        o_ref[...] = jnp.concatenate(outs, axis=1)


def _attn_sample(q, kn, vn, cache_kt, cache_v, page_table, da_lam, subln_g, ppg):
    bd = q.shape[0]
    n_pages = page_table.shape[1]
    page = cache_kt.shape[2]
    assert n_pages % ppg == 0 and cache_v.shape[1:] == (page * DA_HEADS, DA_DV)
    pt = page_table.reshape(-1)
    one = pl.BlockSpec((None, 1, DA_QK), lambda b, g, pt: (b, 0, 0))
    full = lambda a: pl.BlockSpec(a.shape, lambda b, g, pt: (0,) * a.ndim)

    def pspec(i):
        return pl.BlockSpec((None, DA_QK, page),
                            lambda b, g, pt, i=i: (pt[b * n_pages + g * ppg + i], 0, 0))

    out = pl.pallas_call(
        functools.partial(_attn_sample_kernel, ppg=ppg, page=page, past=n_pages * page),
        grid_spec=pltpu.PrefetchScalarGridSpec(
            num_scalar_prefetch=1,
            grid=(bd, n_pages // ppg),
            in_specs=[one, one, one, full(da_lam), full(subln_g)]
                     + [pspec(i) for i in range(ppg)] * 2,
            out_specs=one,
            scratch_shapes=[pltpu.VMEM((2 * DA_HEADS, 1), F32), pltpu.VMEM((2 * DA_HEADS, 1), F32),
                            pltpu.VMEM((2 * DA_HEADS, DA_DV), F32)],
        ),
        out_shape=jax.ShapeDtypeStruct((bd, 1, DA_V), F32),
        compiler_params=_cparams(("parallel", "arbitrary")),
        name="attn_sample",
    )(pt, q.reshape(bd, 1, DA_QK), kn.reshape(bd, 1, DA_QK), vn.reshape(bd, 1, DA_V), da_lam, subln_g,
      *([cache_kt] * ppg), *([cache_v] * ppg))
    return out.reshape(bd, DA_V)


def _merge_kernel(*refs, nsub):
    rows = [refs[k * nsub:(k + 1) * nsub] for k in range(6)]
    (lng_ref, lnb_ref, wrw_ref, wda_ref, wout_ref, n2_ref, wq_ref, sk_ref,
     h2_ref, xn_ref, st_ref) = refs[6 * nsub:]
    stack = lambda rs: rs[0][...] if nsub == 1 else jnp.concatenate([r[...] for r in rs], axis=0)
    y, bonus, g, o, gt, h = [stack(rs) for rs in rows]
    j = _seg_ones()
    mu = _seg64(y, j) * (1.0 / RW_HEAD)
    d = y - mu
    var = _seg64(d * d, j) * (1.0 / RW_HEAD)
    yln = d * lax.rsqrt(var + RW_LN_EPS) * lng_ref[...] + lnb_ref[...]
    yr = ((yln + bonus) * g).astype(BF16)
    y_rw = jnp.dot(yr, wrw_ref[...], preferred_element_type=F32)
    y_da = jnp.dot(o.astype(BF16), wda_ref[...], preferred_element_type=F32)
    mix = gt[:, :D_MODEL] * y_rw + gt[:, D_MODEL:] * y_da
    h2 = h + jnp.dot(mix.astype(BF16), wout_ref[...], preferred_element_type=F32)
    h2_ref[...] = h2
    ms = jnp.mean(h2 * h2, axis=-1, keepdims=True)
    xn = (h2 * lax.rsqrt(ms + NORM_EPS) * n2_ref[...]).astype(BF16)
    xn_ref[...] = xn
    q = jnp.dot(xn, wq_ref[...], preferred_element_type=F32)
    for hc in range(2 * PEER_HEADS):
        qc = q[:, hc * PEER_DHALF:(hc + 1) * PEER_DHALF].astype(BF16)
        st_ref[hc * N_KEYS:(hc + 1) * N_KEYS, :] = lax.dot_general(
            sk_ref[hc % 2], qc, (((1,), (1,)), ((), ())), preferred_element_type=F32)


def _merge(y, bonus, g, o, gates, h, consts, tm, tile_off, nsub):
    b, rows, _ = y.shape
    assert rows % tm == 0 and (rows // tm - tile_off) % nsub == 0
    nt = (rows // tm - tile_off) // nsub
    full = lambda a: pl.BlockSpec(a.shape, lambda i, t: (0,) * a.ndim)

    def rin(c):
        return [pl.BlockSpec((None, tm, c), lambda i, t, k=k: (i, t * nsub + k + tile_off, 0))
                for k in range(nsub)]

    row_arrays = (y, bonus, g, o, gates, h)
    tmo = tm * nsub
    ntok = b * nt * tmo
    return pl.pallas_call(
        functools.partial(_merge_kernel, nsub=nsub),
        grid=(b, nt),
        in_specs=[s for a in row_arrays for s in rin(a.shape[-1])] + [full(a) for a in consts],
        out_specs=[pl.BlockSpec((tmo, D_MODEL), lambda i, t: (i * nt + t, 0)),
                   pl.BlockSpec((tmo, D_MODEL), lambda i, t: (i * nt + t, 0)),
                   pl.BlockSpec((2 * PEER_HEADS * N_KEYS, tmo), lambda i, t: (0, i * nt + t))],
        out_shape=[jax.ShapeDtypeStruct((ntok, D_MODEL), F32),
                   jax.ShapeDtypeStruct((ntok, D_MODEL), BF16),
                   jax.ShapeDtypeStruct((2 * PEER_HEADS * N_KEYS, ntok), F32)],
        compiler_params=_cparams(("parallel", "parallel")),
        name="merge",
    )(*[a for a in row_arrays for _ in range(nsub)], *consts)


def _count(mask):
    return jnp.sum(jnp.where(mask, 1.0, 0.0), axis=0, keepdims=True)


def _top16(x, exact):
    nk = x.shape[0]
    iota = lax.broadcasted_iota(jnp.int32, x.shape, 0).astype(F32)
    rank = jnp.full(x.shape, float(nk), F32)
    vals = []
    for r in range(PEER_TOPK):
        m = jnp.max(x, axis=0, keepdims=True)
        sel = x == m
        if exact:
            cand = jnp.where(sel, iota, float(nk))
            sel = cand == jnp.min(cand, axis=0, keepdims=True)
        x = jnp.where(sel, -jnp.inf, x)
        rank = jnp.where(sel, float(r), rank)
        vals.append(m)
    return vals, rank


def _peer_topk_kernel(st_ref, rank2_ref, c2_ref, cnt1_ref, c1_ref):
    n = st_ref.shape[1]
    rid = lax.broadcasted_iota(jnp.int32, (_CAND_ROWS, n), 0)
    flat = jnp.zeros((_CAND_ROWS, n), jnp.int32) + PEER_TOPK * PEER_TOPK
    irow = jnp.zeros((_CAND_ROWS, n), jnp.int32) + PEER_TOPK
    for ci, (i, j) in enumerate(_CAND):
        flat = jnp.where(rid == ci, i * PEER_TOPK + j, flat)
        irow = jnp.where(rid == ci, i, irow)
    outs = (rank2_ref, c2_ref, cnt1_ref, c1_ref)
    ties = _peer_topk_body(st_ref, outs, flat, irow, exact=False)

    @pl.when(jnp.max(ties) > 0.0)
    def _():
        _peer_topk_body(st_ref, outs, flat, irow, exact=True)


def _peer_topk_body(st_ref, outs, flat, irow, exact):
    rank2_ref, c2_ref, cnt1_ref, c1_ref = outs
    n = st_ref.shape[1]
    ties = jnp.zeros((1, n), F32)
    for h in range(PEER_HEADS):
        s1 = st_ref[(2 * h) * N_KEYS:(2 * h + 1) * N_KEYS, :]
        s2 = st_ref[(2 * h + 1) * N_KEYS:(2 * h + 2) * N_KEYS, :]
        v1, rank1 = _top16(s1, exact)
        v2, rank2 = _top16(s2, exact)
        rows = [v1[i] + v2[j] for (i, j) in _CAND]
        rows += [jnp.full((1, n), -jnp.inf, F32)] * (_CAND_ROWS - len(_CAND))
        cand = jnp.concatenate(rows, axis=0)
        top = v1[0] + v2[0]
        chosen = jnp.zeros(cand.shape, jnp.bool_)
        work = cand
        for _ in range(PEER_TOPK):
            m = jnp.max(work, axis=0, keepdims=True)
            sel = work == m
            if exact:
                fl = jnp.where(sel, flat, PEER_TOPK * PEER_TOPK + 1)
                sel = fl == jnp.min(fl, axis=0, keepdims=True)
            work = jnp.where(sel, -jnp.inf, work)
            chosen = chosen | sel
        if not exact:
            k = float(PEER_TOPK)
            bad = (_count(rank1 < k) != k) | (_count(rank2 < k) != k) | (_count(chosen) != k)
            ties = jnp.maximum(ties, jnp.where(bad, 1.0, 0.0))
        z = jnp.sum(jnp.where(chosen, jnp.exp(cand - top), 0.0), axis=0, keepdims=True)
        cnt1 = jnp.zeros(s1.shape, F32)
        for i in range(PEER_TOPK):
            m_i = jnp.sum(jnp.where(chosen & (irow == i), 1.0, 0.0), axis=0, keepdims=True)
            cnt1 = jnp.where(rank1 == float(i), m_i, cnt1)
        cnt1_ref[h] = cnt1
        c1_ref[h] = jnp.where(rank1 < float(PEER_TOPK), jnp.exp(s1 - v1[0]) / z, 0.0)
        c2_ref[h] = jnp.where(rank2 < float(PEER_TOPK), jnp.exp(s2 - v2[0]), 0.0).astype(BF16)
        rank2_ref[h] = rank2.astype(BF16)
    return ties


def _peer_topk(st, tn):
    ntok = st.shape[1]
    assert ntok % tn == 0
    spec = pl.BlockSpec((PEER_HEADS, N_KEYS, tn), lambda i: (0, 0, i))
    return pl.pallas_call(
        _peer_topk_kernel,
        grid=(ntok // tn,),
        in_specs=[pl.BlockSpec((st.shape[0], tn), lambda i: (0, i))],
        out_specs=[spec] * 4,
        out_shape=[jax.ShapeDtypeStruct((PEER_HEADS, N_KEYS, ntok), dt) for dt in (BF16, BF16, F32, F32)],
        compiler_params=_cparams(("parallel",)),
        name="peer_topk",
    )(st)


def _peer_dense_kernel(x_ref, u_ref, vt_ref, rank2_ref, c2_ref, cnt1_ref, c1_ref, h2_ref, y_ref,
                       acc_ref, hid_ref, p_ref, *, e1b):
    j = pl.program_id(1)
    tn = x_ref.shape[0]
    ngrp = N_KEYS // BF16_ROWS

    @pl.when(j == 0)
    def _():
        acc_ref[...] = jnp.zeros_like(acc_ref)

    hid_ref[...] = lax.dot_general(u_ref[...], x_ref[...], (((1,), (1,)), ((), ())),
                                   preferred_element_type=F32)

    for e in range(e1b):
        e1 = j * e1b + e
        gate = [None] * ngrp
        for h in range(PEER_HEADS):
            cnt = jnp.broadcast_to(cnt1_ref[h, pl.ds(e1, 1), :], (BF16_ROWS, tn)).astype(BF16)
            c1 = jnp.broadcast_to(c1_ref[h, pl.ds(e1, 1), :], (BF16_ROWS, tn)).astype(BF16)
            for g in range(ngrp):
                rows = slice(g * BF16_ROWS, (g + 1) * BF16_ROWS)
                c2 = c2_ref[h, rows, :]
                term = jnp.where(rank2_ref[h, rows, :] < cnt, c2, jnp.zeros_like(c2)) * c1
                gate[g] = term if gate[g] is None else gate[g] + term
        hh = hid_ref[e * N_KEYS:(e + 1) * N_KEYS, :]
        gelu = (0.5 * hh * (1.0 + lax.erf(hh * (2.0 ** -0.5)))).astype(BF16)
        for g in range(ngrp):
            r0 = e * N_KEYS + g * BF16_ROWS
            p_ref[r0:r0 + BF16_ROWS, :] = gelu[g * BF16_ROWS:(g + 1) * BF16_ROWS] * gate[g]
    acc_ref[...] += jnp.dot(vt_ref[...], p_ref[...], preferred_element_type=F32)

    @pl.when(j == pl.num_programs(1) - 1)
    def _():
        y_ref[...] = h2_ref[...] + acc_ref[...].T


def _peer_dense(xn, u_bf, vt_bf, rank2, c2, cnt1, c1, h2, tn, e1b):
    ntok = xn.shape[0]
    assert ntok % tn == 0 and N_KEYS % e1b == 0
    tok = pl.BlockSpec((PEER_HEADS, N_KEYS, tn), lambda i, j: (0, 0, i))
    row = pl.BlockSpec((tn, D_MODEL), lambda i, j: (i, 0))
    return pl.pallas_call(
        functools.partial(_peer_dense_kernel, e1b=e1b),
        grid=(ntok // tn, N_KEYS // e1b),
        in_specs=[row,
                  pl.BlockSpec((e1b * N_KEYS, D_MODEL), lambda i, j: (j, 0)),
                  pl.BlockSpec((D_MODEL, e1b * N_KEYS), lambda i, j: (0, j)),
                  tok, tok, tok, tok, row],
        out_specs=row,
        out_shape=jax.ShapeDtypeStruct((ntok, D_MODEL), F32),
        scratch_shapes=[pltpu.VMEM((D_MODEL, tn), F32), pltpu.VMEM((e1b * N_KEYS, tn), F32),
                        pltpu.VMEM((e1b * N_KEYS, tn), BF16)],
        compiler_params=_cparams(("parallel", "arbitrary")),
        name="peer_dense",
    )(xn, u_bf, vt_bf, rank2, c2, cnt1, c1, h2)


def _state_in(s):
    b = s.shape[0]
    return jnp.transpose(s, (0, 2, 1, 3)).reshape(b, RW_HEAD, RW_DIM)


def _state_out(s):
    b = s.shape[0]
    return jnp.transpose(s.reshape(b, RW_HEAD, RW_HEADS, RW_HEAD), (0, 2, 1, 3))


def _pad_tokens(a, n, axis):
    pad = [(0, 0)] * a.ndim
    pad[axis] = (0, n - a.shape[axis])
    return jnp.pad(a, pad)


def kernel(x_prompt, x_sample, cache_k, cache_v, state_wkv, state_shift, page_table, meta, norm1_g, w_in,
           rw_mu, rw_w0, rw_w2, rw_a0, rw_a2, rw_g2, rw_kk, rw_ka, rw_rk, rw_ln_g, rw_ln_b, w_rw_br,
           da_qn_g, da_kn_g, da_lam, da_subln_g, w_da_br, w_out, norm2_g, peer_wq, peer_subkeys,
           peer_u, peer_v):
    b, seq, _ = x_prompt.shape
    bd = x_sample.shape[0]
    t_real = N_META + seq
    tp = PAD_LEAD + t_real
    assert tp % ROW_TILE == 0 and tp % SCAN_CHUNK == 0 and x_sample.shape[1] == 1
    row2 = lambda a: a.reshape(1, -1)

    w_in_bf = w_in[0].astype(BF16)
    g1 = row2(norm1_g[0])
    qg = row2(jnp.tile(da_qn_g[0], DA_QK // DA_DK))
    kg = row2(jnp.tile(da_kn_g[0], DA_QK // DA_DK))
    zlo = jnp.zeros((W_LORA, RW_DIM), F32)
    prep_params = (row2(rw_mu[0]), row2(rw_w0[0]),
                   jnp.concatenate([rw_w2[0], zlo], axis=0).astype(BF16), row2(rw_a0[0]),
                   jnp.concatenate([zlo, rw_a2[0]], axis=0).astype(BF16), rw_g2[0].astype(BF16),
                   row2(rw_kk[0]), row2(rw_ka[0]), row2(rw_rk[0]))
    merge_consts = (row2(rw_ln_g[0]), row2(rw_ln_b[0]), w_rw_br[0].astype(BF16), w_da_br[0].astype(BF16),
                    w_out[0].astype(BF16), row2(norm2_g[0]), peer_wq[0].astype(BF16),
                    peer_subkeys[0].astype(BF16))
    subln = row2(da_subln_g[0])
    u_bf = peer_u[0].astype(BF16)
    vt_bf = peer_v[0].astype(BF16).T

    hp = jnp.concatenate([jnp.zeros((b, PAD_LEAD, D_MODEL), F32),
                          jnp.broadcast_to(meta[None], (b, N_META, D_MODEL)), x_prompt], axis=1)
    prw, q, k, v, gates = _in_proj(hp.reshape(b * tp, D_MODEL), g1, w_in_bf, qg, kg, 512)
    r3 = lambda a: a.reshape(b, tp, -1)
    pr, pw, pk, pv, pa, pb, pg, pbonus = _rwkv_prep_prompt(r3(prw), prep_params, 384)
    y_scan, s_fin = _rwkv_scan(pr, pw, pk, pv, pa, pb, jnp.zeros((b, RW_HEAD, RW_DIM), F32),
                               PAD_LEAD // SCAN_CHUNK)
    o_attn = _attn_prompt(r3(q), r3(k), r3(v), da_lam[0], subln, 384)
    h2, xn2, st = _merge(y_scan, pbonus, pg, o_attn, r3(gates), hp, merge_consts, ROW_TILE, 1, 4)
    y_prompt = _peer_dense(xn2, u_bf, vt_bf, *_peer_topk(st, 128), h2, 512, 16)

    xs = x_sample.reshape(bd, D_MODEL)
    prw_s, q_s, k_s, v_s, gates_s = _in_proj(xs, g1, w_in_bf, qg, kg, bd)
    sr, sw, sk, sv, sa, sb, sg, sbonus = _rwkv_prep_sample(prw_s, state_shift[0], prep_params)
    ys_scan, ss_fin = _rwkv_step(sr, sw, sk, sv, sa, sb, _state_in(state_wkv[0]))
    n_phys = cache_k.shape[1]
    ckt = jnp.transpose(cache_k[0].reshape(n_phys, -1, DA_QK), (0, 2, 1))
    cv = cache_v[0].reshape(n_phys, -1, DA_DV)
    o_s = _attn_sample(q_s, k_s, v_s, ckt, cv, page_table, da_lam[0], subln, 16)
    f3 = lambda a: a.reshape(1, bd, -1)
    h2s, xn2s, sts = _merge(ys_scan.reshape(1, bd, RW_DIM), f3(sbonus), f3(sg), f3(o_s), f3(gates_s),
                            f3(xs), merge_consts, bd, 0, 1)
    npad = LANES
    tk_s = _peer_topk(_pad_tokens(sts, npad, 1), npad)
    y_s = _peer_dense(_pad_tokens(xn2s, npad, 0), u_bf, vt_bf, *tk_s, _pad_tokens(h2s, npad, 0), npad, 8)

    y_prompt = y_prompt.reshape(b, seq, D_MODEL)
    y_sample = y_s[:bd].reshape(bd, 1, D_MODEL)
    k_p = r3(k)[:, PAD_LEAD:].reshape(1, b, t_real, DA_HEADS, 2, DA_DK)
    v_p = r3(v)[:, PAD_LEAD:].reshape(1, b, t_real, DA_HEADS, DA_DV)
    return (y_prompt, y_sample, k_p, v_p,
            k_s.reshape(1, bd, 1, DA_HEADS, 2, DA_DK), v_s.reshape(1, bd, 1, DA_HEADS, DA_DV),
            _state_out(s_fin)[None], _state_out(ss_fin)[None],
            r3(prw)[:, -1][None], prw_s[None])
```

```python
import functools
import math

import jax
import jax.numpy as jnp
from jax import lax
from jax.experimental import pallas as pl
from jax.experimental.pallas import tpu as pltpu

F32 = jnp.float32
BF16 = jnp.bfloat16

D_MODEL = 1024
N_META = 16
NORM_EPS = 1e-6
RW_HEAD = 64
RW_DIM = D_MODEL // 2
RW_HEADS = RW_DIM // RW_HEAD
W_LORA = 64
A_LORA = 64
G_LORA = 128
RW_PROJ = 3 * RW_DIM + W_LORA + A_LORA + G_LORA
RW_LN_EPS = 64e-5
DA_DK = 64
DA_DV = 2 * DA_DK
DA_DIM = D_MODEL // 2
DA_HEADS = DA_DIM // DA_DV
DA_QK = DA_HEADS * 2 * DA_DK
DA_V = DA_HEADS * DA_DV
N_KEYS = 128
PEER_HEADS = 8
PEER_TOPK = 16
PEER_DQ = 256
PEER_DHALF = PEER_DQ // 2
LAM_INIT = 0.8 - 0.6 * math.exp(-0.3 * 0)

LANES = 128
BF16_ROWS = 16
ROW_TILE = 128
PAD_LEAD = ROW_TILE - N_META
SCAN_CHUNK = 64
NEG_BIG = -1e30
VMEM_LIMIT = 56 * 1024 * 1024

IN_PROJ_ROWS = 512
PREP_ROWS = 384
ATTN_BLOCK = 384
MERGE_STACK = 4
TOPK_TOKENS = 128
DENSE_TOKENS = 512
DENSE_E1 = 16
PAGES_PER_STEP = 16

O1 = RW_PROJ
O2 = O1 + DA_QK
O3 = O2 + DA_QK
O4 = O3 + DA_V
O5 = O4 + D_MODEL
O6 = O5 + D_MODEL

_CAND = [(i, j) for i in range(PEER_TOPK) for j in range(PEER_TOPK) if (i + 1) * (j + 1) <= PEER_TOPK]
_CAND_ROWS = -(-len(_CAND) // 8) * 8


def _cparams(sem):
    return pltpu.CompilerParams(dimension_semantics=sem, vmem_limit_bytes=VMEM_LIMIT)


def _split_dot(x, j):
    hi = x.astype(BF16)
    lo = (x - hi.astype(F32)).astype(BF16)
    return (jnp.dot(hi, j, preferred_element_type=F32)
            + jnp.dot(lo, j, preferred_element_type=F32))


def _seg64(x, j):
    outs = [_split_dot(x[:, c * LANES:(c + 1) * LANES], j) for c in range(x.shape[1] // LANES)]
    return outs[0] if len(outs) == 1 else jnp.concatenate(outs, axis=1)


def _seg_ones():
    r = lax.broadcasted_iota(jnp.int32, (LANES, LANES), 0) // 64
    c = lax.broadcasted_iota(jnp.int32, (LANES, LANES), 1) // 64
    return (r == c).astype(BF16)


def _in_proj_kernel(x_ref, g1_ref, w_ref, qg_ref, kg_ref, prw_ref, q_ref, k_ref, v_ref, gt_ref):
    x = x_ref[...]
    ms = jnp.mean(x * x, axis=-1, keepdims=True)
    xn = (x * lax.rsqrt(ms + NORM_EPS) * g1_ref[...]).astype(BF16)

    def mm(a, b):
        return jnp.dot(xn, w_ref[:, a:b], preferred_element_type=F32)

    j = _seg_ones()

    def head_norm(t, g):
        msq = _seg64(t * t, j) * (1.0 / DA_DK)
        return t * lax.rsqrt(msq + NORM_EPS) * g

    prw_ref[...] = mm(0, O1)
    q_ref[...] = head_norm(mm(O1, O2), qg_ref[...]) * (DA_DK ** -0.5)
    k_ref[...] = head_norm(mm(O2, O3), kg_ref[...])
    v_ref[...] = mm(O3, O4)
    gt_ref[...] = jax.nn.sigmoid(mm(O4, O6))


def _in_proj(x, g1, w_bf, qg, kg, tm):
    n = x.shape[0]
    assert n % tm == 0
    row = lambda c: pl.BlockSpec((tm, c), lambda i: (i, 0))
    full = lambda a: pl.BlockSpec(a.shape, lambda i: (0,) * a.ndim)
    widths = (RW_PROJ, DA_QK, DA_QK, DA_V, 2 * D_MODEL)
    return pl.pallas_call(
        _in_proj_kernel,
        grid=(n // tm,),
        in_specs=[row(D_MODEL), full(g1), full(w_bf), full(qg), full(kg)],
        out_specs=[row(c) for c in widths],
        out_shape=[jax.ShapeDtypeStruct((n, c), F32) for c in widths],
        compiler_params=_cparams(("parallel",)),
        name="in_proj",
    )(x, g1, w_bf, qg, kg)


def _rwkv_prep_math(p, p_prev, mu, w0, w2p, a0, a2p, g2, kkp, kap, rk):
    ps = p + (p_prev - p) * mu
    r = ps[:, 0:RW_DIM]
    k = ps[:, RW_DIM:2 * RW_DIM]
    v = ps[:, 2 * RW_DIM:3 * RW_DIM]
    wa = ps[:, 3 * RW_DIM:3 * RW_DIM + W_LORA + A_LORA]
    gl = ps[:, 3 * RW_DIM + W_LORA + A_LORA:RW_PROJ]
    w = -jax.nn.softplus(-(w0 + jnp.dot(jnp.tanh(wa).astype(BF16), w2p, preferred_element_type=F32))) - 0.5
    decay = jnp.exp(-jnp.exp(w))
    a = jax.nn.sigmoid(a0 + jnp.dot(wa.astype(BF16), a2p, preferred_element_type=F32))
    g = jnp.dot(jax.nn.sigmoid(gl).astype(BF16), g2, preferred_element_type=F32)
    j = _seg_ones()
    kk = k * kkp
    kk = kk / jnp.maximum(jnp.sqrt(_seg64(kk * kk, j)), 1e-12)
    k2 = k * (1.0 + (a - 1.0) * kap)
    bonus = _seg64(r * k2 * rk, j) * v
    return r, decay, k2, v, -kk, kk * a, g, bonus


def _rwkv_prep_carry_kernel(p_ref, mu, w0, w2p, a0, a2p, g2, kkp, kap, rk, *rest):
    outs, carry = rest[:-1], rest[-1]

    @pl.when(pl.program_id(1) == 0)
    def _():
        carry[...] = jnp.zeros_like(carry)

    p = p_ref[...]
    rows = lax.broadcasted_iota(jnp.int32, p.shape, 0)
    p_prev = jnp.where(rows == 0, carry[...], pltpu.roll(p, 1, 0))
    carry[...] = p[p.shape[0] - 1:, :]
    res = _rwkv_prep_math(p, p_prev, mu[...], w0[...], w2p[...], a0[...], a2p[...], g2[...],
                          kkp[...], kap[...], rk[...])
    for o, val in zip(outs, res):
        o[...] = val


def _rwkv_prep_given_kernel(p_ref, pp_ref, mu, w0, w2p, a0, a2p, g2, kkp, kap, rk, *outs):
    res = _rwkv_prep_math(p_ref[...], pp_ref[...], mu[...], w0[...], w2p[...], a0[...], a2p[...],
                          g2[...], kkp[...], kap[...], rk[...])
    for o, val in zip(outs, res):
        o[...] = val


def _rwkv_prep_prompt(p3, params, tm):
    b, tp, _ = p3.shape
    assert tp % tm == 0
    full = lambda a: pl.BlockSpec(a.shape, lambda i, j: (0,) * a.ndim)
    return pl.pallas_call(
        _rwkv_prep_carry_kernel,
        grid=(b, tp // tm),
        in_specs=[pl.BlockSpec((None, tm, RW_PROJ), lambda i, j: (i, j, 0))] + [full(a) for a in params],
        out_specs=[pl.BlockSpec((None, tm, RW_DIM), lambda i, j: (i, j, 0))] * 8,
        out_shape=[jax.ShapeDtypeStruct((b, tp, RW_DIM), F32)] * 8,
        scratch_shapes=[pltpu.VMEM((1, RW_PROJ), F32)],
        compiler_params=_cparams(("arbitrary", "arbitrary")),
        name="rwkv_prep_prompt",
    )(p3, *params)


def _rwkv_prep_sample(p, p_prev, params):
    n = p.shape[0]
    full = lambda a: pl.BlockSpec(a.shape, lambda i: (0,) * a.ndim)
    return pl.pallas_call(
        _rwkv_prep_given_kernel,
        grid=(1,),
        in_specs=[full(p), full(p_prev)] + [full(a) for a in params],
        out_specs=[pl.BlockSpec((n, RW_DIM), lambda i: (0, 0))] * 8,
        out_shape=[jax.ShapeDtypeStruct((n, RW_DIM), F32)] * 8,
        compiler_params=_cparams(("arbitrary",)),
        name="rwkv_prep_sample",
    )(p, p_prev, *params)


def _scan_consts():
    lane = lax.broadcasted_iota(jnp.int32, (RW_HEAD, LANES), 1)
    sub = lax.broadcasted_iota(jnp.int32, (RW_HEAD, LANES), 0)
    first = lane < RW_HEAD
    eye_lo = (lane == sub).astype(F32)
    eye_hi = (lane - RW_HEAD == sub).astype(F32)
    return lane, first, eye_lo, eye_hi


def _seg(first, x_lo, x_hi):
    lo = jnp.sum(x_lo, axis=-1, keepdims=True)
    hi = jnp.sum(x_hi, axis=-1, keepdims=True)
    return jnp.where(first, lo, hi)


def _split_row(first, row):
    lo = jnp.where(first[0:1, :], row, 0.0)
    return lo, row - lo


def _scan_step_kernel(r_ref, w_ref, k_ref, v_ref, a_ref, b_ref, s0_ref, y_ref, s_ref, *, nb):
    _, first, eye_lo, eye_hi = _scan_consts()
    for b in range(nb):
        for p in range(RW_DIM // LANES):
            cols = pl.ds(p * LANES, LANES)
            r_t, w_t, k_t, v_t, a_t, b_t = [ref[pl.ds(b, 1), cols]
                                            for ref in (r_ref, w_ref, k_ref, v_ref, a_ref, b_ref)]
            s = s0_ref[b, :, cols]
            a_lo, a_hi = _split_row(first, a_t)
            r_lo, r_hi = _split_row(first, r_t)
            sa = _seg(first, s * a_lo, s * a_hi)
            vcol = _seg(first, eye_lo * v_t, eye_hi * v_t)
            s = s * w_t + sa * b_t + vcol * k_t
            s_ref[b, :, cols] = s
            ycol = _seg(first, s * r_lo, s * r_hi)
            y_ref[pl.ds(b, 1), cols] = jnp.sum((eye_lo + eye_hi) * ycol, axis=0, keepdims=True)


def _rwkv_step(r, w, k, v, a, b, s0):
    nb = r.shape[0]
    full = lambda x: pl.BlockSpec(x.shape, lambda i: (0,) * x.ndim)
    return pl.pallas_call(
        functools.partial(_scan_step_kernel, nb=nb),
        grid=(1,),
        in_specs=[full(r)] * 6 + [full(s0)],
        out_specs=[full(r), full(s0)],
        out_shape=[jax.ShapeDtypeStruct(r.shape, F32), jax.ShapeDtypeStruct(s0.shape, F32)],
        compiler_params=_cparams(("arbitrary",)),
        name="rwkv_step",
    )(r, w, k, v, a, b, s0)


def _scan_kernel(r_ref, w_ref, k_ref, v_ref, a_ref, b_ref, s0_ref, y_ref, sfin_ref,
                 s_scr, sa_scr, vc_scr, yacc_scr, vstage, zstage, yres, *, nb, zero_chunks):
    npair = RW_DIM // LANES
    chains = [(b, p) for b in range(nb) for p in range(npair)]

    @pl.when(pl.program_id(0) == 0)
    def _():
        for ci, (b, p) in enumerate(chains):
            s_scr[ci] = s0_ref[b, :, p * LANES:(p + 1) * LANES]

    @pl.when(pl.program_id(0) < zero_chunks)
    def _():
        y_ref[...] = jnp.zeros_like(y_ref)
        sfin_ref[...] = s0_ref[...]

    @pl.when(pl.program_id(0) >= zero_chunks)
    def _():
        _scan_chunk(r_ref, w_ref, k_ref, v_ref, a_ref, b_ref, y_ref, sfin_ref,
                    s_scr, sa_scr, vc_scr, yacc_scr, vstage, zstage, yres, chains)


def _scan_chunk(r_ref, w_ref, k_ref, v_ref, a_ref, b_ref, y_ref, sfin_ref,
                s_scr, sa_scr, vc_scr, yacc_scr, vstage, zstage, yres, chains):
    lane, first, eye_lo, eye_hi = _scan_consts()
    sub_t = 8
    yacc_scr[...] = jnp.zeros_like(yacc_scr)
    nch = len(chains)
    eye2 = eye_lo + eye_hi
    seg_ones = _seg_ones()
    zero_b = jnp.zeros((LANES, LANES), BF16)
    j_stack = jnp.concatenate([seg_ones, seg_ones], axis=0)
    j_pair = jnp.concatenate([jnp.concatenate([seg_ones, zero_b], axis=1),
                              jnp.concatenate([zero_b, seg_ones], axis=1)], axis=0)

    def prepare(ci, s, a_t):
        a_lo, a_hi = _split_row(first, a_t)
        sa_scr[ci] = _seg(first, s * a_lo, s * a_hi)

    def group(tg, carry):
        rows = pl.ds(pl.multiple_of(tg * sub_t, sub_t), sub_t)

        def row(ref, ci, i):
            b, p = chains[ci]
            return ref[b, rows, pl.ds(p * LANES, LANES)][i:i + 1, :]

        for i in range(sub_t):
            for ci in range(nch):
                v_t = row(v_ref, ci, i)
                v_hi = v_t.astype(BF16).astype(F32)
                lhs = jnp.concatenate([eye2 * v_hi, eye2 * (v_t - v_hi)], axis=1)
                vstage[pl.ds((i * nch + ci) * RW_HEAD, RW_HEAD), :] = lhs.astype(BF16)
        vc_scr[...] = jnp.dot(vstage[...], j_stack, preferred_element_type=F32).reshape(vc_scr.shape)

        for ci in range(nch):
            prepare(ci, s_scr[ci], row(a_ref, ci, 0))
        for i in range(sub_t):
            for ci in range(nch):
                s = s_scr[ci] * row(w_ref, ci, i) + sa_scr[ci] * row(b_ref, ci, i) \
                    + vc_scr[i * nch + ci] * row(k_ref, ci, i)
                s_scr[ci] = s
                if i + 1 < sub_t:
                    prepare(ci, s, row(a_ref, ci, i + 1))
                z = (s * row(r_ref, ci, i)).astype(BF16)
                zstage[pl.ds((i * (nch // 2) + ci // 2) * RW_HEAD, RW_HEAD),
                       pl.ds((ci % 2) * LANES, LANES)] = z
        yres[...] = jnp.dot(zstage[...], j_pair, preferred_element_type=F32)
        for i in range(sub_t):
            place = (lane % RW_HEAD) == tg * sub_t + i
            for ci in range(nch):
                ycol = yres[pl.ds((i * (nch // 2) + ci // 2) * RW_HEAD, RW_HEAD),
                            pl.ds((ci % 2) * LANES, LANES)]
                yacc_scr[ci] = jnp.where(place, ycol, yacc_scr[ci])
        return carry

    lax.fori_loop(0, SCAN_CHUNK // sub_t, group, 0)

    zeros = jnp.zeros((LANES - RW_HEAD, LANES), F32)
    for ci, (b, p) in enumerate(chains):
        cols = pl.ds(p * LANES, LANES)
        tr = jnp.concatenate([yacc_scr[ci], zeros], axis=0).T
        y_ref[b, :, cols] = jnp.where(first, tr[0:RW_HEAD], pltpu.roll(tr[RW_HEAD:], RW_HEAD, 1))
        sfin_ref[b, :, cols] = s_scr[ci]


def _rwkv_scan(r, w, k, v, a, b, s0, zero_chunks):
    nb, tp, _ = r.shape
    assert tp % SCAN_CHUNK == 0 and SCAN_CHUNK == RW_HEAD
    nch = nb * (RW_DIM // LANES)
    seq = pl.BlockSpec((nb, SCAN_CHUNK, RW_DIM), lambda i: (0, i, 0))
    st = pl.BlockSpec((nb, RW_HEAD, RW_DIM), lambda i: (0, 0, 0))
    return pl.pallas_call(
        functools.partial(_scan_kernel, nb=nb, zero_chunks=zero_chunks),
        grid=(tp // SCAN_CHUNK,),
        in_specs=[seq] * 6 + [st],
        out_specs=[seq, st],
        out_shape=[jax.ShapeDtypeStruct((nb, tp, RW_DIM), F32),
                   jax.ShapeDtypeStruct((nb, RW_HEAD, RW_DIM), F32)],
        scratch_shapes=[pltpu.VMEM((nch, RW_HEAD, LANES), F32),
                        pltpu.VMEM((nch, RW_HEAD, LANES), F32),
                        pltpu.VMEM((8 * nch, RW_HEAD, LANES), F32),
                        pltpu.VMEM((nch, RW_HEAD, LANES), F32),
                        pltpu.VMEM((8 * nch * RW_HEAD, 2 * LANES), BF16),
                        pltpu.VMEM((4 * nch * RW_HEAD, 2 * LANES), BF16),
                        pltpu.VMEM((4 * nch * RW_HEAD, 2 * LANES), F32)],
        compiler_params=_cparams(("arbitrary",)),
        name="rwkv_scan",
    )(r, w, k, v, a, b, s0)


def _lambda_value(lam_ref):
    lv = lam_ref[...]
    s01 = jnp.sum(lv[0:1, :] * lv[1:2, :], axis=-1, keepdims=True)
    s23 = jnp.sum(lv[2:3, :] * lv[3:4, :], axis=-1, keepdims=True)
    return jnp.exp(s01) - jnp.exp(s23) + LAM_INIT


def _alibi_slope(h):
    slope = jnp.float32(2.0 ** (-8.0 * DA_HEADS / DA_HEADS))
    for i in range(DA_HEADS - 2, -1, -1):
        slope = jnp.where(h == i, jnp.float32(2.0 ** (-8.0 * (i + 1) / DA_HEADS)), slope)
    return slope


def _sub_norm(o, g):
    return o * lax.rsqrt(jnp.mean(o * o, axis=-1, keepdims=True) + NORM_EPS) * g * (1.0 - LAM_INIT)


ATT_ONES = 16


def _attn_prompt_kernel(q_ref, k_ref, v_ref, d0_ref, lam_ref, sg_ref, o_ref,
                        qc_ref, vt_ref, bias_ref, m_ref, acc_ref, s_ref, *, blk):
    h = pl.program_id(1)
    qi = pl.program_id(2)
    slope = _alibi_slope(h)
    tp = k_ref.shape[0]
    sub = blk // LANES

    @pl.when(qi == 0)
    def _():
        for jb in range(tp // LANES):
            vt = v_ref[jb * LANES:(jb + 1) * LANES, :].T.astype(BF16)
            vt_ref[jb // sub, 0:DA_DV, (jb % sub) * LANES:(jb % sub + 1) * LANES] = vt
        vt_ref[:, DA_DV:, :] = jnp.ones((tp // blk, ATT_ONES, blk), BF16)

    lane = lax.broadcasted_iota(jnp.int32, (blk, LANES), 1)
    qb = q_ref[...].astype(BF16)
    for c in range(2):
        qc_ref[c] = jnp.where((lane // DA_DK) == c, qb, jnp.zeros_like(qb))
    bias_ref[...] = slope * d0_ref[...]
    m_ref[...] = jnp.full_like(m_ref, NEG_BIG)
    acc_ref[...] = jnp.zeros_like(acc_ref)

    def scores(j):
        kb = k_ref[pl.ds(pl.multiple_of(j * blk, blk), blk), :].astype(BF16)
        for c in range(2):
            s_ref[(j % 2) * 2 + c] = lax.dot_general(kb, qc_ref[c], (((1,), (1,)), ((), ())),
                                                     preferred_element_type=F32)

    def consume(j, masked):
        off = ((qi - j) * blk).astype(F32)
        c0 = slope * off
        if masked:
            kpos = j * blk + lax.broadcasted_iota(jnp.int32, (blk, blk), 0)
            valid = (d0_ref[...] + off >= 0.0) & (kpos >= PAD_LEAD)
        for c in range(2):
            t = s_ref[(j % 2) * 2 + c] - bias_ref[...]
            if masked:
                t = jnp.where(valid, t, NEG_BIG)
            m_old = m_ref[c]
            m_new = jnp.maximum(m_old, jnp.max(t, axis=0, keepdims=True) - c0)
            alpha = jnp.exp(m_old - m_new)
            p = jnp.exp(t - (m_new + c0)).astype(BF16)
            acc_ref[c] = alpha * acc_ref[c] + jnp.dot(vt_ref[j], p, preferred_element_type=F32)
            m_ref[c] = m_new

    def middle(j, carry):
        consume(j, False)
        scores(j + 1)
        return carry

    scores(0)
    consume(0, True)
    scores(jnp.minimum(1, tp // blk - 1))
    lax.fori_loop(1, qi, middle, 0)

    @pl.when(qi > 0)
    def _():
        consume(qi, True)

    lam = _lambda_value(lam_ref)
    a0 = acc_ref[0]
    a1 = acc_ref[1]
    ot = a0[:DA_DV] / a0[DA_DV:DA_DV + 1] - lam * (a1[:DA_DV] / a1[DA_DV:DA_DV + 1])
    o_ref[...] = _sub_norm(ot.T, sg_ref[...])


def _attn_prompt(q, k, v, da_lam, subln_g, blk):
    b, tp, _ = q.shape
    assert tp % blk == 0 and blk % LANES == 0 and PAD_LEAD <= blk
    d0t = (jnp.arange(blk, dtype=F32)[None, :] - jnp.arange(blk, dtype=F32)[:, None])
    qspec = pl.BlockSpec((None, blk, DA_DV), lambda bi, h, i: (bi, i, h))
    kspec = pl.BlockSpec((None, tp, DA_DV), lambda bi, h, i: (bi, 0, h))
    full = lambda a: pl.BlockSpec(a.shape, lambda bi, h, i: (0,) * a.ndim)
    return pl.pallas_call(
        functools.partial(_attn_prompt_kernel, blk=blk),
        grid=(b, DA_HEADS, tp // blk),
        in_specs=[qspec, kspec, kspec, full(d0t), full(da_lam), full(subln_g)],
        out_specs=qspec,
        scratch_shapes=[pltpu.VMEM((2, blk, DA_DV), BF16),
                        pltpu.VMEM((tp // blk, DA_DV + ATT_ONES, blk), BF16),
                        pltpu.VMEM((blk, blk), F32), pltpu.VMEM((2, 1, blk), F32),
                        pltpu.VMEM((2, DA_DV + ATT_ONES, blk), F32),
                        pltpu.VMEM((4, blk, blk), F32)],
        out_shape=jax.ShapeDtypeStruct((b, tp, DA_V), F32),
        compiler_params=_cparams(("parallel", "parallel", "arbitrary")),
        name="attn_prompt",
    )(q, k, v, d0t, da_lam, subln_g)


def _attn_sample_kernel(pt_ref, q_ref, kn_ref, vn_ref, lam_ref, sg_ref, *rest, ppg, page, past):
    k_refs = rest[:ppg]
    v_refs = rest[ppg:2 * ppg]
    o_ref, m_ref, l_ref, acc_ref = rest[2 * ppg:]
    g = pl.program_id(1)
    nrow = 2 * DA_HEADS
    rowi = lax.broadcasted_iota(jnp.int32, (nrow, DA_QK), 0)
    lanei = lax.broadcasted_iota(jnp.int32, (nrow, DA_QK), 1)
    qsel = (lanei // DA_DK) == rowi
    qmat = jnp.where(qsel, q_ref[...], 0.0)
    rh = lax.broadcasted_iota(jnp.int32, (nrow, 1), 0) // 2
    slope = jnp.full((nrow, 1), 2.0 ** (-8.0), F32)
    for i in range(DA_HEADS - 2, -1, -1):
        slope = jnp.where(rh == i, jnp.float32(2.0 ** (-8.0 * (i + 1) / DA_HEADS)), slope)

    @pl.when(g == 0)
    def _():
        m_ref[...] = jnp.full_like(m_ref, NEG_BIG)
        l_ref[...] = jnp.zeros_like(l_ref)
        acc_ref[...] = jnp.zeros_like(acc_ref)

    qb = qmat.astype(BF16)
    kpos = g * (ppg * page) + lax.broadcasted_iota(jnp.int32, (nrow, ppg * page), 1)
    s = jnp.concatenate([jnp.dot(qb, k_refs[i][...].astype(BF16), preferred_element_type=F32)
                         for i in range(ppg)], axis=1)
    s = s - slope * (past - kpos).astype(F32)
    m_old = m_ref[...]
    m_new = jnp.maximum(m_old, jnp.max(s, axis=-1, keepdims=True))
    alpha = jnp.exp(m_old - m_new)
    p = jnp.exp(s - m_new)
    l_ref[...] = alpha * l_ref[...] + jnp.sum(p, axis=-1, keepdims=True)
    pb = p.astype(BF16)
    pv = [jnp.zeros((nrow, DA_DV), F32)] * DA_HEADS
    for i in range(ppg):
        for hh in range(DA_HEADS):
            vh = v_refs[i][pl.ds(hh, page, stride=DA_HEADS), :].astype(BF16)
            pv[hh] = pv[hh] + jnp.dot(pb[:, i * page:(i + 1) * page], vh, preferred_element_type=F32)
    tot = pv[DA_HEADS - 1]
    for hh in range(DA_HEADS - 1):
        tot = jnp.where(rh == hh, pv[hh], tot)
    acc_ref[...] = alpha * acc_ref[...] + tot
    m_ref[...] = m_new

    @pl.when(g == pl.num_programs(1) - 1)
    def _():
        kn = kn_ref[...].astype(BF16).astype(F32)
        s_self = jnp.sum(qb.astype(F32) * kn, axis=-1, keepdims=True)
        m_old = m_ref[...]
        m_new = jnp.maximum(m_old, s_self)
        alpha = jnp.exp(m_old - m_new)
        p_self = jnp.exp(s_self - m_new)
        l_fin = alpha * l_ref[...] + p_self
        vn = vn_ref[...].astype(BF16).astype(F32)
        vrow = jnp.zeros((nrow, DA_DV), F32)
        for hh in range(DA_HEADS):
            vrow = jnp.where(rh == hh, vn[:, hh * DA_DV:(hh + 1) * DA_DV], vrow)
        acc = alpha * acc_ref[...] + p_self.astype(BF16).astype(F32) * vrow
        lam = _lambda_value(lam_ref)
        ri = lax.broadcasted_iota(jnp.int32, (nrow, 1), 0)
        t = acc / l_fin * jnp.where(ri % 2 == 0, 1.0, -lam)
        outs = []
        for hh in range(DA_HEADS):
            o = t[2 * hh:2 * hh + 1, :] + t[2 * hh + 1:2 * hh + 2, :]
            outs.append(_sub_norm(o, sg_ref[...]))
        o_ref[...] = jnp.concatenate(outs, axis=1)


def _attn_sample(q, kn, vn, cache_kt, cache_v, page_table, da_lam, subln_g, ppg):
    bd = q.shape[0]
    n_pages = page_table.shape[1]
    page = cache_kt.shape[2]
    assert n_pages % ppg == 0 and cache_v.shape[1:] == (page * DA_HEADS, DA_DV)
    pt = page_table.reshape(-1)
    one = pl.BlockSpec((None, 1, DA_QK), lambda b, g, pt: (b, 0, 0))
    full = lambda a: pl.BlockSpec(a.shape, lambda b, g, pt: (0,) * a.ndim)

    def pspec(i):
        return pl.BlockSpec((None, DA_QK, page),
                            lambda b, g, pt, i=i: (pt[b * n_pages + g * ppg + i], 0, 0))

    out = pl.pallas_call(
        functools.partial(_attn_sample_kernel, ppg=ppg, page=page, past=n_pages * page),
        grid_spec=pltpu.PrefetchScalarGridSpec(
            num_scalar_prefetch=1,
            grid=(bd, n_pages // ppg),
            in_specs=[one, one, one, full(da_lam), full(subln_g)]
                     + [pspec(i) for i in range(ppg)] * 2,
            out_specs=one,
            scratch_shapes=[pltpu.VMEM((2 * DA_HEADS, 1), F32), pltpu.VMEM((2 * DA_HEADS, 1), F32),
                            pltpu.VMEM((2 * DA_HEADS, DA_DV), F32)],
        ),
        out_shape=jax.ShapeDtypeStruct((bd, 1, DA_V), F32),
        compiler_params=_cparams(("parallel", "arbitrary")),
        name="attn_sample",
    )(pt, q.reshape(bd, 1, DA_QK), kn.reshape(bd, 1, DA_QK), vn.reshape(bd, 1, DA_V), da_lam, subln_g,
      *([cache_kt] * ppg), *([cache_v] * ppg))
    return out.reshape(bd, DA_V)


def _merge_kernel(*refs, nsub):
    rows = [refs[k * nsub:(k + 1) * nsub] for k in range(6)]
    (lng_ref, lnb_ref, wrw_ref, wda_ref, wout_ref, n2_ref, wq_ref, sk_ref,
     h2_ref, xn_ref, st_ref) = refs[6 * nsub:]
    stack = lambda rs: rs[0][...] if nsub == 1 else jnp.concatenate([r[...] for r in rs], axis=0)
    y, bonus, g, o, gt, h = [stack(rs) for rs in rows]
    j = _seg_ones()
    mu = _seg64(y, j) * (1.0 / RW_HEAD)
    d = y - mu
    var = _seg64(d * d, j) * (1.0 / RW_HEAD)
    yln = d * lax.rsqrt(var + RW_LN_EPS) * lng_ref[...] + lnb_ref[...]
    yr = ((yln + bonus) * g).astype(BF16)
    y_rw = jnp.dot(yr, wrw_ref[...], preferred_element_type=F32)
    y_da = jnp.dot(o.astype(BF16), wda_ref[...], preferred_element_type=F32)
    mix = gt[:, :D_MODEL] * y_rw + gt[:, D_MODEL:] * y_da
    h2 = h + jnp.dot(mix.astype(BF16), wout_ref[...], preferred_element_type=F32)
    h2_ref[...] = h2
    ms = jnp.mean(h2 * h2, axis=-1, keepdims=True)
    xn = (h2 * lax.rsqrt(ms + NORM_EPS) * n2_ref[...]).astype(BF16)
    xn_ref[...] = xn
    q = jnp.dot(xn, wq_ref[...], preferred_element_type=F32)
    for hc in range(2 * PEER_HEADS):
        qc = q[:, hc * PEER_DHALF:(hc + 1) * PEER_DHALF].astype(BF16)
        st_ref[hc * N_KEYS:(hc + 1) * N_KEYS, :] = lax.dot_general(
            sk_ref[hc % 2], qc, (((1,), (1,)), ((), ())), preferred_element_type=F32)


def _merge(y, bonus, g, o, gates, h, consts, tm, tile_off, nsub):
    b, rows, _ = y.shape
    assert rows % tm == 0 and (rows // tm - tile_off) % nsub == 0
    nt = (rows // tm - tile_off) // nsub
    full = lambda a: pl.BlockSpec(a.shape, lambda i, t: (0,) * a.ndim)

    def rin(c):
        return [pl.BlockSpec((None, tm, c), lambda i, t, k=k: (i, t * nsub + k + tile_off, 0))
                for k in range(nsub)]

    row_arrays = (y, bonus, g, o, gates, h)
    tmo = tm * nsub
    ntok = b * nt * tmo
    return pl.pallas_call(
        functools.partial(_merge_kernel, nsub=nsub),
        grid=(b, nt),
        in_specs=[s for a in row_arrays for s in rin(a.shape[-1])] + [full(a) for a in consts],
        out_specs=[pl.BlockSpec((tmo, D_MODEL), lambda i, t: (i * nt + t, 0)),
                   pl.BlockSpec((tmo, D_MODEL), lambda i, t: (i * nt + t, 0)),
                   pl.BlockSpec((2 * PEER_HEADS * N_KEYS, tmo), lambda i, t: (0, i * nt + t))],
        out_shape=[jax.ShapeDtypeStruct((ntok, D_MODEL), F32),
                   jax.ShapeDtypeStruct((ntok, D_MODEL), BF16),
                   jax.ShapeDtypeStruct((2 * PEER_HEADS * N_KEYS, ntok), F32)],
        compiler_params=_cparams(("parallel", "parallel")),
        name="merge",
    )(*[a for a in row_arrays for _ in range(nsub)], *consts)


def _count(mask):
    return jnp.sum(jnp.where(mask, 1.0, 0.0), axis=0, keepdims=True)


def _top16(x, exact):
    nk = x.shape[0]
    iota = lax.broadcasted_iota(jnp.int32, x.shape, 0).astype(F32)
    rank = jnp.full(x.shape, float(nk), F32)
    vals = []
    for r in range(PEER_TOPK):
        m = jnp.max(x, axis=0, keepdims=True)
        sel = x == m
        if exact:
            cand = jnp.where(sel, iota, float(nk))
            sel = cand == jnp.min(cand, axis=0, keepdims=True)
        x = jnp.where(sel, -jnp.inf, x)
        rank = jnp.where(sel, float(r), rank)
        vals.append(m)
    return vals, rank


def _peer_topk_kernel(st_ref, rank2_ref, c2_ref, cnt1_ref, c1_ref):
    n = st_ref.shape[1]
    rid = lax.broadcasted_iota(jnp.int32, (_CAND_ROWS, n), 0)
    flat = jnp.zeros((_CAND_ROWS, n), jnp.int32) + PEER_TOPK * PEER_TOPK
    irow = jnp.zeros((_CAND_ROWS, n), jnp.int32) + PEER_TOPK
    for ci, (i, j) in enumerate(_CAND):
        flat = jnp.where(rid == ci, i * PEER_TOPK + j, flat)
        irow = jnp.where(rid == ci, i, irow)
    outs = (rank2_ref, c2_ref, cnt1_ref, c1_ref)
    ties = _peer_topk_body(st_ref, outs, flat, irow, exact=False)

    @pl.when(jnp.max(ties) > 0.0)
    def _():
        _peer_topk_body(st_ref, outs, flat, irow, exact=True)


def _peer_topk_body(st_ref, outs, flat, irow, exact):
    rank2_ref, c2_ref, cnt1_ref, c1_ref = outs
    n = st_ref.shape[1]
    ties = jnp.zeros((1, n), F32)
    for h in range(PEER_HEADS):
        s1 = st_ref[(2 * h) * N_KEYS:(2 * h + 1) * N_KEYS, :]
        s2 = st_ref[(2 * h + 1) * N_KEYS:(2 * h + 2) * N_KEYS, :]
        v1, rank1 = _top16(s1, exact)
        v2, rank2 = _top16(s2, exact)
        rows = [v1[i] + v2[j] for (i, j) in _CAND]
        rows += [jnp.full((1, n), -jnp.inf, F32)] * (_CAND_ROWS - len(_CAND))
        cand = jnp.concatenate(rows, axis=0)
        top = v1[0] + v2[0]
        chosen = jnp.zeros(cand.shape, jnp.bool_)
        work = cand
        for _ in range(PEER_TOPK):
            m = jnp.max(work, axis=0, keepdims=True)
            sel = work == m
            if exact:
                fl = jnp.where(sel, flat, PEER_TOPK * PEER_TOPK + 1)
                sel = fl == jnp.min(fl, axis=0, keepdims=True)
            work = jnp.where(sel, -jnp.inf, work)
            chosen = chosen | sel
        if not exact:
            k = float(PEER_TOPK)
            bad = (_count(rank1 < k) != k) | (_count(rank2 < k) != k) | (_count(chosen) != k)
            ties = jnp.maximum(ties, jnp.where(bad, 1.0, 0.0))
        z = jnp.sum(jnp.where(chosen, jnp.exp(cand - top), 0.0), axis=0, keepdims=True)
        rank1_b = rank1.astype(BF16)
        cnt1 = jnp.zeros(s1.shape, BF16)
        for i in range(PEER_TOPK):
            m_i = jnp.sum(jnp.where(chosen & (irow == i), 1.0, 0.0), axis=0, keepdims=True)
            cnt1 = jnp.where(rank1_b == float(i), m_i.astype(BF16), cnt1)
        cnt1_ref[h] = cnt1.astype(F32)
        c1_ref[h] = jnp.where(rank1 < float(PEER_TOPK), jnp.exp(s1 - v1[0]) / z, 0.0)
        c2_ref[h] = jnp.where(rank2 < float(PEER_TOPK), jnp.exp(s2 - v2[0]), 0.0).astype(BF16)
        rank2_ref[h] = rank2.astype(BF16)
    return ties


def _peer_topk(st, tn):
    ntok = st.shape[1]
    assert ntok % tn == 0
    spec = pl.BlockSpec((PEER_HEADS, N_KEYS, tn), lambda i: (0, 0, i))
    return pl.pallas_call(
        _peer_topk_kernel,
        grid=(ntok // tn,),
        in_specs=[pl.BlockSpec((st.shape[0], tn), lambda i: (0, i))],
        out_specs=[spec] * 4,
        out_shape=[jax.ShapeDtypeStruct((PEER_HEADS, N_KEYS, ntok), dt) for dt in (BF16, BF16, F32, F32)],
        compiler_params=_cparams(("parallel",)),
        name="peer_topk",
    )(st)


def _peer_dense_kernel(x_ref, u_ref, vt_ref, rank2_ref, c2_ref, cnt1_ref, c1_ref, h2_ref, y_ref,
                       acc_ref, hid_ref, p_ref, *, e1b):
    j = pl.program_id(1)
    tn = x_ref.shape[0]
    ngrp = N_KEYS // BF16_ROWS

    @pl.when(j == 0)
    def _():
        acc_ref[...] = jnp.zeros_like(acc_ref)

    hid_ref[...] = lax.dot_general(u_ref[...], x_ref[...], (((1,), (1,)), ((), ())),
                                   preferred_element_type=F32)

    for e in range(e1b):
        e1 = j * e1b + e
        gate = [None] * ngrp
        for h in range(PEER_HEADS):
            cnt = jnp.broadcast_to(cnt1_ref[h, pl.ds(e1, 1), :], (BF16_ROWS, tn)).astype(BF16)
            c1 = jnp.broadcast_to(c1_ref[h, pl.ds(e1, 1), :], (BF16_ROWS, tn)).astype(BF16)
            for g in range(ngrp):
                rows = slice(g * BF16_ROWS, (g + 1) * BF16_ROWS)
                c2 = c2_ref[h, rows, :]
                term = jnp.where(rank2_ref[h, rows, :] < cnt, c2, jnp.zeros_like(c2)) * c1
                gate[g] = term if gate[g] is None else gate[g] + term
        hh = hid_ref[e * N_KEYS:(e + 1) * N_KEYS, :]
        gelu = (0.5 * hh * (1.0 + lax.erf(hh * (2.0 ** -0.5)))).astype(BF16)
        for g in range(ngrp):
            r0 = e * N_KEYS + g * BF16_ROWS
            p_ref[r0:r0 + BF16_ROWS, :] = gelu[g * BF16_ROWS:(g + 1) * BF16_ROWS] * gate[g]
    acc_ref[...] += jnp.dot(vt_ref[...], p_ref[...], preferred_element_type=F32)

    @pl.when(j == pl.num_programs(1) - 1)
    def _():
        y_ref[...] = h2_ref[...] + acc_ref[...].T


def _peer_dense(xn, u_bf, vt_bf, rank2, c2, cnt1, c1, h2, tn, e1b):
    ntok = xn.shape[0]
    assert ntok % tn == 0 and N_KEYS % e1b == 0
    tok = pl.BlockSpec((PEER_HEADS, N_KEYS, tn), lambda i, j: (0, 0, i))
    row = pl.BlockSpec((tn, D_MODEL), lambda i, j: (i, 0))
    return pl.pallas_call(
        functools.partial(_peer_dense_kernel, e1b=e1b),
        grid=(ntok // tn, N_KEYS // e1b),
        in_specs=[row,
                  pl.BlockSpec((e1b * N_KEYS, D_MODEL), lambda i, j: (j, 0)),
                  pl.BlockSpec((D_MODEL, e1b * N_KEYS), lambda i, j: (0, j)),
                  tok, tok, tok, tok, row],
        out_specs=row,
        out_shape=jax.ShapeDtypeStruct((ntok, D_MODEL), F32),
        scratch_shapes=[pltpu.VMEM((D_MODEL, tn), F32), pltpu.VMEM((e1b * N_KEYS, tn), F32),
                        pltpu.VMEM((e1b * N_KEYS, tn), BF16)],
        compiler_params=_cparams(("parallel", "arbitrary")),
        name="peer_dense",
    )(xn, u_bf, vt_bf, rank2, c2, cnt1, c1, h2)


def _state_in(s):
    b = s.shape[0]
    return jnp.transpose(s, (0, 2, 1, 3)).reshape(b, RW_HEAD, RW_DIM)


def _state_out(s):
    b = s.shape[0]
    return jnp.transpose(s.reshape(b, RW_HEAD, RW_HEADS, RW_HEAD), (0, 2, 1, 3))


def _pad_tokens(a, n, axis):
    pad = [(0, 0)] * a.ndim
    pad[axis] = (0, n - a.shape[axis])
    return jnp.pad(a, pad)


def kernel(x_prompt, x_sample, cache_k, cache_v, state_wkv, state_shift, page_table, meta, norm1_g, w_in,
           rw_mu, rw_w0, rw_w2, rw_a0, rw_a2, rw_g2, rw_kk, rw_ka, rw_rk, rw_ln_g, rw_ln_b, w_rw_br,
           da_qn_g, da_kn_g, da_lam, da_subln_g, w_da_br, w_out, norm2_g, peer_wq, peer_subkeys,
           peer_u, peer_v):
    b, seq, _ = x_prompt.shape
    bd = x_sample.shape[0]
    t_real = N_META + seq
    tp = PAD_LEAD + t_real
    assert tp % ROW_TILE == 0 and tp % SCAN_CHUNK == 0 and x_sample.shape[1] == 1
    row2 = lambda a: a.reshape(1, -1)

    w_in_bf = w_in[0].astype(BF16)
    g1 = row2(norm1_g[0])
    qg = row2(jnp.tile(da_qn_g[0], DA_QK // DA_DK))
    kg = row2(jnp.tile(da_kn_g[0], DA_QK // DA_DK))
    zlo = jnp.zeros((W_LORA, RW_DIM), F32)
    prep_params = (row2(rw_mu[0]), row2(rw_w0[0]),
                   jnp.concatenate([rw_w2[0], zlo], axis=0).astype(BF16), row2(rw_a0[0]),
                   jnp.concatenate([zlo, rw_a2[0]], axis=0).astype(BF16), rw_g2[0].astype(BF16),
                   row2(rw_kk[0]), row2(rw_ka[0]), row2(rw_rk[0]))
    merge_consts = (row2(rw_ln_g[0]), row2(rw_ln_b[0]), w_rw_br[0].astype(BF16), w_da_br[0].astype(BF16),
                    w_out[0].astype(BF16), row2(norm2_g[0]), peer_wq[0].astype(BF16),
                    peer_subkeys[0].astype(BF16))
    subln = row2(da_subln_g[0])
    u_bf = peer_u[0].astype(BF16)
    vt_bf = peer_v[0].astype(BF16).T

    hp = jnp.concatenate([jnp.zeros((b, PAD_LEAD, D_MODEL), F32),
                          jnp.broadcast_to(meta[None], (b, N_META, D_MODEL)), x_prompt], axis=1)
    prw, q, k, v, gates = _in_proj(hp.reshape(b * tp, D_MODEL), g1, w_in_bf, qg, kg, IN_PROJ_ROWS)
    r3 = lambda a: a.reshape(b, tp, -1)
    pr, pw, pk, pv, pa, pb, pg, pbonus = _rwkv_prep_prompt(r3(prw), prep_params, PREP_ROWS)
    y_scan, s_fin = _rwkv_scan(pr, pw, pk, pv, pa, pb, jnp.zeros((b, RW_HEAD, RW_DIM), F32),
                               PAD_LEAD // SCAN_CHUNK)
    o_attn = _attn_prompt(r3(q), r3(k), r3(v), da_lam[0], subln, ATTN_BLOCK)
    h2, xn2, st = _merge(y_scan, pbonus, pg, o_attn, r3(gates), hp, merge_consts, ROW_TILE, 1, MERGE_STACK)
    y_prompt = _peer_dense(xn2, u_bf, vt_bf, *_peer_topk(st, TOPK_TOKENS), h2, DENSE_TOKENS, DENSE_E1)

    xs = x_sample.reshape(bd, D_MODEL)
    prw_s, q_s, k_s, v_s, gates_s = _in_proj(xs, g1, w_in_bf, qg, kg, bd)
    sr, sw, sk, sv, sa, sb, sg, sbonus = _rwkv_prep_sample(prw_s, state_shift[0], prep_params)
    ys_scan, ss_fin = _rwkv_step(sr, sw, sk, sv, sa, sb, _state_in(state_wkv[0]))
    n_phys = cache_k.shape[1]
    ckt = jnp.transpose(cache_k[0].reshape(n_phys, -1, DA_QK), (0, 2, 1))
    cv = cache_v[0].reshape(n_phys, -1, DA_DV)
    o_s = _attn_sample(q_s, k_s, v_s, ckt, cv, page_table, da_lam[0], subln, PAGES_PER_STEP)
    f3 = lambda a: a.reshape(1, bd, -1)
    h2s, xn2s, sts = _merge(ys_scan.reshape(1, bd, RW_DIM), f3(sbonus), f3(sg), f3(o_s), f3(gates_s),
                            f3(xs), merge_consts, bd, 0, 1)
    npad = LANES
    tk_s = _peer_topk(_pad_tokens(sts, npad, 1), npad)
    y_s = _peer_dense(_pad_tokens(xn2s, npad, 0), u_bf, vt_bf, *tk_s, _pad_tokens(h2s, npad, 0), npad, DENSE_E1)

    y_prompt = y_prompt.reshape(b, seq, D_MODEL)
    y_sample = y_s[:bd].reshape(bd, 1, D_MODEL)
    k_p = r3(k)[:, PAD_LEAD:].reshape(1, b, t_real, DA_HEADS, 2, DA_DK)
    v_p = r3(v)[:, PAD_LEAD:].reshape(1, b, t_real, DA_HEADS, DA_DV)
    return (y_prompt, y_sample, k_p, v_p,
            k_s.reshape(1, bd, 1, DA_HEADS, 2, DA_DK), v_s.reshape(1, bd, 1, DA_HEADS, DA_DV),
            _state_out(s_fin)[None], _state_out(ss_fin)[None],
            r3(prw)[:, -1][None], prw_s[None])
```

```python
import functools
import math

import jax
import jax.numpy as jnp
from jax import lax
from jax.experimental import pallas as pl
from jax.experimental.pallas import tpu as pltpu

F32 = jnp.float32
BF16 = jnp.bfloat16

D_MODEL = 1024
N_META = 16
NORM_EPS = 1e-6
RW_HEAD = 64
RW_DIM = D_MODEL // 2
RW_HEADS = RW_DIM // RW_HEAD
W_LORA = 64
A_LORA = 64
G_LORA = 128
RW_PROJ = 3 * RW_DIM + W_LORA + A_LORA + G_LORA
RW_LN_EPS = 64e-5
DA_DK = 64
DA_DV = 2 * DA_DK
DA_DIM = D_MODEL // 2
DA_HEADS = DA_DIM // DA_DV
DA_QK = DA_HEADS * 2 * DA_DK
DA_V = DA_HEADS * DA_DV
N_KEYS = 128
PEER_HEADS = 8
PEER_TOPK = 16
PEER_DQ = 256
PEER_DHALF = PEER_DQ // 2
LAM_INIT = 0.8 - 0.6 * math.exp(-0.3 * 0)

LANES = 128
BF16_ROWS = 16
ROW_TILE = 128
PAD_LEAD = ROW_TILE - N_META
SCAN_CHUNK = 64
NEG_BIG = -1e30
VMEM_LIMIT = 56 * 1024 * 1024

IN_PROJ_ROWS = 512
PREP_ROWS = 384
ATTN_BLOCK = 384
MERGE_STACK = 4
TOPK_TOKENS = 128
DENSE_TOKENS = 512
DENSE_E1 = 16
PAGES_PER_STEP = 16

O1 = RW_PROJ
O2 = O1 + DA_QK
O3 = O2 + DA_QK
O4 = O3 + DA_V
O5 = O4 + D_MODEL
O6 = O5 + D_MODEL

_CAND = [(i, j) for i in range(PEER_TOPK) for j in range(PEER_TOPK) if (i + 1) * (j + 1) <= PEER_TOPK]
_CAND_ROWS = -(-len(_CAND) // 8) * 8


def _cparams(sem):
    return pltpu.CompilerParams(dimension_semantics=sem, vmem_limit_bytes=VMEM_LIMIT)


def _split_dot(x, j):
    hi = x.astype(BF16)
    lo = (x - hi.astype(F32)).astype(BF16)
    return (jnp.dot(hi, j, preferred_element_type=F32)
            + jnp.dot(lo, j, preferred_element_type=F32))


def _seg64(x, j):
    outs = [_split_dot(x[:, c * LANES:(c + 1) * LANES], j) for c in range(x.shape[1] // LANES)]
    return outs[0] if len(outs) == 1 else jnp.concatenate(outs, axis=1)


def _seg_ones():
    r = lax.broadcasted_iota(jnp.int32, (LANES, LANES), 0) // 64
    c = lax.broadcasted_iota(jnp.int32, (LANES, LANES), 1) // 64
    return (r == c).astype(BF16)


def _in_proj_kernel(x_ref, g1_ref, w_ref, qg_ref, kg_ref, prw_ref, q_ref, k_ref, v_ref, gt_ref):
    x = x_ref[...]
    ms = jnp.mean(x * x, axis=-1, keepdims=True)
    xn = (x * lax.rsqrt(ms + NORM_EPS) * g1_ref[...]).astype(BF16)

    def mm(a, b):
        return jnp.dot(xn, w_ref[:, a:b], preferred_element_type=F32)

    j = _seg_ones()

    def head_norm(t, g):
        msq = _seg64(t * t, j) * (1.0 / DA_DK)
        return t * lax.rsqrt(msq + NORM_EPS) * g

    prw_ref[...] = mm(0, O1)
    q_ref[...] = head_norm(mm(O1, O2), qg_ref[...]) * (DA_DK ** -0.5)
    k_ref[...] = head_norm(mm(O2, O3), kg_ref[...])
    v_ref[...] = mm(O3, O4)
    gt_ref[...] = jax.nn.sigmoid(mm(O4, O6))


def _in_proj(x, g1, w_bf, qg, kg, tm):
    n = x.shape[0]
    assert n % tm == 0
    row = lambda c: pl.BlockSpec((tm, c), lambda i: (i, 0))
    full = lambda a: pl.BlockSpec(a.shape, lambda i: (0,) * a.ndim)
    widths = (RW_PROJ, DA_QK, DA_QK, DA_V, 2 * D_MODEL)
    return pl.pallas_call(
        _in_proj_kernel,
        grid=(n // tm,),
        in_specs=[row(D_MODEL), full(g1), full(w_bf), full(qg), full(kg)],
        out_specs=[row(c) for c in widths],
        out_shape=[jax.ShapeDtypeStruct((n, c), F32) for c in widths],
        compiler_params=_cparams(("parallel",)),
        name="in_proj",
    )(x, g1, w_bf, qg, kg)


def _rwkv_prep_math(p, p_prev, mu, w0, w2p, a0, a2p, g2, kkp, kap, rk):
    ps = p + (p_prev - p) * mu
    r = ps[:, 0:RW_DIM]
    k = ps[:, RW_DIM:2 * RW_DIM]
    v = ps[:, 2 * RW_DIM:3 * RW_DIM]
    wa = ps[:, 3 * RW_DIM:3 * RW_DIM + W_LORA + A_LORA]
    gl = ps[:, 3 * RW_DIM + W_LORA + A_LORA:RW_PROJ]
    w = -jax.nn.softplus(-(w0 + jnp.dot(jnp.tanh(wa).astype(BF16), w2p, preferred_element_type=F32))) - 0.5
    decay = jnp.exp(-jnp.exp(w))
    a = jax.nn.sigmoid(a0 + jnp.dot(wa.astype(BF16), a2p, preferred_element_type=F32))
    g = jnp.dot(jax.nn.sigmoid(gl).astype(BF16), g2, preferred_element_type=F32)
    j = _seg_ones()
    kk = k * kkp
    kk = kk / jnp.maximum(jnp.sqrt(_seg64(kk * kk, j)), 1e-12)
    k2 = k * (1.0 + (a - 1.0) * kap)
    bonus = _seg64(r * k2 * rk, j) * v
    return r, decay, k2, v, -kk, kk * a, g, bonus


def _rwkv_prep_carry_kernel(p_ref, mu, w0, w2p, a0, a2p, g2, kkp, kap, rk, *rest):
    outs, carry = rest[:-1], rest[-1]

    @pl.when(pl.program_id(1) == 0)
    def _():
        carry[...] = jnp.zeros_like(carry)

    p = p_ref[...]
    rows = lax.broadcasted_iota(jnp.int32, p.shape, 0)
    p_prev = jnp.where(rows == 0, carry[...], pltpu.roll(p, 1, 0))
    carry[...] = p[p.shape[0] - 1:, :]
    res = _rwkv_prep_math(p, p_prev, mu[...], w0[...], w2p[...], a0[...], a2p[...], g2[...],
                          kkp[...], kap[...], rk[...])
    for o, val in zip(outs, res):
        o[...] = val


def _rwkv_prep_given_kernel(p_ref, pp_ref, mu, w0, w2p, a0, a2p, g2, kkp, kap, rk, *outs):
    res = _rwkv_prep_math(p_ref[...], pp_ref[...], mu[...], w0[...], w2p[...], a0[...], a2p[...],
                          g2[...], kkp[...], kap[...], rk[...])
    for o, val in zip(outs, res):
        o[...] = val


def _rwkv_prep_prompt(p3, params, tm):
    b, tp, _ = p3.shape
    assert tp % tm == 0
    full = lambda a: pl.BlockSpec(a.shape, lambda i, j: (0,) * a.ndim)
    return pl.pallas_call(
        _rwkv_prep_carry_kernel,
        grid=(b, tp // tm),
        in_specs=[pl.BlockSpec((None, tm, RW_PROJ), lambda i, j: (i, j, 0))] + [full(a) for a in params],
        out_specs=[pl.BlockSpec((None, tm, RW_DIM), lambda i, j: (i, j, 0))] * 8,
        out_shape=[jax.ShapeDtypeStruct((b, tp, RW_DIM), F32)] * 8,
        scratch_shapes=[pltpu.VMEM((1, RW_PROJ), F32)],
        compiler_params=_cparams(("arbitrary", "arbitrary")),
        name="rwkv_prep_prompt",
    )(p3, *params)


def _rwkv_prep_sample(p, p_prev, params):
    n = p.shape[0]
    full = lambda a: pl.BlockSpec(a.shape, lambda i: (0,) * a.ndim)
    return pl.pallas_call(
        _rwkv_prep_given_kernel,
        grid=(1,),
        in_specs=[full(p), full(p_prev)] + [full(a) for a in params],
        out_specs=[pl.BlockSpec((n, RW_DIM), lambda i: (0, 0))] * 8,
        out_shape=[jax.ShapeDtypeStruct((n, RW_DIM), F32)] * 8,
        compiler_params=_cparams(("arbitrary",)),
        name="rwkv_prep_sample",
    )(p, p_prev, *params)


def _scan_consts():
    lane = lax.broadcasted_iota(jnp.int32, (RW_HEAD, LANES), 1)
    sub = lax.broadcasted_iota(jnp.int32, (RW_HEAD, LANES), 0)
    first = lane < RW_HEAD
    eye_lo = (lane == sub).astype(F32)
    eye_hi = (lane - RW_HEAD == sub).astype(F32)
    return lane, first, eye_lo, eye_hi


def _seg(first, x_lo, x_hi):
    lo = jnp.sum(x_lo, axis=-1, keepdims=True)
    hi = jnp.sum(x_hi, axis=-1, keepdims=True)
    return jnp.where(first, lo, hi)


def _split_row(first, row):
    lo = jnp.where(first[0:1, :], row, 0.0)
    return lo, row - lo


def _scan_step_kernel(r_ref, w_ref, k_ref, v_ref, a_ref, b_ref, s0_ref, y_ref, s_ref, *, nb):
    _, first, eye_lo, eye_hi = _scan_consts()
    for b in range(nb):
        for p in range(RW_DIM // LANES):
            cols = pl.ds(p * LANES, LANES)
            r_t, w_t, k_t, v_t, a_t, b_t = [ref[pl.ds(b, 1), cols]
                                            for ref in (r_ref, w_ref, k_ref, v_ref, a_ref, b_ref)]
            s = s0_ref[b, :, cols]
            a_lo, a_hi = _split_row(first, a_t)
            r_lo, r_hi = _split_row(first, r_t)
            sa = _seg(first, s * a_lo, s * a_hi)
            vcol = _seg(first, eye_lo * v_t, eye_hi * v_t)
            s = s * w_t + sa * b_t + vcol * k_t
            s_ref[b, :, cols] = s
            ycol = _seg(first, s * r_lo, s * r_hi)
            y_ref[pl.ds(b, 1), cols] = jnp.sum((eye_lo + eye_hi) * ycol, axis=0, keepdims=True)


def _rwkv_step(r, w, k, v, a, b, s0):
    nb = r.shape[0]
    full = lambda x: pl.BlockSpec(x.shape, lambda i: (0,) * x.ndim)
    return pl.pallas_call(
        functools.partial(_scan_step_kernel, nb=nb),
        grid=(1,),
        in_specs=[full(r)] * 6 + [full(s0)],
        out_specs=[full(r), full(s0)],
        out_shape=[jax.ShapeDtypeStruct(r.shape, F32), jax.ShapeDtypeStruct(s0.shape, F32)],
        compiler_params=_cparams(("arbitrary",)),
        name="rwkv_step",
    )(r, w, k, v, a, b, s0)


def _scan_kernel(r_ref, w_ref, k_ref, v_ref, a_ref, b_ref, s0_ref, y_ref, sfin_ref,
                 s_scr, sa_scr, vc_scr, yacc_scr, vstage, zstage, yres, *, nb, zero_chunks):
    npair = RW_DIM // LANES
    chains = [(b, p) for b in range(nb) for p in range(npair)]

    @pl.when(pl.program_id(0) == 0)
    def _():
        for ci, (b, p) in enumerate(chains):
            s_scr[ci] = s0_ref[b, :, p * LANES:(p + 1) * LANES]

    @pl.when(pl.program_id(0) < zero_chunks)
    def _():
        y_ref[...] = jnp.zeros_like(y_ref)
        sfin_ref[...] = s0_ref[...]

    @pl.when(pl.program_id(0) >= zero_chunks)
    def _():
        _scan_chunk(r_ref, w_ref, k_ref, v_ref, a_ref, b_ref, y_ref, sfin_ref,
                    s_scr, sa_scr, vc_scr, yacc_scr, vstage, zstage, yres, chains)


def _scan_chunk(r_ref, w_ref, k_ref, v_ref, a_ref, b_ref, y_ref, sfin_ref,
                s_scr, sa_scr, vc_scr, yacc_scr, vstage, zstage, yres, chains):
    lane, first, eye_lo, eye_hi = _scan_consts()
    sub_t = 8
    yacc_scr[...] = jnp.zeros_like(yacc_scr)
    nch = len(chains)
    eye2 = eye_lo + eye_hi
    seg_ones = _seg_ones()
    zero_b = jnp.zeros((LANES, LANES), BF16)
    j_stack = jnp.concatenate([seg_ones, seg_ones], axis=0)
    j_pair = jnp.concatenate([jnp.concatenate([seg_ones, zero_b], axis=1),
                              jnp.concatenate([zero_b, seg_ones], axis=1)], axis=0)

    def prepare(ci, s, a_t):
        a_lo, a_hi = _split_row(first, a_t)
        sa_scr[ci] = _seg(first, s * a_lo, s * a_hi)

    def group(tg, carry):
        rows = pl.ds(pl.multiple_of(tg * sub_t, sub_t), sub_t)

        def row(ref, ci, i):
            b, p = chains[ci]
            return ref[b, rows, pl.ds(p * LANES, LANES)][i:i + 1, :]

        for i in range(sub_t):
            for ci in range(nch):
                v_t = row(v_ref, ci, i)
                v_hi = v_t.astype(BF16).astype(F32)
                lhs = jnp.concatenate([eye2 * v_hi, eye2 * (v_t - v_hi)], axis=1)
                vstage[pl.ds((i * nch + ci) * RW_HEAD, RW_HEAD), :] = lhs.astype(BF16)
        vc_scr[...] = jnp.dot(vstage[...], j_stack, preferred_element_type=F32).reshape(vc_scr.shape)

        for ci in range(nch):
            prepare(ci, s_scr[ci], row(a_ref, ci, 0))
        for i in range(sub_t):
            for ci in range(nch):
                s = s_scr[ci] * row(w_ref, ci, i) + sa_scr[ci] * row(b_ref, ci, i) \
                    + vc_scr[i * nch + ci] * row(k_ref, ci, i)
                s_scr[ci] = s
                if i + 1 < sub_t:
                    prepare(ci, s, row(a_ref, ci, i + 1))
                z = (s * row(r_ref, ci, i)).astype(BF16)
                zstage[pl.ds((i * (nch // 2) + ci // 2) * RW_HEAD, RW_HEAD),
                       pl.ds((ci % 2) * LANES, LANES)] = z
        yres[...] = jnp.dot(zstage[...], j_pair, preferred_element_type=F32)
        for i in range(sub_t):
            place = (lane % RW_HEAD) == tg * sub_t + i
            for ci in range(nch):
                ycol = yres[pl.ds((i * (nch // 2) + ci // 2) * RW_HEAD, RW_HEAD),
                            pl.ds((ci % 2) * LANES, LANES)]
                yacc_scr[ci] = jnp.where(place, ycol, yacc_scr[ci])
        return carry

    lax.fori_loop(0, SCAN_CHUNK // sub_t, group, 0)

    zeros = jnp.zeros((LANES - RW_HEAD, LANES), F32)
    for ci, (b, p) in enumerate(chains):
        cols = pl.ds(p * LANES, LANES)
        tr = jnp.concatenate([yacc_scr[ci], zeros], axis=0).T
        y_ref[b, :, cols] = jnp.where(first, tr[0:RW_HEAD], pltpu.roll(tr[RW_HEAD:], RW_HEAD, 1))
        sfin_ref[b, :, cols] = s_scr[ci]


def _rwkv_scan(r, w, k, v, a, b, s0, zero_chunks):
    nb, tp, _ = r.shape
    assert tp % SCAN_CHUNK == 0 and SCAN_CHUNK == RW_HEAD
    nch = nb * (RW_DIM // LANES)
    seq = pl.BlockSpec((nb, SCAN_CHUNK, RW_DIM), lambda i: (0, i, 0))
    st = pl.BlockSpec((nb, RW_HEAD, RW_DIM), lambda i: (0, 0, 0))
    return pl.pallas_call(
        functools.partial(_scan_kernel, nb=nb, zero_chunks=zero_chunks),
        grid=(tp // SCAN_CHUNK,),
        in_specs=[seq] * 6 + [st],
        out_specs=[seq, st],
        out_shape=[jax.ShapeDtypeStruct((nb, tp, RW_DIM), F32),
                   jax.ShapeDtypeStruct((nb, RW_HEAD, RW_DIM), F32)],
        scratch_shapes=[pltpu.VMEM((nch, RW_HEAD, LANES), F32),
                        pltpu.VMEM((nch, RW_HEAD, LANES), F32),
                        pltpu.VMEM((8 * nch, RW_HEAD, LANES), F32),
                        pltpu.VMEM((nch, RW_HEAD, LANES), F32),
                        pltpu.VMEM((8 * nch * RW_HEAD, 2 * LANES), BF16),
                        pltpu.VMEM((4 * nch * RW_HEAD, 2 * LANES), BF16),
                        pltpu.VMEM((4 * nch * RW_HEAD, 2 * LANES), F32)],
        compiler_params=_cparams(("arbitrary",)),
        name="rwkv_scan",
    )(r, w, k, v, a, b, s0)


def _lambda_value(lam_ref):
    lv = lam_ref[...]
    s01 = jnp.sum(lv[0:1, :] * lv[1:2, :], axis=-1, keepdims=True)
    s23 = jnp.sum(lv[2:3, :] * lv[3:4, :], axis=-1, keepdims=True)
    return jnp.exp(s01) - jnp.exp(s23) + LAM_INIT


def _alibi_slope(h):
    slope = jnp.float32(2.0 ** (-8.0 * DA_HEADS / DA_HEADS))
    for i in range(DA_HEADS - 2, -1, -1):
        slope = jnp.where(h == i, jnp.float32(2.0 ** (-8.0 * (i + 1) / DA_HEADS)), slope)
    return slope


def _sub_norm(o, g):
    return o * lax.rsqrt(jnp.mean(o * o, axis=-1, keepdims=True) + NORM_EPS) * g * (1.0 - LAM_INIT)


ATT_ONES = 16


def _attn_prompt_kernel(q_ref, k_ref, v_ref, d0_ref, lam_ref, sg_ref, o_ref,
                        qc_ref, vt_ref, bias_ref, m_ref, acc_ref, s_ref, *, blk):
    h = pl.program_id(1)
    qi = pl.program_id(2)
    slope = _alibi_slope(h)
    tp = k_ref.shape[0]
    sub = blk // LANES

    @pl.when(qi == 0)
    def _():
        for jb in range(tp // LANES):
            vt = v_ref[jb * LANES:(jb + 1) * LANES, :].T.astype(BF16)
            vt_ref[jb // sub, 0:DA_DV, (jb % sub) * LANES:(jb % sub + 1) * LANES] = vt
        vt_ref[:, DA_DV:, :] = jnp.ones((tp // blk, ATT_ONES, blk), BF16)

    lane = lax.broadcasted_iota(jnp.int32, (blk, LANES), 1)
    qb = q_ref[...].astype(BF16)
    for c in range(2):
        qc_ref[c] = jnp.where((lane // DA_DK) == c, qb, jnp.zeros_like(qb))
    bias_ref[...] = slope * d0_ref[...]
    m_ref[...] = jnp.full_like(m_ref, NEG_BIG)
    acc_ref[...] = jnp.zeros_like(acc_ref)

    def scores(j):
        kb = k_ref[pl.ds(pl.multiple_of(j * blk, blk), blk), :].astype(BF16)
        for c in range(2):
            s_ref[(j % 2) * 2 + c] = lax.dot_general(kb, qc_ref[c], (((1,), (1,)), ((), ())),
                                                     preferred_element_type=F32)

    def consume(j, masked):
        off = ((qi - j) * blk).astype(F32)
        c0 = slope * off
        if masked:
            kpos = j * blk + lax.broadcasted_iota(jnp.int32, (blk, blk), 0)
            valid = (d0_ref[...] + off >= 0.0) & (kpos >= PAD_LEAD)
        for c in range(2):
            t = s_ref[(j % 2) * 2 + c] - bias_ref[...]
            if masked:
                t = jnp.where(valid, t, NEG_BIG)
            m_old = m_ref[c]
            m_new = jnp.maximum(m_old, jnp.max(t, axis=0, keepdims=True) - c0)
            alpha = jnp.exp(m_old - m_new)
            p = jnp.exp(t - (m_new + c0)).astype(BF16)
            acc_ref[c] = alpha * acc_ref[c] + jnp.dot(vt_ref[j], p, preferred_element_type=F32)
            m_ref[c] = m_new

    def middle(j, carry):
        consume(j, False)
        scores(j + 1)
        return carry

    scores(0)
    consume(0, True)
    scores(jnp.minimum(1, tp // blk - 1))
    lax.fori_loop(1, qi, middle, 0)

    @pl.when(qi > 0)
    def _():
        consume(qi, True)

    lam = _lambda_value(lam_ref)
    a0 = acc_ref[0]
    a1 = acc_ref[1]
    ot = a0[:DA_DV] / a0[DA_DV:DA_DV + 1] - lam * (a1[:DA_DV] / a1[DA_DV:DA_DV + 1])
    o_ref[...] = _sub_norm(ot.T, sg_ref[...])


def _attn_prompt(q, k, v, da_lam, subln_g, blk):
    b, tp, _ = q.shape
    assert tp % blk == 0 and blk % LANES == 0 and PAD_LEAD <= blk
    d0t = (jnp.arange(blk, dtype=F32)[None, :] - jnp.arange(blk, dtype=F32)[:, None])
    qspec = pl.BlockSpec((None, blk, DA_DV), lambda bi, h, i: (bi, i, h))
    kspec = pl.BlockSpec((None, tp, DA_DV), lambda bi, h, i: (bi, 0, h))
    full = lambda a: pl.BlockSpec(a.shape, lambda bi, h, i: (0,) * a.ndim)
    return pl.pallas_call(
        functools.partial(_attn_prompt_kernel, blk=blk),
        grid=(b, DA_HEADS, tp // blk),
        in_specs=[qspec, kspec, kspec, full(d0t), full(da_lam), full(subln_g)],
        out_specs=qspec,
        scratch_shapes=[pltpu.VMEM((2, blk, DA_DV), BF16),
                        pltpu.VMEM((tp // blk, DA_DV + ATT_ONES, blk), BF16),
                        pltpu.VMEM((blk, blk), F32), pltpu.VMEM((2, 1, blk), F32),
                        pltpu.VMEM((2, DA_DV + ATT_ONES, blk), F32),
                        pltpu.VMEM((4, blk, blk), F32)],
        out_shape=jax.ShapeDtypeStruct((b, tp, DA_V), F32),
        compiler_params=_cparams(("parallel", "parallel", "arbitrary")),
        name="attn_prompt",
    )(q, k, v, d0t, da_lam, subln_g)


def _attn_sample_kernel(pt_ref, q_ref, kn_ref, vn_ref, lam_ref, sg_ref, *rest, ppg, page, past):
    k_refs = rest[:ppg]
    v_refs = rest[ppg:2 * ppg]
    o_ref, m_ref, l_ref, acc_ref = rest[2 * ppg:]
    g = pl.program_id(1)
    nrow = 2 * DA_HEADS
    rowi = lax.broadcasted_iota(jnp.int32, (nrow, DA_QK), 0)
    lanei = lax.broadcasted_iota(jnp.int32, (nrow, DA_QK), 1)
    qsel = (lanei // DA_DK) == rowi
    qmat = jnp.where(qsel, q_ref[...], 0.0)
    rh = lax.broadcasted_iota(jnp.int32, (nrow, 1), 0) // 2
    slope = jnp.full((nrow, 1), 2.0 ** (-8.0), F32)
    for i in range(DA_HEADS - 2, -1, -1):
        slope = jnp.where(rh == i, jnp.float32(2.0 ** (-8.0 * (i + 1) / DA_HEADS)), slope)

    @pl.when(g == 0)
    def _():
        m_ref[...] = jnp.full_like(m_ref, NEG_BIG)
        l_ref[...] = jnp.zeros_like(l_ref)
        acc_ref[...] = jnp.zeros_like(acc_ref)

    qb = qmat.astype(BF16)
    kpos = g * (ppg * page) + lax.broadcasted_iota(jnp.int32, (nrow, ppg * page), 1)
    s = jnp.concatenate([jnp.dot(qb, k_refs[i][...].astype(BF16), preferred_element_type=F32)
                         for i in range(ppg)], axis=1)
    s = s - slope * (past - kpos).astype(F32)
    m_old = m_ref[...]
    m_new = jnp.maximum(m_old, jnp.max(s, axis=-1, keepdims=True))
    alpha = jnp.exp(m_old - m_new)
    p = jnp.exp(s - m_new)
    l_ref[...] = alpha * l_ref[...] + jnp.sum(p, axis=-1, keepdims=True)
    pb = p.astype(BF16)
    pv = [jnp.zeros((nrow, DA_DV), F32)] * DA_HEADS
    for i in range(ppg):
        for hh in range(DA_HEADS):
            vh = v_refs[i][pl.ds(hh, page, stride=DA_HEADS), :].astype(BF16)
            pv[hh] = pv[hh] + jnp.dot(pb[:, i * page:(i + 1) * page], vh, preferred_element_type=F32)
    tot = pv[DA_HEADS - 1]
    for hh in range(DA_HEADS - 1):
        tot = jnp.where(rh == hh, pv[hh], tot)
    acc_ref[...] = alpha * acc_ref[...] + tot
    m_ref[...] = m_new

    @pl.when(g == pl.num_programs(1) - 1)
    def _():
        kn = kn_ref[...].astype(BF16).astype(F32)
        s_self = jnp.sum(qb.astype(F32) * kn, axis=-1, keepdims=True)
        m_old = m_ref[...]
        m_new = jnp.maximum(m_old, s_self)
        alpha = jnp.exp(m_old - m_new)
        p_self = jnp.exp(s_self - m_new)
        l_fin = alpha * l_ref[...] + p_self
        vn = vn_ref[...].astype(BF16).astype(F32)
        vrow = jnp.zeros((nrow, DA_DV), F32)
        for hh in range(DA_HEADS):
            vrow = jnp.where(rh == hh, vn[:, hh * DA_DV:(hh + 1) * DA_DV], vrow)
        acc = alpha * acc_ref[...] + p_self.astype(BF16).astype(F32) * vrow
        lam = _lambda_value(lam_ref)
        ri = lax.broadcasted_iota(jnp.int32, (nrow, 1), 0)
        t = acc / l_fin * jnp.where(ri % 2 == 0, 1.0, -lam)
        outs = []
        for hh in range(DA_HEADS):
            o = t[2 * hh:2 * hh + 1, :] + t[2 * hh + 1:2 * hh + 2, :]
            outs.append(_sub_norm(o, sg_ref[...]))
        o_ref[...] = jnp.concatenate(outs, axis=1)


def _attn_sample(q, kn, vn, cache_kt, cache_v, page_table, da_lam, subln_g, ppg):
    bd = q.shape[0]
    n_pages = page_table.shape[1]
    page = cache_kt.shape[2]
    assert n_pages % ppg == 0 and cache_v.shape[1:] == (page * DA_HEADS, DA_DV)
    pt = page_table.reshape(-1)
    one = pl.BlockSpec((None, 1, DA_QK), lambda b, g, pt: (b, 0, 0))
    full = lambda a: pl.BlockSpec(a.shape, lambda b, g, pt: (0,) * a.ndim)

    def pspec(i):
        return pl.BlockSpec((None, DA_QK, page),
                            lambda b, g, pt, i=i: (pt[b * n_pages + g * ppg + i], 0, 0))

    out = pl.pallas_call(
        functools.partial(_attn_sample_kernel, ppg=ppg, page=page, past=n_pages * page),
        grid_spec=pltpu.PrefetchScalarGridSpec(
            num_scalar_prefetch=1,
            grid=(bd, n_pages // ppg),
            in_specs=[one, one, one, full(da_lam), full(subln_g)]
                     + [pspec(i) for i in range(ppg)] * 2,
            out_specs=one,
            scratch_shapes=[pltpu.VMEM((2 * DA_HEADS, 1), F32), pltpu.VMEM((2 * DA_HEADS, 1), F32),
                            pltpu.VMEM((2 * DA_HEADS, DA_DV), F32)],
        ),
        out_shape=jax.ShapeDtypeStruct((bd, 1, DA_V), F32),
        compiler_params=_cparams(("parallel", "arbitrary")),
        name="attn_sample",
    )(pt, q.reshape(bd, 1, DA_QK), kn.reshape(bd, 1, DA_QK), vn.reshape(bd, 1, DA_V), da_lam, subln_g,
      *([cache_kt] * ppg), *([cache_v] * ppg))
    return out.reshape(bd, DA_V)


def _merge_kernel(*refs, nsub):
    rows = [refs[k * nsub:(k + 1) * nsub] for k in range(6)]
    (lng_ref, lnb_ref, wrw_ref, wda_ref, wout_ref, n2_ref, wq_ref, sk_ref,
     h2_ref, xn_ref, st_ref) = refs[6 * nsub:]
    stack = lambda rs: rs[0][...] if nsub == 1 else jnp.concatenate([r[...] for r in rs], axis=0)
    y, bonus, g, o, gt, h = [stack(rs) for rs in rows]
    j = _seg_ones()
    mu = _seg64(y, j) * (1.0 / RW_HEAD)
    d = y - mu
    var = _seg64(d * d, j) * (1.0 / RW_HEAD)
    yln = d * lax.rsqrt(var + RW_LN_EPS) * lng_ref[...] + lnb_ref[...]
    yr = ((yln + bonus) * g).astype(BF16)
    y_rw = jnp.dot(yr, wrw_ref[...], preferred_element_type=F32)
    y_da = jnp.dot(o.astype(BF16), wda_ref[...], preferred_element_type=F32)
    mix = gt[:, :D_MODEL] * y_rw + gt[:, D_MODEL:] * y_da
    h2 = h + jnp.dot(mix.astype(BF16), wout_ref[...], preferred_element_type=F32)
    h2_ref[...] = h2
    ms = jnp.mean(h2 * h2, axis=-1, keepdims=True)
    xn = (h2 * lax.rsqrt(ms + NORM_EPS) * n2_ref[...]).astype(BF16)
    xn_ref[...] = xn
    q = jnp.dot(xn, wq_ref[...], preferred_element_type=F32)
    for hc in range(2 * PEER_HEADS):
        qc = q[:, hc * PEER_DHALF:(hc + 1) * PEER_DHALF].astype(BF16)
        st_ref[hc * N_KEYS:(hc + 1) * N_KEYS, :] = lax.dot_general(
            sk_ref[hc % 2], qc, (((1,), (1,)), ((), ())), preferred_element_type=F32)


def _merge(y, bonus, g, o, gates, h, consts, tm, tile_off, nsub):
    b, rows, _ = y.shape
    assert rows % tm == 0 and (rows // tm - tile_off) % nsub == 0
    nt = (rows // tm - tile_off) // nsub
    full = lambda a: pl.BlockSpec(a.shape, lambda i, t: (0,) * a.ndim)

    def rin(c):
        return [pl.BlockSpec((None, tm, c), lambda i, t, k=k: (i, t * nsub + k + tile_off, 0))
                for k in range(nsub)]

    row_arrays = (y, bonus, g, o, gates, h)
    tmo = tm * nsub
    ntok = b * nt * tmo
    return pl.pallas_call(
        functools.partial(_merge_kernel, nsub=nsub),
        grid=(b, nt),
        in_specs=[s for a in row_arrays for s in rin(a.shape[-1])] + [full(a) for a in consts],
        out_specs=[pl.BlockSpec((tmo, D_MODEL), lambda i, t: (i * nt + t, 0)),
                   pl.BlockSpec((tmo, D_MODEL), lambda i, t: (i * nt + t, 0)),
                   pl.BlockSpec((2 * PEER_HEADS * N_KEYS, tmo), lambda i, t: (0, i * nt + t))],
        out_shape=[jax.ShapeDtypeStruct((ntok, D_MODEL), F32),
                   jax.ShapeDtypeStruct((ntok, D_MODEL), BF16),
                   jax.ShapeDtypeStruct((2 * PEER_HEADS * N_KEYS, ntok), F32)],
        compiler_params=_cparams(("parallel", "parallel")),
        name="merge",
    )(*[a for a in row_arrays for _ in range(nsub)], *consts)


def _count(mask):
    return jnp.sum(jnp.where(mask, 1.0, 0.0), axis=0, keepdims=True)


def _top16(x, exact):
    nk = x.shape[0]
    iota = lax.broadcasted_iota(jnp.int32, x.shape, 0).astype(F32)
    rank = jnp.full(x.shape, float(nk), F32)
    vals = []
    for r in range(PEER_TOPK):
        m = jnp.max(x, axis=0, keepdims=True)
        sel = x == m
        if exact:
            cand = jnp.where(sel, iota, float(nk))
            sel = cand == jnp.min(cand, axis=0, keepdims=True)
        x = jnp.where(sel, -jnp.inf, x)
        rank = jnp.where(sel, float(r), rank)
        vals.append(m)
    return vals, rank


def _peer_topk_kernel(st_ref, rank2_ref, c2_ref, cnt1_ref, c1_ref):
    n = st_ref.shape[1]
    rid = lax.broadcasted_iota(jnp.int32, (_CAND_ROWS, n), 0)
    flat = jnp.zeros((_CAND_ROWS, n), jnp.int32) + PEER_TOPK * PEER_TOPK
    irow = jnp.zeros((_CAND_ROWS, n), jnp.int32) + PEER_TOPK
    for ci, (i, j) in enumerate(_CAND):
        flat = jnp.where(rid == ci, i * PEER_TOPK + j, flat)
        irow = jnp.where(rid == ci, i, irow)
    outs = (rank2_ref, c2_ref, cnt1_ref, c1_ref)
    ties = _peer_topk_body(st_ref, outs, flat, irow, exact=False)

    @pl.when(jnp.max(ties) > 0.0)
    def _():
        _peer_topk_body(st_ref, outs, flat, irow, exact=True)


def _peer_topk_body(st_ref, outs, flat, irow, exact):
    rank2_ref, c2_ref, cnt1_ref, c1_ref = outs
    n = st_ref.shape[1]
    ties = jnp.zeros((1, n), F32)
    for h in range(PEER_HEADS):
        s1 = st_ref[(2 * h) * N_KEYS:(2 * h + 1) * N_KEYS, :]
        s2 = st_ref[(2 * h + 1) * N_KEYS:(2 * h + 2) * N_KEYS, :]
        v1, rank1 = _top16(s1, exact)
        v2, rank2 = _top16(s2, exact)
        rows = [v1[i] + v2[j] for (i, j) in _CAND]
        rows += [jnp.full((1, n), -jnp.inf, F32)] * (_CAND_ROWS - len(_CAND))
        cand = jnp.concatenate(rows, axis=0)
        top = v1[0] + v2[0]
        chosen = jnp.zeros(cand.shape, jnp.bool_)
        work = cand
        for _ in range(PEER_TOPK):
            m = jnp.max(work, axis=0, keepdims=True)
            sel = work == m
            if exact:
                fl = jnp.where(sel, flat, PEER_TOPK * PEER_TOPK + 1)
                sel = fl == jnp.min(fl, axis=0, keepdims=True)
            work = jnp.where(sel, -jnp.inf, work)
            chosen = chosen | sel
        if not exact:
            k = float(PEER_TOPK)
            bad = (_count(rank1 < k) != k) | (_count(rank2 < k) != k) | (_count(chosen) != k)
            ties = jnp.maximum(ties, jnp.where(bad, 1.0, 0.0))
        z = jnp.sum(jnp.where(chosen, jnp.exp(cand - top), 0.0), axis=0, keepdims=True)
        rank1_b = rank1.astype(BF16)
        cnt1 = jnp.zeros(s1.shape, BF16)
        for i in range(PEER_TOPK):
            m_i = jnp.sum(jnp.where(chosen & (irow == i), 1.0, 0.0), axis=0, keepdims=True)
            cnt1 = jnp.where(rank1_b == float(i), m_i.astype(BF16), cnt1)
        cnt1_ref[h] = cnt1.astype(F32)
        c1_ref[h] = jnp.where(rank1 < float(PEER_TOPK), jnp.exp(s1 - v1[0]) / z * 0.5, 0.0)
        c2_ref[h] = jnp.where(rank2 < float(PEER_TOPK), jnp.exp(s2 - v2[0]), 0.0).astype(BF16)
        rank2_ref[h] = rank2.astype(BF16)
    return ties


def _peer_topk(st, tn):
    ntok = st.shape[1]
    assert ntok % tn == 0
    spec = pl.BlockSpec((PEER_HEADS, N_KEYS, tn), lambda i: (0, 0, i))
    return pl.pallas_call(
        _peer_topk_kernel,
        grid=(ntok // tn,),
        in_specs=[pl.BlockSpec((st.shape[0], tn), lambda i: (0, i))],
        out_specs=[spec] * 4,
        out_shape=[jax.ShapeDtypeStruct((PEER_HEADS, N_KEYS, ntok), dt) for dt in (BF16, BF16, F32, F32)],
        compiler_params=_cparams(("parallel",)),
        name="peer_topk",
    )(st)


def _peer_dense_kernel(x_ref, u_ref, vt_ref, rank2_ref, c2_ref, cnt1_ref, c1_ref, h2_ref, y_ref,
                       acc_ref, hid_ref, p_ref, *, e1b):
    j = pl.program_id(1)
    tn = x_ref.shape[0]
    ngrp = N_KEYS // BF16_ROWS

    @pl.when(j == 0)
    def _():
        acc_ref[...] = jnp.zeros_like(acc_ref)

    hid_ref[...] = lax.dot_general(u_ref[...], x_ref[...], (((1,), (1,)), ((), ())),
                                   preferred_element_type=F32)

    for e in range(e1b):
        e1 = j * e1b + e
        gate = [None] * ngrp
        for h in range(PEER_HEADS):
            cnt = jnp.broadcast_to(cnt1_ref[h, pl.ds(e1, 1), :], (BF16_ROWS, tn)).astype(BF16)
            c1 = jnp.broadcast_to(c1_ref[h, pl.ds(e1, 1), :], (BF16_ROWS, tn)).astype(BF16)
            for g in range(ngrp):
                rows = slice(g * BF16_ROWS, (g + 1) * BF16_ROWS)
                c2 = c2_ref[h, rows, :]
                term = jnp.where(rank2_ref[h, rows, :] < cnt, c2, jnp.zeros_like(c2)) * c1
                gate[g] = term if gate[g] is None else gate[g] + term
        hh = hid_ref[e * N_KEYS:(e + 1) * N_KEYS, :]
        gelu = (hh * (1.0 + lax.erf(hh * (2.0 ** -0.5)))).astype(BF16)
        for g in range(ngrp):
            r0 = e * N_KEYS + g * BF16_ROWS
            p_ref[r0:r0 + BF16_ROWS, :] = gelu[g * BF16_ROWS:(g + 1) * BF16_ROWS] * gate[g]
    acc_ref[...] += jnp.dot(vt_ref[...], p_ref[...], preferred_element_type=F32)

    @pl.when(j == pl.num_programs(1) - 1)
    def _():
        y_ref[...] = h2_ref[...] + acc_ref[...].T


def _peer_dense(xn, u_bf, vt_bf, rank2, c2, cnt1, c1, h2, tn, e1b):
    ntok = xn.shape[0]
    assert ntok % tn == 0 and N_KEYS % e1b == 0
    tok = pl.BlockSpec((PEER_HEADS, N_KEYS, tn), lambda i, j: (0, 0, i))
    row = pl.BlockSpec((tn, D_MODEL), lambda i, j: (i, 0))
    return pl.pallas_call(
        functools.partial(_peer_dense_kernel, e1b=e1b),
        grid=(ntok // tn, N_KEYS // e1b),
        in_specs=[row,
                  pl.BlockSpec((e1b * N_KEYS, D_MODEL), lambda i, j: (j, 0)),
                  pl.BlockSpec((D_MODEL, e1b * N_KEYS), lambda i, j: (0, j)),
                  tok, tok, tok, tok, row],
        out_specs=row,
        out_shape=jax.ShapeDtypeStruct((ntok, D_MODEL), F32),
        scratch_shapes=[pltpu.VMEM((D_MODEL, tn), F32), pltpu.VMEM((e1b * N_KEYS, tn), F32),
                        pltpu.VMEM((e1b * N_KEYS, tn), BF16)],
        compiler_params=_cparams(("parallel", "arbitrary")),
        name="peer_dense",
    )(xn, u_bf, vt_bf, rank2, c2, cnt1, c1, h2)


def _state_in(s):
    b = s.shape[0]
    return jnp.transpose(s, (0, 2, 1, 3)).reshape(b, RW_HEAD, RW_DIM)


def _state_out(s):
    b = s.shape[0]
    return jnp.transpose(s.reshape(b, RW_HEAD, RW_HEADS, RW_HEAD), (0, 2, 1, 3))


def _pad_tokens(a, n, axis):
    pad = [(0, 0)] * a.ndim
    pad[axis] = (0, n - a.shape[axis])
    return jnp.pad(a, pad)


def kernel(x_prompt, x_sample, cache_k, cache_v, state_wkv, state_shift, page_table, meta, norm1_g, w_in,
           rw_mu, rw_w0, rw_w2, rw_a0, rw_a2, rw_g2, rw_kk, rw_ka, rw_rk, rw_ln_g, rw_ln_b, w_rw_br,
           da_qn_g, da_kn_g, da_lam, da_subln_g, w_da_br, w_out, norm2_g, peer_wq, peer_subkeys,
           peer_u, peer_v):
    b, seq, _ = x_prompt.shape
    bd = x_sample.shape[0]
    t_real = N_META + seq
    tp = PAD_LEAD + t_real
    assert tp % ROW_TILE == 0 and tp % SCAN_CHUNK == 0 and x_sample.shape[1] == 1
    row2 = lambda a: a.reshape(1, -1)

    w_in_bf = w_in[0].astype(BF16)
    g1 = row2(norm1_g[0])
    qg = row2(jnp.tile(da_qn_g[0], DA_QK // DA_DK))
    kg = row2(jnp.tile(da_kn_g[0], DA_QK // DA_DK))
    zlo = jnp.zeros((W_LORA, RW_DIM), F32)
    prep_params = (row2(rw_mu[0]), row2(rw_w0[0]),
                   jnp.concatenate([rw_w2[0], zlo], axis=0).astype(BF16), row2(rw_a0[0]),
                   jnp.concatenate([zlo, rw_a2[0]], axis=0).astype(BF16), rw_g2[0].astype(BF16),
                   row2(rw_kk[0]), row2(rw_ka[0]), row2(rw_rk[0]))
    merge_consts = (row2(rw_ln_g[0]), row2(rw_ln_b[0]), w_rw_br[0].astype(BF16), w_da_br[0].astype(BF16),
                    w_out[0].astype(BF16), row2(norm2_g[0]), peer_wq[0].astype(BF16),
                    peer_subkeys[0].astype(BF16))
    subln = row2(da_subln_g[0])
    u_bf = peer_u[0].astype(BF16)
    vt_bf = peer_v[0].astype(BF16).T

    hp = jnp.concatenate([jnp.zeros((b, PAD_LEAD, D_MODEL), F32),
                          jnp.broadcast_to(meta[None], (b, N_META, D_MODEL)), x_prompt], axis=1)
    prw, q, k, v, gates = _in_proj(hp.reshape(b * tp, D_MODEL), g1, w_in_bf, qg, kg, IN_PROJ_ROWS)
    r3 = lambda a: a.reshape(b, tp, -1)
    pr, pw, pk, pv, pa, pb, pg, pbonus = _rwkv_prep_prompt(r3(prw), prep_params, PREP_ROWS)
    y_scan, s_fin = _rwkv_scan(pr, pw, pk, pv, pa, pb, jnp.zeros((b, RW_HEAD, RW_DIM), F32),
                               PAD_LEAD // SCAN_CHUNK)
    o_attn = _attn_prompt(r3(q), r3(k), r3(v), da_lam[0], subln, ATTN_BLOCK)
    h2, xn2, st = _merge(y_scan, pbonus, pg, o_attn, r3(gates), hp, merge_consts, ROW_TILE, 1, MERGE_STACK)
    y_prompt = _peer_dense(xn2, u_bf, vt_bf, *_peer_topk(st, TOPK_TOKENS), h2, DENSE_TOKENS, DENSE_E1)

    xs = x_sample.reshape(bd, D_MODEL)
    prw_s, q_s, k_s, v_s, gates_s = _in_proj(xs, g1, w_in_bf, qg, kg, bd)
    sr, sw, sk, sv, sa, sb, sg, sbonus = _rwkv_prep_sample(prw_s, state_shift[0], prep_params)
    ys_scan, ss_fin = _rwkv_step(sr, sw, sk, sv, sa, sb, _state_in(state_wkv[0]))
    n_phys = cache_k.shape[1]
    ckt = jnp.transpose(cache_k[0].reshape(n_phys, -1, DA_QK), (0, 2, 1))
    cv = cache_v[0].reshape(n_phys, -1, DA_DV)
    o_s = _attn_sample(q_s, k_s, v_s, ckt, cv, page_table, da_lam[0], subln, PAGES_PER_STEP)
    f3 = lambda a: a.reshape(1, bd, -1)
    h2s, xn2s, sts = _merge(ys_scan.reshape(1, bd, RW_DIM), f3(sbonus), f3(sg), f3(o_s), f3(gates_s),
                            f3(xs), merge_consts, bd, 0, 1)
    npad = LANES
    tk_s = _peer_topk(_pad_tokens(sts, npad, 1), npad)
    y_s = _peer_dense(_pad_tokens(xn2s, npad, 0), u_bf, vt_bf, *tk_s, _pad_tokens(h2s, npad, 0), npad, DENSE_E1)

    y_prompt = y_prompt.reshape(b, seq, D_MODEL)
    y_sample = y_s[:bd].reshape(bd, 1, D_MODEL)
    k_p = r3(k)[:, PAD_LEAD:].reshape(1, b, t_real, DA_HEADS, 2, DA_DK)
    v_p = r3(v)[:, PAD_LEAD:].reshape(1, b, t_real, DA_HEADS, DA_DV)
    return (y_prompt, y_sample, k_p, v_p,
            k_s.reshape(1, bd, 1, DA_HEADS, 2, DA_DK), v_s.reshape(1, bd, 1, DA_HEADS, DA_DV),
            _state_out(s_fin)[None], _state_out(ss_fin)[None],
            r3(prw)[:, -1][None], prw_s[None])
```

```python
import functools
import math

import jax
import jax.numpy as jnp
from jax import lax
from jax.experimental import pallas as pl
from jax.experimental.pallas import tpu as pltpu

F32 = jnp.float32
BF16 = jnp.bfloat16

D_MODEL = 1024
N_META = 16
NORM_EPS = 1e-6
RW_HEAD = 64
RW_DIM = D_MODEL // 2
RW_HEADS = RW_DIM // RW_HEAD
W_LORA = 64
A_LORA = 64
G_LORA = 128
RW_PROJ = 3 * RW_DIM + W_LORA + A_LORA + G_LORA
RW_LN_EPS = 64e-5
DA_DK = 64
DA_DV = 2 * DA_DK
DA_DIM = D_MODEL // 2
DA_HEADS = DA_DIM // DA_DV
DA_QK = DA_HEADS * 2 * DA_DK
DA_V = DA_HEADS * DA_DV
N_KEYS = 128
PEER_HEADS = 8
PEER_TOPK = 16
PEER_DQ = 256
PEER_DHALF = PEER_DQ // 2
LAM_INIT = 0.8 - 0.6 * math.exp(-0.3 * 0)

LANES = 128
BF16_ROWS = 16
ROW_TILE = 128
PAD_LEAD = ROW_TILE - N_META
SCAN_CHUNK = 64
NEG_BIG = -1e30
VMEM_LIMIT = 56 * 1024 * 1024

IN_PROJ_ROWS = 512
PREP_ROWS = 384
ATTN_BLOCK = 384
MERGE_STACK = 4
TOPK_TOKENS = 128
DENSE_TOKENS = 512
DENSE_E1 = 16
PAGES_PER_STEP = 32

O1 = RW_PROJ
O2 = O1 + DA_QK
O3 = O2 + DA_QK
O4 = O3 + DA_V
O5 = O4 + D_MODEL
O6 = O5 + D_MODEL

_CAND = [(i, j) for i in range(PEER_TOPK) for j in range(PEER_TOPK) if (i + 1) * (j + 1) <= PEER_TOPK]
_CAND_ROWS = -(-len(_CAND) // 8) * 8


def _cparams(sem):
    return pltpu.CompilerParams(dimension_semantics=sem, vmem_limit_bytes=VMEM_LIMIT)


def _split_dot(x, j):
    hi = x.astype(BF16)
    lo = (x - hi.astype(F32)).astype(BF16)
    return (jnp.dot(hi, j, preferred_element_type=F32)
            + jnp.dot(lo, j, preferred_element_type=F32))


def _seg64(x, j):
    outs = [_split_dot(x[:, c * LANES:(c + 1) * LANES], j) for c in range(x.shape[1] // LANES)]
    return outs[0] if len(outs) == 1 else jnp.concatenate(outs, axis=1)


def _seg_ones():
    r = lax.broadcasted_iota(jnp.int32, (LANES, LANES), 0) // 64
    c = lax.broadcasted_iota(jnp.int32, (LANES, LANES), 1) // 64
    return (r == c).astype(BF16)


def _in_proj_kernel(x_ref, g1_ref, w_ref, qg_ref, kg_ref, prw_ref, q_ref, k_ref, v_ref, gt_ref):
    x = x_ref[...]
    ms = jnp.mean(x * x, axis=-1, keepdims=True)
    xn = (x * lax.rsqrt(ms + NORM_EPS) * g1_ref[...]).astype(BF16)

    def mm(a, b):
        return jnp.dot(xn, w_ref[:, a:b], preferred_element_type=F32)

    j = _seg_ones()

    def head_norm(t, g):
        msq = _seg64(t * t, j) * (1.0 / DA_DK)
        return t * lax.rsqrt(msq + NORM_EPS) * g

    prw_ref[...] = mm(0, O1)
    q_ref[...] = head_norm(mm(O1, O2), qg_ref[...]) * (DA_DK ** -0.5)
    k_ref[...] = head_norm(mm(O2, O3), kg_ref[...])
    v_ref[...] = mm(O3, O4)
    gt_ref[...] = jax.nn.sigmoid(mm(O4, O6))


def _in_proj(x, g1, w_bf, qg, kg, tm):
    n = x.shape[0]
    assert n % tm == 0
    row = lambda c: pl.BlockSpec((tm, c), lambda i: (i, 0))
    full = lambda a: pl.BlockSpec(a.shape, lambda i: (0,) * a.ndim)
    widths = (RW_PROJ, DA_QK, DA_QK, DA_V, 2 * D_MODEL)
    return pl.pallas_call(
        _in_proj_kernel,
        grid=(n // tm,),
        in_specs=[row(D_MODEL), full(g1), full(w_bf), full(qg), full(kg)],
        out_specs=[row(c) for c in widths],
        out_shape=[jax.ShapeDtypeStruct((n, c), F32) for c in widths],
        compiler_params=_cparams(("parallel",)),
        name="in_proj",
    )(x, g1, w_bf, qg, kg)


def _rwkv_prep_math(p, p_prev, mu, w0, w2p, a0, a2p, g2, kkp, kap, rk):
    ps = p + (p_prev - p) * mu
    r = ps[:, 0:RW_DIM]
    k = ps[:, RW_DIM:2 * RW_DIM]
    v = ps[:, 2 * RW_DIM:3 * RW_DIM]
    wa = ps[:, 3 * RW_DIM:3 * RW_DIM + W_LORA + A_LORA]
    gl = ps[:, 3 * RW_DIM + W_LORA + A_LORA:RW_PROJ]
    w = -jax.nn.softplus(-(w0 + jnp.dot(jnp.tanh(wa).astype(BF16), w2p, preferred_element_type=F32))) - 0.5
    decay = jnp.exp(-jnp.exp(w))
    a = jax.nn.sigmoid(a0 + jnp.dot(wa.astype(BF16), a2p, preferred_element_type=F32))
    g = jnp.dot(jax.nn.sigmoid(gl).astype(BF16), g2, preferred_element_type=F32)
    j = _seg_ones()
    kk = k * kkp
    kk = kk / jnp.maximum(jnp.sqrt(_seg64(kk * kk, j)), 1e-12)
    k2 = k * (1.0 + (a - 1.0) * kap)
    bonus = _seg64(r * k2 * rk, j) * v
    return r, decay, k2, v, -kk, kk * a, g, bonus


def _rwkv_prep_carry_kernel(p_ref, mu, w0, w2p, a0, a2p, g2, kkp, kap, rk, *rest):
    outs, carry = rest[:-1], rest[-1]

    @pl.when(pl.program_id(1) == 0)
    def _():
        carry[...] = jnp.zeros_like(carry)

    p = p_ref[...]
    rows = lax.broadcasted_iota(jnp.int32, p.shape, 0)
    p_prev = jnp.where(rows == 0, carry[...], pltpu.roll(p, 1, 0))
    carry[...] = p[p.shape[0] - 1:, :]
    res = _rwkv_prep_math(p, p_prev, mu[...], w0[...], w2p[...], a0[...], a2p[...], g2[...],
                          kkp[...], kap[...], rk[...])
    for o, val in zip(outs, res):
        o[...] = val


def _rwkv_prep_given_kernel(p_ref, pp_ref, mu, w0, w2p, a0, a2p, g2, kkp, kap, rk, *outs):
    res = _rwkv_prep_math(p_ref[...], pp_ref[...], mu[...], w0[...], w2p[...], a0[...], a2p[...],
                          g2[...], kkp[...], kap[...], rk[...])
    for o, val in zip(outs, res):
        o[...] = val


def _rwkv_prep_prompt(p3, params, tm):
    b, tp, _ = p3.shape
    assert tp % tm == 0
    full = lambda a: pl.BlockSpec(a.shape, lambda i, j: (0,) * a.ndim)
    return pl.pallas_call(
        _rwkv_prep_carry_kernel,
        grid=(b, tp // tm),
        in_specs=[pl.BlockSpec((None, tm, RW_PROJ), lambda i, j: (i, j, 0))] + [full(a) for a in params],
        out_specs=[pl.BlockSpec((None, tm, RW_DIM), lambda i, j: (i, j, 0))] * 8,
        out_shape=[jax.ShapeDtypeStruct((b, tp, RW_DIM), F32)] * 8,
        scratch_shapes=[pltpu.VMEM((1, RW_PROJ), F32)],
        compiler_params=_cparams(("arbitrary", "arbitrary")),
        name="rwkv_prep_prompt",
    )(p3, *params)


def _rwkv_prep_sample(p, p_prev, params):
    n = p.shape[0]
    full = lambda a: pl.BlockSpec(a.shape, lambda i: (0,) * a.ndim)
    return pl.pallas_call(
        _rwkv_prep_given_kernel,
        grid=(1,),
        in_specs=[full(p), full(p_prev)] + [full(a) for a in params],
        out_specs=[pl.BlockSpec((n, RW_DIM), lambda i: (0, 0))] * 8,
        out_shape=[jax.ShapeDtypeStruct((n, RW_DIM), F32)] * 8,
        compiler_params=_cparams(("arbitrary",)),
        name="rwkv_prep_sample",
    )(p, p_prev, *params)


def _scan_consts():
    lane = lax.broadcasted_iota(jnp.int32, (RW_HEAD, LANES), 1)
    sub = lax.broadcasted_iota(jnp.int32, (RW_HEAD, LANES), 0)
    first = lane < RW_HEAD
    eye_lo = (lane == sub).astype(F32)
    eye_hi = (lane - RW_HEAD == sub).astype(F32)
    return lane, first, eye_lo, eye_hi


def _seg(first, x_lo, x_hi):
    lo = jnp.sum(x_lo, axis=-1, keepdims=True)
    hi = jnp.sum(x_hi, axis=-1, keepdims=True)
    return jnp.where(first, lo, hi)


def _split_row(first, row):
    lo = jnp.where(first[0:1, :], row, 0.0)
    return lo, row - lo


def _scan_step_kernel(r_ref, w_ref, k_ref, v_ref, a_ref, b_ref, s0_ref, y_ref, s_ref, *, nb):
    _, first, eye_lo, eye_hi = _scan_consts()
    for b in range(nb):
        for p in range(RW_DIM // LANES):
            cols = pl.ds(p * LANES, LANES)
            r_t, w_t, k_t, v_t, a_t, b_t = [ref[pl.ds(b, 1), cols]
                                            for ref in (r_ref, w_ref, k_ref, v_ref, a_ref, b_ref)]
            s = s0_ref[b, :, cols]
            a_lo, a_hi = _split_row(first, a_t)
            r_lo, r_hi = _split_row(first, r_t)
            sa = _seg(first, s * a_lo, s * a_hi)
            vcol = _seg(first, eye_lo * v_t, eye_hi * v_t)
            s = s * w_t + sa * b_t + vcol * k_t
            s_ref[b, :, cols] = s
            ycol = _seg(first, s * r_lo, s * r_hi)
            y_ref[pl.ds(b, 1), cols] = jnp.sum((eye_lo + eye_hi) * ycol, axis=0, keepdims=True)


def _rwkv_step(r, w, k, v, a, b, s0):
    nb = r.shape[0]
    full = lambda x: pl.BlockSpec(x.shape, lambda i: (0,) * x.ndim)
    return pl.pallas_call(
        functools.partial(_scan_step_kernel, nb=nb),
        grid=(1,),
        in_specs=[full(r)] * 6 + [full(s0)],
        out_specs=[full(r), full(s0)],
        out_shape=[jax.ShapeDtypeStruct(r.shape, F32), jax.ShapeDtypeStruct(s0.shape, F32)],
        compiler_params=_cparams(("arbitrary",)),
        name="rwkv_step",
    )(r, w, k, v, a, b, s0)


def _scan_kernel(r_ref, w_ref, k_ref, v_ref, a_ref, b_ref, s0_ref, y_ref, sfin_ref,
                 s_scr, sa_scr, vc_scr, yacc_scr, vstage, zstage, yres, *, nb, zero_chunks):
    npair = RW_DIM // LANES
    chains = [(b, p) for b in range(nb) for p in range(npair)]

    @pl.when(pl.program_id(0) == 0)
    def _():
        for ci, (b, p) in enumerate(chains):
            s_scr[ci] = s0_ref[b, :, p * LANES:(p + 1) * LANES]

    @pl.when(pl.program_id(0) < zero_chunks)
    def _():
        y_ref[...] = jnp.zeros_like(y_ref)
        sfin_ref[...] = s0_ref[...]

    @pl.when(pl.program_id(0) >= zero_chunks)
    def _():
        _scan_chunk(r_ref, w_ref, k_ref, v_ref, a_ref, b_ref, y_ref, sfin_ref,
                    s_scr, sa_scr, vc_scr, yacc_scr, vstage, zstage, yres, chains)


def _scan_chunk(r_ref, w_ref, k_ref, v_ref, a_ref, b_ref, y_ref, sfin_ref,
                s_scr, sa_scr, vc_scr, yacc_scr, vstage, zstage, yres, chains):
    lane, first, eye_lo, eye_hi = _scan_consts()
    sub_t = 8
    yacc_scr[...] = jnp.zeros_like(yacc_scr)
    nch = len(chains)
    eye2 = eye_lo + eye_hi
    seg_ones = _seg_ones()
    zero_b = jnp.zeros((LANES, LANES), BF16)
    j_stack = jnp.concatenate([seg_ones, seg_ones], axis=0)
    j_pair = jnp.concatenate([jnp.concatenate([seg_ones, zero_b], axis=1),
                              jnp.concatenate([zero_b, seg_ones], axis=1)], axis=0)

    def prepare(ci, s, a_t):
        a_lo, a_hi = _split_row(first, a_t)
        sa_scr[ci] = _seg(first, s * a_lo, s * a_hi)

    def group(tg, carry):
        rows = pl.ds(pl.multiple_of(tg * sub_t, sub_t), sub_t)

        def row(ref, ci, i):
            b, p = chains[ci]
            return ref[b, rows, pl.ds(p * LANES, LANES)][i:i + 1, :]

        for i in range(sub_t):
            for ci in range(nch):
                v_t = row(v_ref, ci, i)
                v_hi = v_t.astype(BF16).astype(F32)
                lhs = jnp.concatenate([eye2 * v_hi, eye2 * (v_t - v_hi)], axis=1)
                vstage[pl.ds((i * nch + ci) * RW_HEAD, RW_HEAD), :] = lhs.astype(BF16)
        vc_scr[...] = jnp.dot(vstage[...], j_stack, preferred_element_type=F32).reshape(vc_scr.shape)

        for ci in range(nch):
            prepare(ci, s_scr[ci], row(a_ref, ci, 0))
        for i in range(sub_t):
            for ci in range(nch):
                s = s_scr[ci] * row(w_ref, ci, i) + sa_scr[ci] * row(b_ref, ci, i) \
                    + vc_scr[i * nch + ci] * row(k_ref, ci, i)
                s_scr[ci] = s
                if i + 1 < sub_t:
                    prepare(ci, s, row(a_ref, ci, i + 1))
                z = (s * row(r_ref, ci, i)).astype(BF16)
                zstage[pl.ds((i * (nch // 2) + ci // 2) * RW_HEAD, RW_HEAD),
                       pl.ds((ci % 2) * LANES, LANES)] = z
        yres[...] = jnp.dot(zstage[...], j_pair, preferred_element_type=F32)
        for i in range(sub_t):
            place = (lane % RW_HEAD) == tg * sub_t + i
            for ci in range(nch):
                ycol = yres[pl.ds((i * (nch // 2) + ci // 2) * RW_HEAD, RW_HEAD),
                            pl.ds((ci % 2) * LANES, LANES)]
                yacc_scr[ci] = jnp.where(place, ycol, yacc_scr[ci])
        return carry

    lax.fori_loop(0, SCAN_CHUNK // sub_t, group, 0)

    zeros = jnp.zeros((LANES - RW_HEAD, LANES), F32)
    for ci, (b, p) in enumerate(chains):
        cols = pl.ds(p * LANES, LANES)
        tr = jnp.concatenate([yacc_scr[ci], zeros], axis=0).T
        y_ref[b, :, cols] = jnp.where(first, tr[0:RW_HEAD], pltpu.roll(tr[RW_HEAD:], RW_HEAD, 1))
        sfin_ref[b, :, cols] = s_scr[ci]


def _rwkv_scan(r, w, k, v, a, b, s0, zero_chunks):
    nb, tp, _ = r.shape
    assert tp % SCAN_CHUNK == 0 and SCAN_CHUNK == RW_HEAD
    nch = nb * (RW_DIM // LANES)
    seq = pl.BlockSpec((nb, SCAN_CHUNK, RW_DIM), lambda i: (0, i, 0))
    st = pl.BlockSpec((nb, RW_HEAD, RW_DIM), lambda i: (0, 0, 0))
    return pl.pallas_call(
        functools.partial(_scan_kernel, nb=nb, zero_chunks=zero_chunks),
        grid=(tp // SCAN_CHUNK,),
        in_specs=[seq] * 6 + [st],
        out_specs=[seq, st],
        out_shape=[jax.ShapeDtypeStruct((nb, tp, RW_DIM), F32),
                   jax.ShapeDtypeStruct((nb, RW_HEAD, RW_DIM), F32)],
        scratch_shapes=[pltpu.VMEM((nch, RW_HEAD, LANES), F32),
                        pltpu.VMEM((nch, RW_HEAD, LANES), F32),
                        pltpu.VMEM((8 * nch, RW_HEAD, LANES), F32),
                        pltpu.VMEM((nch, RW_HEAD, LANES), F32),
                        pltpu.VMEM((8 * nch * RW_HEAD, 2 * LANES), BF16),
                        pltpu.VMEM((4 * nch * RW_HEAD, 2 * LANES), BF16),
                        pltpu.VMEM((4 * nch * RW_HEAD, 2 * LANES), F32)],
        compiler_params=_cparams(("arbitrary",)),
        name="rwkv_scan",
    )(r, w, k, v, a, b, s0)


def _lambda_value(lam_ref):
    lv = lam_ref[...]
    s01 = jnp.sum(lv[0:1, :] * lv[1:2, :], axis=-1, keepdims=True)
    s23 = jnp.sum(lv[2:3, :] * lv[3:4, :], axis=-1, keepdims=True)
    return jnp.exp(s01) - jnp.exp(s23) + LAM_INIT


def _alibi_slope(h):
    slope = jnp.float32(2.0 ** (-8.0 * DA_HEADS / DA_HEADS))
    for i in range(DA_HEADS - 2, -1, -1):
        slope = jnp.where(h == i, jnp.float32(2.0 ** (-8.0 * (i + 1) / DA_HEADS)), slope)
    return slope


def _sub_norm(o, g):
    return o * lax.rsqrt(jnp.mean(o * o, axis=-1, keepdims=True) + NORM_EPS) * g * (1.0 - LAM_INIT)


ATT_ONES = 16


def _attn_prompt_kernel(q_ref, k_ref, v_ref, d0_ref, lam_ref, sg_ref, o_ref,
                        qc_ref, vt_ref, bias_ref, m_ref, acc_ref, s_ref, *, blk):
    h = pl.program_id(1)
    qi = pl.program_id(2)
    slope = _alibi_slope(h)
    tp = k_ref.shape[0]
    sub = blk // LANES

    @pl.when(qi == 0)
    def _():
        for jb in range(tp // LANES):
            vt = v_ref[jb * LANES:(jb + 1) * LANES, :].T.astype(BF16)
            vt_ref[jb // sub, 0:DA_DV, (jb % sub) * LANES:(jb % sub + 1) * LANES] = vt
        vt_ref[:, DA_DV:, :] = jnp.ones((tp // blk, ATT_ONES, blk), BF16)

    lane = lax.broadcasted_iota(jnp.int32, (blk, LANES), 1)
    qb = q_ref[...].astype(BF16)
    for c in range(2):
        qc_ref[c] = jnp.where((lane // DA_DK) == c, qb, jnp.zeros_like(qb))
    bias_ref[...] = slope * d0_ref[...]
    m_ref[...] = jnp.full_like(m_ref, NEG_BIG)
    acc_ref[...] = jnp.zeros_like(acc_ref)

    def scores(j):
        kb = k_ref[pl.ds(pl.multiple_of(j * blk, blk), blk), :].astype(BF16)
        for c in range(2):
            s_ref[(j % 2) * 2 + c] = lax.dot_general(kb, qc_ref[c], (((1,), (1,)), ((), ())),
                                                     preferred_element_type=F32)

    def consume(j, masked):
        off = ((qi - j) * blk).astype(F32)
        c0 = slope * off
        if masked:
            kpos = j * blk + lax.broadcasted_iota(jnp.int32, (blk, blk), 0)
            valid = (d0_ref[...] + off >= 0.0) & (kpos >= PAD_LEAD)
        for c in range(2):
            t = s_ref[(j % 2) * 2 + c] - bias_ref[...]
            if masked:
                t = jnp.where(valid, t, NEG_BIG)
            m_old = m_ref[c]
            m_new = jnp.maximum(m_old, jnp.max(t, axis=0, keepdims=True) - c0)
            alpha = jnp.exp(m_old - m_new)
            p = jnp.exp(t - (m_new + c0)).astype(BF16)
            acc_ref[c] = alpha * acc_ref[c] + jnp.dot(vt_ref[j], p, preferred_element_type=F32)
            m_ref[c] = m_new

    def middle(j, carry):
        consume(j, False)
        scores(j + 1)
        return carry

    scores(0)
    consume(0, True)
    scores(jnp.minimum(1, tp // blk - 1))
    lax.fori_loop(1, qi, middle, 0)

    @pl.when(qi > 0)
    def _():
        consume(qi, True)

    lam = _lambda_value(lam_ref)
    a0 = acc_ref[0]
    a1 = acc_ref[1]
    ot = a0[:DA_DV] / a0[DA_DV:DA_DV + 1] - lam * (a1[:DA_DV] / a1[DA_DV:DA_DV + 1])
    o_ref[...] = _sub_norm(ot.T, sg_ref[...])


def _attn_prompt(q, k, v, da_lam, subln_g, blk):
    b, tp, _ = q.shape
    assert tp % blk == 0 and blk % LANES == 0 and PAD_LEAD <= blk
    d0t = (jnp.arange(blk, dtype=F32)[None, :] - jnp.arange(blk, dtype=F32)[:, None])
    qspec = pl.BlockSpec((None, blk, DA_DV), lambda bi, h, i: (bi, i, h))
    kspec = pl.BlockSpec((None, tp, DA_DV), lambda bi, h, i: (bi, 0, h))
    full = lambda a: pl.BlockSpec(a.shape, lambda bi, h, i: (0,) * a.ndim)
    return pl.pallas_call(
        functools.partial(_attn_prompt_kernel, blk=blk),
        grid=(b, DA_HEADS, tp // blk),
        in_specs=[qspec, kspec, kspec, full(d0t), full(da_lam), full(subln_g)],
        out_specs=qspec,
        scratch_shapes=[pltpu.VMEM((2, blk, DA_DV), BF16),
                        pltpu.VMEM((tp // blk, DA_DV + ATT_ONES, blk), BF16),
                        pltpu.VMEM((blk, blk), F32), pltpu.VMEM((2, 1, blk), F32),
                        pltpu.VMEM((2, DA_DV + ATT_ONES, blk), F32),
                        pltpu.VMEM((4, blk, blk), F32)],
        out_shape=jax.ShapeDtypeStruct((b, tp, DA_V), F32),
        compiler_params=_cparams(("parallel", "parallel", "arbitrary")),
        name="attn_prompt",
    )(q, k, v, d0t, da_lam, subln_g)


def _attn_sample_kernel(pt_ref, q_ref, kn_ref, vn_ref, lam_ref, sg_ref, *rest, ppg, page, past):
    k_refs = rest[:ppg]
    v_refs = rest[ppg:2 * ppg]
    o_ref, m_ref, l_ref, acc_ref = rest[2 * ppg:]
    g = pl.program_id(1)
    nrow = 2 * DA_HEADS
    rowi = lax.broadcasted_iota(jnp.int32, (nrow, DA_QK), 0)
    lanei = lax.broadcasted_iota(jnp.int32, (nrow, DA_QK), 1)
    qsel = (lanei // DA_DK) == rowi
    qmat = jnp.where(qsel, q_ref[...], 0.0)
    rh = lax.broadcasted_iota(jnp.int32, (nrow, 1), 0) // 2
    slope = jnp.full((nrow, 1), 2.0 ** (-8.0), F32)
    for i in range(DA_HEADS - 2, -1, -1):
        slope = jnp.where(rh == i, jnp.float32(2.0 ** (-8.0 * (i + 1) / DA_HEADS)), slope)

    @pl.when(g == 0)
    def _():
        m_ref[...] = jnp.full_like(m_ref, NEG_BIG)
        l_ref[...] = jnp.zeros_like(l_ref)
        acc_ref[...] = jnp.zeros_like(acc_ref)

    qb = qmat.astype(BF16)
    kpos = g * (ppg * page) + lax.broadcasted_iota(jnp.int32, (nrow, ppg * page), 1)
    s = jnp.concatenate([jnp.dot(qb, k_refs[i][...].astype(BF16), preferred_element_type=F32)
                         for i in range(ppg)], axis=1)
    s = s - slope * (past - kpos).astype(F32)
    m_old = m_ref[...]
    m_new = jnp.maximum(m_old, jnp.max(s, axis=-1, keepdims=True))
    alpha = jnp.exp(m_old - m_new)
    p = jnp.exp(s - m_new)
    l_ref[...] = alpha * l_ref[...] + jnp.sum(p, axis=-1, keepdims=True)
    pb = p.astype(BF16)
    pv = [jnp.zeros((nrow, DA_DV), F32)] * DA_HEADS
    for i in range(ppg):
        for hh in range(DA_HEADS):
            vh = v_refs[i][pl.ds(hh, page, stride=DA_HEADS), :].astype(BF16)
            pv[hh] = pv[hh] + jnp.dot(pb[:, i * page:(i + 1) * page], vh, preferred_element_type=F32)
    tot = pv[DA_HEADS - 1]
    for hh in range(DA_HEADS - 1):
        tot = jnp.where(rh == hh, pv[hh], tot)
    acc_ref[...] = alpha * acc_ref[...] + tot
    m_ref[...] = m_new

    @pl.when(g == pl.num_programs(1) - 1)
    def _():
        kn = kn_ref[...].astype(BF16).astype(F32)
        s_self = jnp.sum(qb.astype(F32) * kn, axis=-1, keepdims=True)
        m_old = m_ref[...]
        m_new = jnp.maximum(m_old, s_self)
        alpha = jnp.exp(m_old - m_new)
        p_self = jnp.exp(s_self - m_new)
        l_fin = alpha * l_ref[...] + p_self
        vn = vn_ref[...].astype(BF16).astype(F32)
        vrow = jnp.zeros((nrow, DA_DV), F32)
        for hh in range(DA_HEADS):
            vrow = jnp.where(rh == hh, vn[:, hh * DA_DV:(hh + 1) * DA_DV], vrow)
        acc = alpha * acc_ref[...] + p_self.astype(BF16).astype(F32) * vrow
        lam = _lambda_value(lam_ref)
        ri = lax.broadcasted_iota(jnp.int32, (nrow, 1), 0)
        t = acc / l_fin * jnp.where(ri % 2 == 0, 1.0, -lam)
        outs = []
        for hh in range(DA_HEADS):
            o = t[2 * hh:2 * hh + 1, :] + t[2 * hh + 1:2 * hh + 2, :]
            outs.append(_sub_norm(o, sg_ref[...]))
        o_ref[...] = jnp.concatenate(outs, axis=1)


def _attn_sample(q, kn, vn, cache_kt, cache_v, page_table, da_lam, subln_g, ppg):
    bd = q.shape[0]
    n_pages = page_table.shape[1]
    page = cache_kt.shape[2]
    assert n_pages % ppg == 0 and cache_v.shape[1:] == (page * DA_HEADS, DA_DV)
    pt = page_table.reshape(-1)
    one = pl.BlockSpec((None, 1, DA_QK), lambda b, g, pt: (b, 0, 0))
    full = lambda a: pl.BlockSpec(a.shape, lambda b, g, pt: (0,) * a.ndim)

    def pspec(i):
        return pl.BlockSpec((None, DA_QK, page),
                            lambda b, g, pt, i=i: (pt[b * n_pages + g * ppg + i], 0, 0))

    out = pl.pallas_call(
        functools.partial(_attn_sample_kernel, ppg=ppg, page=page, past=n_pages * page),
        grid_spec=pltpu.PrefetchScalarGridSpec(
            num_scalar_prefetch=1,
            grid=(bd, n_pages // ppg),
            in_specs=[one, one, one, full(da_lam), full(subln_g)]
                     + [pspec(i) for i in range(ppg)] * 2,
            out_specs=one,
            scratch_shapes=[pltpu.VMEM((2 * DA_HEADS, 1), F32), pltpu.VMEM((2 * DA_HEADS, 1), F32),
                            pltpu.VMEM((2 * DA_HEADS, DA_DV), F32)],
        ),
        out_shape=jax.ShapeDtypeStruct((bd, 1, DA_V), F32),
        compiler_params=_cparams(("parallel", "arbitrary")),
        name="attn_sample",
    )(pt, q.reshape(bd, 1, DA_QK), kn.reshape(bd, 1, DA_QK), vn.reshape(bd, 1, DA_V), da_lam, subln_g,
      *([cache_kt] * ppg), *([cache_v] * ppg))
    return out.reshape(bd, DA_V)


def _merge_kernel(*refs, nsub):
    rows = [refs[k * nsub:(k + 1) * nsub] for k in range(6)]
    (lng_ref, lnb_ref, wrw_ref, wda_ref, wout_ref, n2_ref, wq_ref, sk_ref,
     h2_ref, xn_ref, st_ref) = refs[6 * nsub:]
    stack = lambda rs: rs[0][...] if nsub == 1 else jnp.concatenate([r[...] for r in rs], axis=0)
    y, bonus, g, o, gt, h = [stack(rs) for rs in rows]
    j = _seg_ones()
    mu = _seg64(y, j) * (1.0 / RW_HEAD)
    d = y - mu
    var = _seg64(d * d, j) * (1.0 / RW_HEAD)
    yln = d * lax.rsqrt(var + RW_LN_EPS) * lng_ref[...] + lnb_ref[...]
    yr = ((yln + bonus) * g).astype(BF16)
    y_rw = jnp.dot(yr, wrw_ref[...], preferred_element_type=F32)
    y_da = jnp.dot(o.astype(BF16), wda_ref[...], preferred_element_type=F32)
    mix = gt[:, :D_MODEL] * y_rw + gt[:, D_MODEL:] * y_da
    h2 = h + jnp.dot(mix.astype(BF16), wout_ref[...], preferred_element_type=F32)
    h2_ref[...] = h2
    ms = jnp.mean(h2 * h2, axis=-1, keepdims=True)
    xn = (h2 * lax.rsqrt(ms + NORM_EPS) * n2_ref[...]).astype(BF16)
    xn_ref[...] = xn
    q = jnp.dot(xn, wq_ref[...], preferred_element_type=F32)
    for hc in range(2 * PEER_HEADS):
        qc = q[:, hc * PEER_DHALF:(hc + 1) * PEER_DHALF].astype(BF16)
        st_ref[hc * N_KEYS:(hc + 1) * N_KEYS, :] = lax.dot_general(
            sk_ref[hc % 2], qc, (((1,), (1,)), ((), ())), preferred_element_type=F32)


def _merge(y, bonus, g, o, gates, h, consts, tm, tile_off, nsub):
    b, rows, _ = y.shape
    assert rows % tm == 0 and (rows // tm - tile_off) % nsub == 0
    nt = (rows // tm - tile_off) // nsub
    full = lambda a: pl.BlockSpec(a.shape, lambda i, t: (0,) * a.ndim)

    def rin(c):
        return [pl.BlockSpec((None, tm, c), lambda i, t, k=k: (i, t * nsub + k + tile_off, 0))
                for k in range(nsub)]

    row_arrays = (y, bonus, g, o, gates, h)
    tmo = tm * nsub
    ntok = b * nt * tmo
    return pl.pallas_call(
        functools.partial(_merge_kernel, nsub=nsub),
        grid=(b, nt),
        in_specs=[s for a in row_arrays for s in rin(a.shape[-1])] + [full(a) for a in consts],
        out_specs=[pl.BlockSpec((tmo, D_MODEL), lambda i, t: (i * nt + t, 0)),
                   pl.BlockSpec((tmo, D_MODEL), lambda i, t: (i * nt + t, 0)),
                   pl.BlockSpec((2 * PEER_HEADS * N_KEYS, tmo), lambda i, t: (0, i * nt + t))],
        out_shape=[jax.ShapeDtypeStruct((ntok, D_MODEL), F32),
                   jax.ShapeDtypeStruct((ntok, D_MODEL), BF16),
                   jax.ShapeDtypeStruct((2 * PEER_HEADS * N_KEYS, ntok), F32)],
        compiler_params=_cparams(("parallel", "parallel")),
        name="merge",
    )(*[a for a in row_arrays for _ in range(nsub)], *consts)


def _count(mask):
    return jnp.sum(jnp.where(mask, 1.0, 0.0), axis=0, keepdims=True)


def _top16(x, exact):
    nk = x.shape[0]
    iota = lax.broadcasted_iota(jnp.int32, x.shape, 0).astype(F32)
    rank = jnp.full(x.shape, float(nk), F32)
    vals = []
    for r in range(PEER_TOPK):
        m = jnp.max(x, axis=0, keepdims=True)
        sel = x == m
        if exact:
            cand = jnp.where(sel, iota, float(nk))
            sel = cand == jnp.min(cand, axis=0, keepdims=True)
        x = jnp.where(sel, -jnp.inf, x)
        rank = jnp.where(sel, float(r), rank)
        vals.append(m)
    return vals, rank


def _peer_topk_kernel(st_ref, rank2_ref, c2_ref, cnt1_ref, c1_ref):
    n = st_ref.shape[1]
    rid = lax.broadcasted_iota(jnp.int32, (_CAND_ROWS, n), 0)
    flat = jnp.zeros((_CAND_ROWS, n), jnp.int32) + PEER_TOPK * PEER_TOPK
    irow = jnp.zeros((_CAND_ROWS, n), jnp.int32) + PEER_TOPK
    for ci, (i, j) in enumerate(_CAND):
        flat = jnp.where(rid == ci, i * PEER_TOPK + j, flat)
        irow = jnp.where(rid == ci, i, irow)
    outs = (rank2_ref, c2_ref, cnt1_ref, c1_ref)
    ties = _peer_topk_body(st_ref, outs, flat, irow, exact=False)

    @pl.when(jnp.max(ties) > 0.0)
    def _():
        _peer_topk_body(st_ref, outs, flat, irow, exact=True)


def _peer_topk_body(st_ref, outs, flat, irow, exact):
    rank2_ref, c2_ref, cnt1_ref, c1_ref = outs
    n = st_ref.shape[1]
    ties = jnp.zeros((1, n), F32)
    for h in range(PEER_HEADS):
        s1 = st_ref[(2 * h) * N_KEYS:(2 * h + 1) * N_KEYS, :]
        s2 = st_ref[(2 * h + 1) * N_KEYS:(2 * h + 2) * N_KEYS, :]
        v1, rank1 = _top16(s1, exact)
        v2, rank2 = _top16(s2, exact)
        rows = [v1[i] + v2[j] for (i, j) in _CAND]
        rows += [jnp.full((1, n), -jnp.inf, F32)] * (_CAND_ROWS - len(_CAND))
        cand = jnp.concatenate(rows, axis=0)
        top = v1[0] + v2[0]
        chosen = jnp.zeros(cand.shape, jnp.bool_)
        work = cand
        for _ in range(PEER_TOPK):
            m = jnp.max(work, axis=0, keepdims=True)
            sel = work == m
            if exact:
                fl = jnp.where(sel, flat, PEER_TOPK * PEER_TOPK + 1)
                sel = fl == jnp.min(fl, axis=0, keepdims=True)
            work = jnp.where(sel, -jnp.inf, work)
            chosen = chosen | sel
        if not exact:
            k = float(PEER_TOPK)
            bad = (_count(rank1 < k) != k) | (_count(rank2 < k) != k) | (_count(chosen) != k)
            ties = jnp.maximum(ties, jnp.where(bad, 1.0, 0.0))
        z = jnp.sum(jnp.where(chosen, jnp.exp(cand - top), 0.0), axis=0, keepdims=True)
        rank1_b = rank1.astype(BF16)
        cnt1 = jnp.zeros(s1.shape, BF16)
        for i in range(PEER_TOPK):
            m_i = jnp.sum(jnp.where(chosen & (irow == i), 1.0, 0.0), axis=0, keepdims=True)
            cnt1 = jnp.where(rank1_b == float(i), m_i.astype(BF16), cnt1)
        cnt1_ref[h] = cnt1.astype(F32)
        c1_ref[h] = jnp.where(rank1 < float(PEER_TOPK), jnp.exp(s1 - v1[0]) / z * 0.5, 0.0)
        c2_ref[h] = jnp.where(rank2 < float(PEER_TOPK), jnp.exp(s2 - v2[0]), 0.0).astype(BF16)
        rank2_ref[h] = rank2.astype(BF16)
    return ties


def _peer_topk(st, tn):
    ntok = st.shape[1]
    assert ntok % tn == 0
    spec = pl.BlockSpec((PEER_HEADS, N_KEYS, tn), lambda i: (0, 0, i))
    return pl.pallas_call(
        _peer_topk_kernel,
        grid=(ntok // tn,),
        in_specs=[pl.BlockSpec((st.shape[0], tn), lambda i: (0, i))],
        out_specs=[spec] * 4,
        out_shape=[jax.ShapeDtypeStruct((PEER_HEADS, N_KEYS, ntok), dt) for dt in (BF16, BF16, F32, F32)],
        compiler_params=_cparams(("parallel",)),
        name="peer_topk",
    )(st)


def _peer_dense_kernel(x_ref, u_ref, vt_ref, rank2_ref, c2_ref, cnt1_ref, c1_ref, h2_ref, y_ref,
                       acc_ref, hid_ref, p_ref, *, e1b):
    j = pl.program_id(1)
    tn = x_ref.shape[0]
    ngrp = N_KEYS // BF16_ROWS

    @pl.when(j == 0)
    def _():
        acc_ref[...] = jnp.zeros_like(acc_ref)

    hid_ref[...] = lax.dot_general(u_ref[...], x_ref[...], (((1,), (1,)), ((), ())),
                                   preferred_element_type=F32)

    for e in range(e1b):
        e1 = j * e1b + e
        gate = [None] * ngrp
        for h in range(PEER_HEADS):
            cnt = jnp.broadcast_to(cnt1_ref[h, pl.ds(e1, 1), :], (BF16_ROWS, tn)).astype(BF16)
            c1 = jnp.broadcast_to(c1_ref[h, pl.ds(e1, 1), :], (BF16_ROWS, tn)).astype(BF16)
            for g in range(ngrp):
                rows = slice(g * BF16_ROWS, (g + 1) * BF16_ROWS)
                c2 = c2_ref[h, rows, :]
                term = jnp.where(rank2_ref[h, rows, :] < cnt, c2, jnp.zeros_like(c2)) * c1
                gate[g] = term if gate[g] is None else gate[g] + term
        hh = hid_ref[e * N_KEYS:(e + 1) * N_KEYS, :]
        gelu = (hh * (1.0 + lax.erf(hh * (2.0 ** -0.5)))).astype(BF16)
        for g in range(ngrp):
            r0 = e * N_KEYS + g * BF16_ROWS
            p_ref[r0:r0 + BF16_ROWS, :] = gelu[g * BF16_ROWS:(g + 1) * BF16_ROWS] * gate[g]
    acc_ref[...] += jnp.dot(vt_ref[...], p_ref[...], preferred_element_type=F32)

    @pl.when(j == pl.num_programs(1) - 1)
    def _():
        y_ref[...] = h2_ref[...] + acc_ref[...].T


def _peer_dense(xn, u_bf, vt_bf, rank2, c2, cnt1, c1, h2, tn, e1b):
    ntok = xn.shape[0]
    assert ntok % tn == 0 and N_KEYS % e1b == 0
    tok = pl.BlockSpec((PEER_HEADS, N_KEYS, tn), lambda i, j: (0, 0, i))
    row = pl.BlockSpec((tn, D_MODEL), lambda i, j: (i, 0))
    return pl.pallas_call(
        functools.partial(_peer_dense_kernel, e1b=e1b),
        grid=(ntok // tn, N_KEYS // e1b),
        in_specs=[row,
                  pl.BlockSpec((e1b * N_KEYS, D_MODEL), lambda i, j: (j, 0)),
                  pl.BlockSpec((D_MODEL, e1b * N_KEYS), lambda i, j: (0, j)),
                  tok, tok, tok, tok, row],
        out_specs=row,
        out_shape=jax.ShapeDtypeStruct((ntok, D_MODEL), F32),
        scratch_shapes=[pltpu.VMEM((D_MODEL, tn), F32), pltpu.VMEM((e1b * N_KEYS, tn), F32),
                        pltpu.VMEM((e1b * N_KEYS, tn), BF16)],
        compiler_params=_cparams(("parallel", "arbitrary")),
        name="peer_dense",
    )(xn, u_bf, vt_bf, rank2, c2, cnt1, c1, h2)


def _state_in(s):
    b = s.shape[0]
    return jnp.transpose(s, (0, 2, 1, 3)).reshape(b, RW_HEAD, RW_DIM)


def _state_out(s):
    b = s.shape[0]
    return jnp.transpose(s.reshape(b, RW_HEAD, RW_HEADS, RW_HEAD), (0, 2, 1, 3))


def _pad_tokens(a, n, axis):
    pad = [(0, 0)] * a.ndim
    pad[axis] = (0, n - a.shape[axis])
    return jnp.pad(a, pad)


def kernel(x_prompt, x_sample, cache_k, cache_v, state_wkv, state_shift, page_table, meta, norm1_g, w_in,
           rw_mu, rw_w0, rw_w2, rw_a0, rw_a2, rw_g2, rw_kk, rw_ka, rw_rk, rw_ln_g, rw_ln_b, w_rw_br,
           da_qn_g, da_kn_g, da_lam, da_subln_g, w_da_br, w_out, norm2_g, peer_wq, peer_subkeys,
           peer_u, peer_v):
    b, seq, _ = x_prompt.shape
    bd = x_sample.shape[0]
    t_real = N_META + seq
    tp = PAD_LEAD + t_real
    assert tp % ROW_TILE == 0 and tp % SCAN_CHUNK == 0 and x_sample.shape[1] == 1
    row2 = lambda a: a.reshape(1, -1)

    w_in_bf = w_in[0].astype(BF16)
    g1 = row2(norm1_g[0])
    qg = row2(jnp.tile(da_qn_g[0], DA_QK // DA_DK))
    kg = row2(jnp.tile(da_kn_g[0], DA_QK // DA_DK))
    zlo = jnp.zeros((W_LORA, RW_DIM), F32)
    prep_params = (row2(rw_mu[0]), row2(rw_w0[0]),
                   jnp.concatenate([rw_w2[0], zlo], axis=0).astype(BF16), row2(rw_a0[0]),
                   jnp.concatenate([zlo, rw_a2[0]], axis=0).astype(BF16), rw_g2[0].astype(BF16),
                   row2(rw_kk[0]), row2(rw_ka[0]), row2(rw_rk[0]))
    merge_consts = (row2(rw_ln_g[0]), row2(rw_ln_b[0]), w_rw_br[0].astype(BF16), w_da_br[0].astype(BF16),
                    w_out[0].astype(BF16), row2(norm2_g[0]), peer_wq[0].astype(BF16),
                    peer_subkeys[0].astype(BF16))
    subln = row2(da_subln_g[0])
    u_bf = peer_u[0].astype(BF16)
    vt_bf = peer_v[0].astype(BF16).T

    hp = jnp.concatenate([jnp.zeros((b, PAD_LEAD, D_MODEL), F32),
                          jnp.broadcast_to(meta[None], (b, N_META, D_MODEL)), x_prompt], axis=1)
    prw, q, k, v, gates = _in_proj(hp.reshape(b * tp, D_MODEL), g1, w_in_bf, qg, kg, IN_PROJ_ROWS)
    r3 = lambda a: a.reshape(b, tp, -1)
    pr, pw, pk, pv, pa, pb, pg, pbonus = _rwkv_prep_prompt(r3(prw), prep_params, PREP_ROWS)
    y_scan, s_fin = _rwkv_scan(pr, pw, pk, pv, pa, pb, jnp.zeros((b, RW_HEAD, RW_DIM), F32),
                               PAD_LEAD // SCAN_CHUNK)
    o_attn = _attn_prompt(r3(q), r3(k), r3(v), da_lam[0], subln, ATTN_BLOCK)
    h2, xn2, st = _merge(y_scan, pbonus, pg, o_attn, r3(gates), hp, merge_consts, ROW_TILE, 1, MERGE_STACK)
    y_prompt = _peer_dense(xn2, u_bf, vt_bf, *_peer_topk(st, TOPK_TOKENS), h2, DENSE_TOKENS, DENSE_E1)

    xs = x_sample.reshape(bd, D_MODEL)
    prw_s, q_s, k_s, v_s, gates_s = _in_proj(xs, g1, w_in_bf, qg, kg, bd)
    sr, sw, sk, sv, sa, sb, sg, sbonus = _rwkv_prep_sample(prw_s, state_shift[0], prep_params)
    ys_scan, ss_fin = _rwkv_step(sr, sw, sk, sv, sa, sb, _state_in(state_wkv[0]))
    n_phys = cache_k.shape[1]
    ckt = jnp.transpose(cache_k[0].reshape(n_phys, -1, DA_QK), (0, 2, 1))
    cv = cache_v[0].reshape(n_phys, -1, DA_DV)
    o_s = _attn_sample(q_s, k_s, v_s, ckt, cv, page_table, da_lam[0], subln, PAGES_PER_STEP)
    f3 = lambda a: a.reshape(1, bd, -1)
    h2s, xn2s, sts = _merge(ys_scan.reshape(1, bd, RW_DIM), f3(sbonus), f3(sg), f3(o_s), f3(gates_s),
                            f3(xs), merge_consts, bd, 0, 1)
    npad = LANES
    tk_s = _peer_topk(_pad_tokens(sts, npad, 1), npad)
    y_s = _peer_dense(_pad_tokens(xn2s, npad, 0), u_bf, vt_bf, *tk_s, _pad_tokens(h2s, npad, 0), npad, DENSE_E1)

    y_prompt = y_prompt.reshape(b, seq, D_MODEL)
    y_sample = y_s[:bd].reshape(bd, 1, D_MODEL)
    k_p = r3(k)[:, PAD_LEAD:].reshape(1, b, t_real, DA_HEADS, 2, DA_DK)
    v_p = r3(v)[:, PAD_LEAD:].reshape(1, b, t_real, DA_HEADS, DA_DV)
    return (y_prompt, y_sample, k_p, v_p,
            k_s.reshape(1, bd, 1, DA_HEADS, 2, DA_DK), v_s.reshape(1, bd, 1, DA_HEADS, DA_DV),
            _state_out(s_fin)[None], _state_out(ss_fin)[None],
            r3(prw)[:, -1][None], prw_s[None])
```
